```python
import jax
import jax.numpy as jnp
from jax import lax
import numpy as np

D_MODEL = 1024
BATCH = 4
SEQ = 8192
DEPTH = 4

HEAD_DIM = 64
MIX_HEADS = 12
MIX_WIDTH = MIX_HEADS * HEAD_DIM
MEM_HEADS = 4
MEM_WIDTH = MEM_HEADS * HEAD_DIM
MEM_LEN = 256
CONV_WIDTH = 3
ROPE_DIM = HEAD_DIM // 4
ROPE_THETA = 500000.0
MOBA_BLOCK = 256
MOBA_TOPK = 3
MOBA_Q_CHUNK = 16
D_FF = 2816
N_EXPERTS = 8
TOP_K = 2
LN_EPS = 1e-5
N_A_LAYERS = DEPTH // 2
N_B_LAYERS = DEPTH - N_A_LAYERS
N_DENSE = (DEPTH + 1) // 2
N_MOE = DEPTH // 2
DEEPNORM_ALPHA = float((2 * DEPTH) ** 0.25)
DEEPNORM_BETA = float((8 * DEPTH) ** -0.25)
ATTN_SCALE = HEAD_DIM ** -0.5

kernel_name = 'hybrid_shortconv_moba_yoco_deepnorm_moe'


def layer_norm(x, g, b):
    xf = x.astype(jnp.float32)
    mu = jnp.mean(xf, axis=-1, keepdims=True)
    var = jnp.mean(jnp.square(xf - mu), axis=-1, keepdims=True)
    y = (xf - mu) * lax.rsqrt(var + LN_EPS)
    return (y * g + b).astype(x.dtype)


def rope_tables(positions):
    inv_freq = ROPE_THETA ** (-jnp.arange(0, ROPE_DIM, 2, dtype=jnp.float32) / ROPE_DIM)
    angles = positions.astype(jnp.float32)[..., None] * inv_freq
    return jnp.cos(angles)[:, :, None, :], jnp.sin(angles)[:, :, None, :]


def apply_partial_rope(t, cos, sin):
    half = ROPE_DIM // 2
    t1 = t[..., :half].astype(jnp.float32)
    t2 = t[..., half:ROPE_DIM].astype(jnp.float32)
    rot = jnp.concatenate([t1 * cos - t2 * sin, t2 * cos + t1 * sin], axis=-1).astype(t.dtype)
    return jnp.concatenate([rot, t[..., ROPE_DIM:]], axis=-1)


def short_conv_mixer(u_b, u_c, u_h, conv_w):
    z = u_c * u_h
    z = lax.conv_general_dilated(
        z, conv_w[:, None, :], window_strides=(1,), padding=[(CONV_WIDTH - 1, 0)],
        dimension_numbers=('NWC', 'WIO', 'NWC'), feature_group_count=MIX_WIDTH)
    return u_b * z


def memory_cross_attention(q, k, v):
    logits = jnp.einsum('bshd,bmhd->bhsm', q, k).astype(jnp.float32) * ATTN_SCALE
    probs = jax.nn.softmax(logits, axis=-1).astype(v.dtype)
    out = jnp.einsum('bhsm,bmhd->bshd', probs, v)
    return out.reshape(q.shape[0], q.shape[1], -1)


def shared_key_value(h, w_kv, cos, sin):
    bsz, seq, _ = h.shape
    kv = jnp.einsum('bsd,de->bse', h, w_kv)
    k = apply_partial_rope(kv[..., :MIX_WIDTH].reshape(bsz, seq, MIX_HEADS, HEAD_DIM), cos, sin)
    v = kv[..., MIX_WIDTH:].reshape(bsz, seq, MIX_HEADS, HEAD_DIM)
    pad = (-seq) % MOBA_BLOCK
    k = jnp.pad(k, ((0, 0), (0, pad), (0, 0), (0, 0)))
    v = jnp.pad(v, ((0, 0), (0, pad), (0, 0), (0, 0)))
    n_blocks = (seq + pad) // MOBA_BLOCK
    k_blocks = k.reshape(bsz, n_blocks, MOBA_BLOCK, MIX_HEADS, HEAD_DIM).transpose(0, 3, 1, 2, 4)
    v_blocks = v.reshape(bsz, n_blocks, MOBA_BLOCK, MIX_HEADS, HEAD_DIM).transpose(0, 3, 1, 2, 4)
    k_means = jnp.mean(k_blocks.astype(jnp.float32), axis=3).astype(k.dtype)
    return k_blocks, v_blocks, k_means


def moba_attention(q, k_blocks, v_blocks, k_means):
    bsz, seq, n_heads, head_dim = q.shape
    n_blocks = k_blocks.shape[2]
    topk = min(MOBA_TOPK, n_blocks)
    n_sel = topk * MOBA_BLOCK
    n_chunks = seq // MOBA_Q_CHUNK
    q_chunks = q.reshape(bsz, n_chunks, MOBA_Q_CHUNK, n_heads, head_dim).transpose(1, 0, 3, 2, 4)
    b_idx = jnp.arange(bsz)[:, None, None, None]
    h_idx = jnp.arange(n_heads)[None, :, None, None]
    block_ids = jnp.arange(n_blocks)

    def one_chunk(args):
        c, qc = args
        start = c * MOBA_Q_CHUNK
        own = start // MOBA_BLOCK
        q_pos = start + jnp.arange(MOBA_Q_CHUNK)
        gate = jnp.einsum('bhqd,bhnd->bhqn', qc, k_means).astype(jnp.float32)
        gate = jnp.where(block_ids < own, gate, -jnp.inf)
        _, sel = lax.top_k(gate, topk)
        sel_valid = sel < own
        k_sel = k_blocks[b_idx, h_idx, sel]
        v_sel = v_blocks[b_idx, h_idx, sel]
        k_own = lax.dynamic_index_in_dim(k_blocks, own, axis=2, keepdims=False)
        v_own = lax.dynamic_index_in_dim(v_blocks, own, axis=2, keepdims=False)
        s_sel = jnp.einsum('bhqd,bhqkjd->bhqkj', qc, k_sel).astype(jnp.float32) * ATTN_SCALE
        s_sel = jnp.where(sel_valid[..., None], s_sel, -jnp.inf)
        s_own = jnp.einsum('bhqd,bhjd->bhqj', qc, k_own).astype(jnp.float32) * ATTN_SCALE
        k_pos = own * MOBA_BLOCK + jnp.arange(MOBA_BLOCK)
        s_own = jnp.where(k_pos[None, :] <= q_pos[:, None], s_own, -jnp.inf)
        scores = jnp.concatenate([s_sel.reshape(bsz, n_heads, MOBA_Q_CHUNK, n_sel), s_own], axis=-1)
        probs = jax.nn.softmax(scores, axis=-1).astype(qc.dtype)
        p_sel = probs[..., :n_sel].reshape(bsz, n_heads, MOBA_Q_CHUNK, topk, MOBA_BLOCK)
        p_own = probs[..., n_sel:]
        return (jnp.einsum('bhqkj,bhqkjd->bhqd', p_sel, v_sel)
                + jnp.einsum('bhqj,bhjd->bhqd', p_own, v_own))

    out = lax.map(one_chunk, (jnp.arange(n_chunks), q_chunks))
    return out.transpose(1, 0, 3, 2, 4).reshape(bsz, seq, n_heads * head_dim)


def swiglu(x, w_gu, w_down):
    gu = jnp.einsum('bsd,df->bsf', x, w_gu)
    g, u = gu[..., :D_FF], gu[..., D_FF:]
    return jnp.einsum('bsf,fd->bsd', jax.nn.silu(g) * u, w_down)


def moe_swiglu(x, w_router, w_gu, w_down):
    logits = jnp.einsum('bsd,de->bse', x, w_router).astype(jnp.float32)
    top_logits, top_idx = lax.top_k(logits, TOP_K)
    top_gates = jax.nn.softmax(top_logits, axis=-1)
    gates = jnp.sum(jax.nn.one_hot(top_idx, N_EXPERTS, dtype=jnp.float32) * top_gates[..., None], axis=-2)
    gates = gates.astype(x.dtype)
    out = jnp.zeros_like(x)
    for e in range(N_EXPERTS):
        out = out + gates[..., e:e + 1] * swiglu(x, w_gu[e], w_down[e])
    return out


def setup_inputs(seed: int = 0) -> dict:
    key = jax.random.key(seed)
    ks = jax.random.split(key, 20)
    f32 = jnp.float32

    def nrm(k, shape, scale):
        return jax.random.normal(k, shape, f32) * scale

    x = nrm(ks[0], (BATCH, SEQ, D_MODEL), 1.0)
    mem = nrm(ks[1], (BATCH, MEM_LEN, D_MODEL), 1.0)
    offset = jax.random.randint(ks[2], (BATCH, 1), 0, 1024, dtype=jnp.int32)
    positions = offset + jnp.arange(SEQ, dtype=jnp.int32)[None, :]
    w_in_a = nrm(ks[3], (N_A_LAYERS, D_MODEL, 3 * MIX_WIDTH + MEM_WIDTH), D_MODEL ** -0.5)
    conv_a = nrm(ks[4], (N_A_LAYERS, CONV_WIDTH, MIX_WIDTH), CONV_WIDTH ** -0.5)
    w_q_b = nrm(ks[5], (N_B_LAYERS, D_MODEL, MIX_WIDTH + MEM_WIDTH), D_MODEL ** -0.5)
    w_kv_shared = nrm(ks[6], (D_MODEL, 2 * MIX_WIDTH), D_MODEL ** -0.5)
    w_mem_kv = nrm(ks[7], (DEPTH, D_MODEL, 2 * MEM_WIDTH), D_MODEL ** -0.5)
    w_o = nrm(ks[8], (DEPTH, MIX_WIDTH + MEM_WIDTH, D_MODEL), (MIX_WIDTH + MEM_WIDTH) ** -0.5 * DEEPNORM_BETA)
    ln1_g = 1.0 + nrm(ks[9], (DEPTH, D_MODEL), 0.02)
    ln1_b = nrm(ks[10], (DEPTH, D_MODEL), 0.02)
    ln2_g = 1.0 + nrm(ks[11], (DEPTH, D_MODEL), 0.02)
    ln2_b = nrm(ks[12], (DEPTH, D_MODEL), 0.02)
    w_gu_dense = nrm(ks[13], (N_DENSE, D_MODEL, 2 * D_FF), D_MODEL ** -0.5)
    w_down_dense = nrm(ks[14], (N_DENSE, D_FF, D_MODEL), D_FF ** -0.5 * DEEPNORM_BETA)
    w_router = nrm(ks[15], (N_MOE, D_MODEL, N_EXPERTS), D_MODEL ** -0.5)
    w_gu_moe = nrm(ks[16], (N_MOE, N_EXPERTS, D_MODEL, 2 * D_FF), D_MODEL ** -0.5)
    w_down_moe = nrm(ks[17], (N_MOE, N_EXPERTS, D_FF, D_MODEL), D_FF ** -0.5 * DEEPNORM_BETA)
    return {'x': x, 'mem': mem, 'positions': positions, 'w_in_a': w_in_a, 'conv_a': conv_a,
            'w_q_b': w_q_b, 'w_kv_shared': w_kv_shared, 'w_mem_kv': w_mem_kv, 'w_o': w_o,
            'ln1_g': ln1_g, 'ln1_b': ln1_b, 'ln2_g': ln2_g, 'ln2_b': ln2_b,
            'w_gu_dense': w_gu_dense, 'w_down_dense': w_down_dense, 'w_router': w_router,
            'w_gu_moe': w_gu_moe, 'w_down_moe': w_down_moe}


def reference(x, mem, positions, w_in_a, conv_a, w_q_b, w_kv_shared, w_mem_kv, w_o,
              ln1_g, ln1_b, ln2_g, ln2_b, w_gu_dense, w_down_dense, w_router,
              w_gu_moe, w_down_moe):
    bsz, seq, _ = x.shape
    cos, sin = rope_tables(positions)
    shared = None
    for layer in range(DEPTH):
        mem_kv = jnp.einsum('bmd,de->bme', mem, w_mem_kv[layer])
        mem_k = mem_kv[..., :MEM_WIDTH].reshape(bsz, -1, MEM_HEADS, HEAD_DIM)
        mem_v = mem_kv[..., MEM_WIDTH:].reshape(bsz, -1, MEM_HEADS, HEAD_DIM)
        if layer < N_A_LAYERS:
            proj = jnp.einsum('bsd,de->bse', x, w_in_a[layer])
            u_b = proj[..., :MIX_WIDTH]
            u_c = proj[..., MIX_WIDTH:2 * MIX_WIDTH]
            u_h = proj[..., 2 * MIX_WIDTH:3 * MIX_WIDTH]
            q_mem = proj[..., 3 * MIX_WIDTH:]
            main = short_conv_mixer(u_b, u_c, u_h, conv_a[layer])
        else:
            proj = jnp.einsum('bsd,de->bse', x, w_q_b[layer - N_A_LAYERS])
            q = apply_partial_rope(proj[..., :MIX_WIDTH].reshape(bsz, seq, MIX_HEADS, HEAD_DIM), cos, sin)
            q_mem = proj[..., MIX_WIDTH:]
            k_blocks, v_blocks, k_means = shared
            main = moba_attention(q, k_blocks, v_blocks, k_means)
        mem_out = memory_cross_attention(q_mem.reshape(bsz, seq, MEM_HEADS, HEAD_DIM), mem_k, mem_v)
        mix = jnp.einsum('bse,ed->bsd', jnp.concatenate([main, mem_out], axis=-1), w_o[layer])
        x = layer_norm(DEEPNORM_ALPHA * x + mix, ln1_g[layer], ln1_b[layer])
        if layer % 2 == 0:
            ffn = swiglu(x, w_gu_dense[layer // 2], w_down_dense[layer // 2])
        else:
            ffn = moe_swiglu(x, w_router[layer // 2], w_gu_moe[layer // 2], w_down_moe[layer // 2])
        x = layer_norm(DEEPNORM_ALPHA * x + ffn, ln2_g[layer], ln2_b[layer])
        if layer == N_A_LAYERS - 1:
            shared = shared_key_value(x, w_kv_shared, cos, sin)
    return x
```

```python
import functools

import jax
import jax.numpy as jnp
from jax import lax
from jax.experimental import pallas as pl
from jax.experimental.pallas import tpu as pltpu

F32 = jnp.float32
BF16 = jnp.bfloat16

HEAD_DIM = 64
MIX_HEADS = 12
MIX_WIDTH = MIX_HEADS * HEAD_DIM
MEM_HEADS = 4
MEM_WIDTH = MEM_HEADS * HEAD_DIM
CONV_WIDTH = 3
ROPE_DIM = HEAD_DIM // 4
ROPE_THETA = 500000.0
MOBA_BLOCK = 256
MOBA_TOPK = 3
N_EXPERTS = 8
LN_EPS = 1e-5
ATTN_SCALE = HEAD_DIM ** -0.5

LANES = 128
HEADS_PER_LANE_GROUP = LANES // HEAD_DIM
MASK_VALUE = -1e30
VMEM_LIMIT_BYTES = 56 * 1024 * 1024

SEQ_TILE = 512
FFN_TILE = 1024
FF_CHUNK = 1408


def _params(semantics):
    return pltpu.CompilerParams(dimension_semantics=semantics, vmem_limit_bytes=VMEM_LIMIT_BYTES)


def _layer_norm(r, g, b):
    mu = jnp.mean(r, axis=-1, keepdims=True)
    c = r - mu
    var = jnp.mean(c * c, axis=-1, keepdims=True)
    return c * lax.rsqrt(var + LN_EPS) * g + b


def _dot(a, b):
    return jnp.dot(a, b, preferred_element_type=F32)


def _dot_nt(a, b):
    return lax.dot_general(a, b, (((1,), (1,)), ((), ())), preferred_element_type=F32)


def _head_lane_mask(width, head):
    lane = lax.broadcasted_iota(jnp.int32, (1, width), 1)
    return (lane >= head * HEAD_DIM) & (lane < (head + 1) * HEAD_DIM)


def _mem_attention(q_mem, mem_k, mem_v):
    out = jnp.zeros(q_mem.shape, F32)
    for head in range(MEM_HEADS):
        hm = _head_lane_mask(MEM_WIDTH, head)
        qh = jnp.where(hm, q_mem * ATTN_SCALE, 0.0).astype(BF16)
        s = _dot_nt(qh, mem_k)
        m = jnp.max(s, axis=-1, keepdims=True)
        p = jnp.exp(s - m)
        l = jnp.sum(p, axis=-1, keepdims=True)
        o = _dot(p.astype(BF16), mem_v) / l
        out = jnp.where(hm, o, out)
    return out


def _rope_tables(pos, freq, reps):
    angle = pos * freq
    cos = jnp.cos(angle)
    sin = jnp.sin(angle)
    return (jnp.concatenate([cos] * reps, axis=1), jnp.concatenate([sin] * reps, axis=1))


def _matmul_kernel(x_ref, w_ref, o_ref):
    o_ref[...] = _dot(x_ref[...].astype(BF16), w_ref[...]).astype(o_ref.dtype)


def _matmul(x, w, out_dtype, tile_m=256):
    m, k = x.shape
    n = w.shape[1]
    return pl.pallas_call(
        _matmul_kernel,
        grid=(m // tile_m,),
        in_specs=[pl.BlockSpec((tile_m, k), lambda i: (i, 0)),
                  pl.BlockSpec((k, n), lambda i: (0, 0))],
        out_specs=pl.BlockSpec((tile_m, n), lambda i: (i, 0)),
        out_shape=jax.ShapeDtypeStruct((m, n), out_dtype),
        compiler_params=_params(("arbitrary",)),
        name="mem_kv_proj",
    )(x, w)


def _mixer_a_kernel(x_ref, win_ref, conv_ref, mk_ref, mv_ref, wo_ref, g_ref, b_ref, o_ref,
                    ztail_ref, *, alpha):
    tile = x_ref.shape[1]

    @pl.when(pl.program_id(1) == 0)
    def _():
        ztail_ref[...] = jnp.zeros_like(ztail_ref)

    x = x_ref[0]
    proj = _dot(x.astype(BF16), win_ref[...])
    u_b = proj[:, :MIX_WIDTH]
    u_c = proj[:, MIX_WIDTH:2 * MIX_WIDTH]
    u_h = proj[:, 2 * MIX_WIDTH:3 * MIX_WIDTH]
    q_mem = proj[:, 3 * MIX_WIDTH:]

    z = u_c * u_h
    tail = ztail_ref[...]
    ztail_ref[...] = z[tile - 8:, :]
    row = lax.broadcasted_iota(jnp.int32, z.shape, 0)
    z1 = jnp.where(row == 0, tail[7:8, :], pltpu.roll(z, 1, 0))
    z2 = jnp.where(row == 0, tail[6:7, :], jnp.where(row == 1, tail[7:8, :], pltpu.roll(z, 2, 0)))
    w = conv_ref[...]
    main = u_b * (w[0:1, :] * z2 + w[1:2, :] * z1 + w[2:3, :] * z)

    mem_out = _mem_attention(q_mem, mk_ref[0], mv_ref[0])
    mix = (_dot(main.astype(BF16), wo_ref[:MIX_WIDTH, :])
           + _dot(mem_out.astype(BF16), wo_ref[MIX_WIDTH:, :]))
    o_ref[0] = _layer_norm(alpha * x + mix, g_ref[...], b_ref[...])


def _mixer_a(x, w_in, conv_w, mem_kv, layer, w_o, g, b, alpha):
    bsz, seq, d = x.shape
    tile = min(SEQ_TILE, seq)
    mem_len = mem_kv.shape[1]
    return pl.pallas_call(
        functools.partial(_mixer_a_kernel, alpha=alpha),
        grid=(bsz, seq // tile),
        in_specs=[
            pl.BlockSpec((1, tile, d), lambda bi, ti: (bi, ti, 0)),
            pl.BlockSpec(w_in.shape, lambda bi, ti: (0, 0)),
            pl.BlockSpec(conv_w.shape, lambda bi, ti: (0, 0)),
            pl.BlockSpec((1, mem_len, MEM_WIDTH), lambda bi, ti: (bi, 0, 2 * layer)),
            pl.BlockSpec((1, mem_len, MEM_WIDTH), lambda bi, ti: (bi, 0, 2 * layer + 1)),
            pl.BlockSpec(w_o.shape, lambda bi, ti: (0, 0)),
            pl.BlockSpec((1, d), lambda bi, ti: (0, 0)),
            pl.BlockSpec((1, d), lambda bi, ti: (0, 0)),
        ],
        out_specs=pl.BlockSpec((1, tile, d), lambda bi, ti: (bi, ti, 0)),
        out_shape=jax.ShapeDtypeStruct(x.shape, F32),
        scratch_shapes=[pltpu.VMEM((8, MIX_WIDTH), F32)],
        compiler_params=_params(("arbitrary", "arbitrary")),
        name="mixer_a",
    )(x, w_in, conv_w, mem_kv, mem_kv, w_o, g, b)


def _shared_kv_kernel(x_ref, pos_ref, freq_ref, wk_ref, wkr_ref, wv_ref, k_ref, v_ref, km_ref):
    xb = x_ref[0].astype(BF16)
    cos, sin = _rope_tables(pos_ref[0], freq_ref[...], MIX_WIDTH // LANES)
    k = _dot(xb, wk_ref[...]) * cos + _dot(xb, wkr_ref[...]) * sin
    k_ref[0] = k.astype(BF16)
    v_ref[0] = _dot(xb, wv_ref[...]).astype(BF16)
    km_ref[0, 0] = jnp.mean(k, axis=0, keepdims=True)


def _shared_kv(x, pos, freq, w_k, w_kr, w_v):
    bsz, seq, d = x.shape
    n_blocks = seq // MOBA_BLOCK
    full = lambda bi, ti: (0, 0)
    return pl.pallas_call(
        _shared_kv_kernel,
        grid=(bsz, n_blocks),
        in_specs=[
            pl.BlockSpec((1, MOBA_BLOCK, d), lambda bi, ti: (bi, ti, 0)),
            pl.BlockSpec((1, MOBA_BLOCK, 1), lambda bi, ti: (bi, ti, 0)),
            pl.BlockSpec(freq.shape, full),
            pl.BlockSpec(w_k.shape, full),
            pl.BlockSpec(w_kr.shape, full),
            pl.BlockSpec(w_v.shape, full),
        ],
        out_specs=[
            pl.BlockSpec((1, MOBA_BLOCK, MIX_WIDTH), lambda bi, ti: (bi, ti, 0)),
            pl.BlockSpec((1, MOBA_BLOCK, MIX_WIDTH), lambda bi, ti: (bi, ti, 0)),
            pl.BlockSpec((1, 1, 1, MIX_WIDTH), lambda bi, ti: (bi, ti, 0, 0)),
        ],
        out_shape=[
            jax.ShapeDtypeStruct((bsz, seq, MIX_WIDTH), BF16),
            jax.ShapeDtypeStruct((bsz, seq, MIX_WIDTH), BF16),
            jax.ShapeDtypeStruct((bsz, n_blocks, 1, MIX_WIDTH), F32),
        ],
        compiler_params=_params(("arbitrary", "arbitrary")),
        name="shared_kv",
    )(x, pos, freq, w_k, w_kr, w_v)


def _query_b_kernel(x_ref, pos_ref, freq_ref, wq_ref, wqr_ref, mk_ref, mv_ref, q_ref, mo_ref):
    xb = x_ref[0].astype(BF16)
    proj = _dot(xb, wq_ref[...])
    cos, sin = _rope_tables(pos_ref[0], freq_ref[...], MIX_WIDTH // LANES)
    q = proj[:, :MIX_WIDTH] * cos + _dot(xb, wqr_ref[...]) * sin
    q_ref[0] = (q * ATTN_SCALE).astype(BF16)
    mo_ref[0] = _mem_attention(proj[:, MIX_WIDTH:], mk_ref[0], mv_ref[0]).astype(BF16)


def _query_b(x, pos, freq, w_q, w_qr, mem_kv, layer):
    bsz, seq, d = x.shape
    tile = min(SEQ_TILE, seq)
    mem_len = mem_kv.shape[1]
    full = lambda bi, ti: (0, 0)
    return pl.pallas_call(
        _query_b_kernel,
        grid=(bsz, seq // tile),
        in_specs=[
            pl.BlockSpec((1, tile, d), lambda bi, ti: (bi, ti, 0)),
            pl.BlockSpec((1, tile, 1), lambda bi, ti: (bi, ti, 0)),
            pl.BlockSpec(freq.shape, full),
            pl.BlockSpec(w_q.shape, full),
            pl.BlockSpec(w_qr.shape, full),
            pl.BlockSpec((1, mem_len, MEM_WIDTH), lambda bi, ti: (bi, 0, 2 * layer)),
            pl.BlockSpec((1, mem_len, MEM_WIDTH), lambda bi, ti: (bi, 0, 2 * layer + 1)),
        ],
        out_specs=[
            pl.BlockSpec((1, tile, MIX_WIDTH), lambda bi, ti: (bi, ti, 0)),
            pl.BlockSpec((1, tile, MEM_WIDTH), lambda bi, ti: (bi, ti, 0)),
        ],
        out_shape=[
            jax.ShapeDtypeStruct((bsz, seq, MIX_WIDTH), BF16),
            jax.ShapeDtypeStruct((bsz, seq, MEM_WIDTH), BF16),
        ],
        compiler_params=_params(("arbitrary", "arbitrary")),
        name="query_b",
    )(x, pos, freq, w_q, w_qr, mem_kv, mem_kv)


def _moba_kernel(q_ref, k_ref, v_ref, km_ref, o_ref, m_ref, l_ref, acc_ref):
    blk = MOBA_BLOCK
    own = pl.program_id(2)
    q = q_ref[0]
    k_means = km_ref[0].astype(BF16)
    block_id = lax.broadcasted_iota(jnp.int32, (blk, LANES), 1)
    is_past = block_id < own
    causal = (lax.broadcasted_iota(jnp.int32, (blk, blk), 1)
              <= lax.broadcasted_iota(jnp.int32, (blk, blk), 0))
    out = jnp.zeros((blk, LANES), F32)

    for head in range(HEADS_PER_LANE_GROUP):
        hm = _head_lane_mask(LANES, head)
        qh = jnp.where(hm, q, jnp.zeros_like(q))

        gate = jnp.where(is_past, _dot_nt(qh, k_means), -jnp.inf)
        kth = gate
        for _ in range(MOBA_TOPK - 1):
            kth = jnp.where(kth == jnp.max(kth, axis=-1, keepdims=True), -jnp.inf, kth)
        threshold = jnp.max(kth, axis=-1, keepdims=True)
        block_bias = jnp.where(is_past & (gate >= threshold), 0.0, MASK_VALUE).astype(BF16)

        start = pl.multiple_of(own * blk, blk)
        s = jnp.where(causal, _dot_nt(qh, k_ref[0, pl.ds(start, blk), :]), MASK_VALUE)
        m = jnp.max(s, axis=-1, keepdims=True)
        p = jnp.exp(s - m)
        m_ref[...] = m
        l_ref[...] = jnp.sum(p, axis=-1, keepdims=True)
        acc_ref[...] = _dot(p.astype(BF16), v_ref[0, pl.ds(start, blk), :])

        def past_block(j, carry):
            off = pl.multiple_of(j * blk, blk)
            pick = (lax.broadcasted_iota(jnp.int32, (LANES, blk), 0) == j).astype(BF16)
            s = _dot_nt(qh, k_ref[0, pl.ds(off, blk), :]) + _dot(block_bias, pick)
            m_old = m_ref[...]
            m_new = jnp.maximum(m_old, jnp.max(s, axis=-1, keepdims=True))
            p = jnp.exp(s - m_new)
            scale = jnp.exp(m_old - m_new)
            l_ref[...] = scale * l_ref[...] + jnp.sum(p, axis=-1, keepdims=True)
            acc_ref[...] = scale * acc_ref[...] + _dot(p.astype(BF16), v_ref[0, pl.ds(off, blk), :])
            m_ref[...] = m_new
            return carry

        lax.fori_loop(0, own, past_block, 0)
        out = jnp.where(hm, acc_ref[...] / l_ref[...], out)

    o_ref[0] = out.astype(o_ref.dtype)


def _moba(q, k, v, k_means):
    bsz, seq, _ = q.shape
    n_blocks = seq // MOBA_BLOCK
    groups = MIX_WIDTH // LANES
    return pl.pallas_call(
        _moba_kernel,
        grid=(bsz, groups, n_blocks),
        in_specs=[
            pl.BlockSpec((1, MOBA_BLOCK, LANES), lambda bi, gi, qi: (bi, qi, gi)),
            pl.BlockSpec((1, seq, LANES), lambda bi, gi, qi: (bi, 0, gi)),
            pl.BlockSpec((1, seq, LANES), lambda bi, gi, qi: (bi, 0, gi)),
            pl.BlockSpec((1, LANES, LANES), lambda bi, gi, qi: (bi, 0, gi)),
        ],
        out_specs=pl.BlockSpec((1, MOBA_BLOCK, LANES), lambda bi, gi, qi: (bi, qi, gi)),
        out_shape=jax.ShapeDtypeStruct(q.shape, BF16),
        scratch_shapes=[pltpu.VMEM((MOBA_BLOCK, 1), F32), pltpu.VMEM((MOBA_BLOCK, 1), F32),
                        pltpu.VMEM((MOBA_BLOCK, LANES), F32)],
        compiler_params=_params(("arbitrary", "arbitrary", "arbitrary")),
        name="moba",
    )(q, k, v, k_means)


def _mixer_out_kernel(x_ref, main_ref, mo_ref, wo_ref, g_ref, b_ref, o_ref, *, alpha):
    mix = _dot(main_ref[0], wo_ref[:MIX_WIDTH, :]) + _dot(mo_ref[0], wo_ref[MIX_WIDTH:, :])
    o_ref[0] = _layer_norm(alpha * x_ref[0] + mix, g_ref[...], b_ref[...])


def _mixer_out(x, main, mem_out, w_o, g, b, alpha):
    bsz, seq, d = x.shape
    tile = min(SEQ_TILE, seq)
    full = lambda bi, ti: (0, 0)
    return pl.pallas_call(
        functools.partial(_mixer_out_kernel, alpha=alpha),
        grid=(bsz, seq // tile),
        in_specs=[
            pl.BlockSpec((1, tile, d), lambda bi, ti: (bi, ti, 0)),
            pl.BlockSpec((1, tile, MIX_WIDTH), lambda bi, ti: (bi, ti, 0)),
            pl.BlockSpec((1, tile, MEM_WIDTH), lambda bi, ti: (bi, ti, 0)),
            pl.BlockSpec(w_o.shape, full),
            pl.BlockSpec((1, d), full),
            pl.BlockSpec((1, d), full),
        ],
        out_specs=pl.BlockSpec((1, tile, d), lambda bi, ti: (bi, ti, 0)),
        out_shape=jax.ShapeDtypeStruct(x.shape, F32),
        compiler_params=_params(("arbitrary", "arbitrary")),
        name="mixer_out",
    )(x, main, mem_out, w_o, g, b)


def _ffn_kernel(x_ref, wr_ref, wg_ref, wu_ref, wd_ref, g_ref, b_ref, o_ref,
                acc_ref, xb_ref, gate_ref, *, alpha, n_experts):
    e = pl.program_id(1)
    c = pl.program_id(2)
    first = (e == 0) & (c == 0)
    last = (e == pl.num_programs(1) - 1) & (c == pl.num_programs(2) - 1)

    @pl.when(first)
    def _():
        x = x_ref[...]
        acc_ref[...] = jnp.zeros_like(acc_ref)
        xb_ref[...] = x.astype(BF16)
        if n_experts > 1:
            lane = lax.broadcasted_iota(jnp.int32, (x.shape[0], LANES), 1)
            logits = jnp.dot(x, wr_ref[...], precision=lax.Precision.HIGHEST,
                             preferred_element_type=F32)
            logits = jnp.where(lane < n_experts, logits, -jnp.inf)
            top1 = jnp.max(logits, axis=-1, keepdims=True)
            idx1 = jnp.min(jnp.where(logits == top1, lane, LANES), axis=-1, keepdims=True)
            rest = jnp.where(lane == idx1, -jnp.inf, logits)
            top2 = jnp.max(rest, axis=-1, keepdims=True)
            idx2 = jnp.min(jnp.where(rest == top2, lane, LANES), axis=-1, keepdims=True)
            e2 = jnp.exp(top2 - top1)
            gate_ref[...] = jnp.where(lane == idx1, 1.0 / (1.0 + e2),
                                      jnp.where(lane == idx2, e2 / (1.0 + e2), 0.0))

    xb = xb_ref[...]
    gp = _dot(xb, wg_ref[0])
    up = _dot(xb, wu_ref[0])
    h = gp * jax.nn.sigmoid(gp) * up
    if n_experts > 1:
        lane = lax.broadcasted_iota(jnp.int32, gate_ref.shape, 1)
        h = h * jnp.sum(jnp.where(lane == e, gate_ref[...], 0.0), axis=-1, keepdims=True)
    acc_ref[...] += _dot(h.astype(BF16), wd_ref[0])

    @pl.when(last)
    def _():
        o_ref[...] = _layer_norm(alpha * x_ref[...] + acc_ref[...], g_ref[...], b_ref[...])


def _ffn(x, w_router, w_gu, w_down, g, b, alpha):
    n, d = x.shape
    n_experts, d_ff, _ = w_down.shape
    tile = min(FFN_TILE, n)
    chunk = FF_CHUNK if d_ff % FF_CHUNK == 0 else d_ff
    n_chunks = d_ff // chunk
    return pl.pallas_call(
        functools.partial(_ffn_kernel, alpha=alpha, n_experts=n_experts),
        grid=(n // tile, n_experts, n_chunks),
        in_specs=[
            pl.BlockSpec((tile, d), lambda i, e, c: (i, 0)),
            pl.BlockSpec(w_router.shape, lambda i, e, c: (0, 0)),
            pl.BlockSpec((1, d, chunk), lambda i, e, c: (e, 0, c)),
            pl.BlockSpec((1, d, chunk), lambda i, e, c: (e, 0, c + n_chunks)),
            pl.BlockSpec((1, chunk, d), lambda i, e, c: (e, c, 0)),
            pl.BlockSpec((1, d), lambda i, e, c: (0, 0)),
            pl.BlockSpec((1, d), lambda i, e, c: (0, 0)),
        ],
        out_specs=pl.BlockSpec((tile, d), lambda i, e, c: (i, 0)),
        out_shape=jax.ShapeDtypeStruct(x.shape, F32),
        scratch_shapes=[pltpu.VMEM((tile, d), F32), pltpu.VMEM((tile, d), BF16),
                        pltpu.VMEM((tile, LANES), F32)],
        compiler_params=_params(("arbitrary", "arbitrary", "arbitrary")),
        name="ffn_moe" if n_experts > 1 else "ffn_dense",
    )(x, w_router, w_gu, w_gu, w_down, g, b)


def _rotate_half_columns(w):
    half = ROPE_DIM // 2
    col = jnp.arange(w.shape[1])
    c = col % HEAD_DIM
    src = jnp.where(c < half, col + half, col - half)
    sign = jnp.where(c < half, -1.0, jnp.where(c < ROPE_DIM, 1.0, 0.0)).astype(w.dtype)
    return w[:, jnp.clip(src, 0, w.shape[1] - 1)] * sign


def _rope_lane_frequencies():
    inv_freq = ROPE_THETA ** (-jnp.arange(0, ROPE_DIM, 2, dtype=F32) / ROPE_DIM)
    c = jnp.arange(LANES) % HEAD_DIM
    return jnp.where(c < ROPE_DIM, inv_freq[c % (ROPE_DIM // 2)], 0.0).astype(F32)[None, :]


def kernel(x, mem, positions, w_in_a, conv_a, w_q_b, w_kv_shared, w_mem_kv, w_o, ln1_g, ln1_b,
           ln2_g, ln2_b, w_gu_dense, w_down_dense, w_router, w_gu_moe, w_down_moe):
    bsz, seq, d = x.shape
    depth = w_o.shape[0]
    n_a = w_in_a.shape[0]
    alpha = float((2 * depth) ** 0.25)
    assert seq % MOBA_BLOCK == 0 and seq // MOBA_BLOCK <= LANES

    pos = positions.astype(F32)[:, :, None]
    freq = _rope_lane_frequencies()
    mem_len = mem.shape[1]
    w_mem_all = jnp.transpose(w_mem_kv, (1, 0, 2)).reshape(d, depth * 2 * MEM_WIDTH).astype(BF16)
    mem_kv = _matmul(mem.reshape(bsz * mem_len, d), w_mem_all, BF16).reshape(bsz, mem_len, -1)

    w_router_pad = jnp.pad(w_router, ((0, 0), (0, 0), (0, LANES - w_router.shape[-1])))
    no_router = jnp.zeros((d, LANES), F32)
    shared = None
    for layer in range(depth):
        g1, b1 = ln1_g[layer][None, :], ln1_b[layer][None, :]
        g2, b2 = ln2_g[layer][None, :], ln2_b[layer][None, :]
        wo = w_o[layer].astype(BF16)
        if layer < n_a:
            x = _mixer_a(x, w_in_a[layer].astype(BF16), conv_a[layer], mem_kv, layer, wo, g1, b1, alpha)
        else:
            wq = w_q_b[layer - n_a]
            q, mem_out = _query_b(x, pos, freq, wq.astype(BF16),
                                  _rotate_half_columns(wq[:, :MIX_WIDTH]).astype(BF16), mem_kv, layer)
            main = _moba(q, *shared)
            x = _mixer_out(x, main, mem_out, wo, g1, b1, alpha)
        xf = x.reshape(bsz * seq, d)
        if layer % 2 == 0:
            xf = _ffn(xf, no_router, w_gu_dense[layer // 2][None].astype(BF16),
                      w_down_dense[layer // 2][None].astype(BF16), g2, b2, alpha)
        else:
            xf = _ffn(xf, w_router_pad[layer // 2], w_gu_moe[layer // 2].astype(BF16),
                      w_down_moe[layer // 2].astype(BF16), g2, b2, alpha)
        x = xf.reshape(bsz, seq, d)
        if layer == n_a - 1:
            wk = w_kv_shared[:, :MIX_WIDTH]
            k, v, km = _shared_kv(x, pos, freq, wk.astype(BF16), _rotate_half_columns(wk).astype(BF16),
                                  w_kv_shared[:, MIX_WIDTH:].astype(BF16))
            n_blocks = seq // MOBA_BLOCK
            km = jnp.pad(km.reshape(bsz, n_blocks, MIX_WIDTH), ((0, 0), (0, LANES - n_blocks), (0, 0)))
            shared = (k, v, km)
    return x
```

```python
import functools

import jax
import jax.numpy as jnp
from jax import lax
from jax.experimental import pallas as pl
from jax.experimental.pallas import tpu as pltpu

F32 = jnp.float32
BF16 = jnp.bfloat16

HEAD_DIM = 64
MIX_HEADS = 12
MIX_WIDTH = MIX_HEADS * HEAD_DIM
MEM_HEADS = 4
MEM_WIDTH = MEM_HEADS * HEAD_DIM
CONV_WIDTH = 3
ROPE_DIM = HEAD_DIM // 4
ROPE_THETA = 500000.0
MOBA_BLOCK = 256
MOBA_TOPK = 3
N_EXPERTS = 8
LN_EPS = 1e-5
ATTN_SCALE = HEAD_DIM ** -0.5

LANES = 128
HEADS_PER_LANE_GROUP = LANES // HEAD_DIM
MASK_VALUE = -1e30
VMEM_LIMIT_BYTES = 56 * 1024 * 1024

SEQ_TILE = 512
FFN_TILE = 1024
FF_CHUNK = 1408
MOBA_CHUNK_BLOCKS = 4


def _params(semantics):
    return pltpu.CompilerParams(dimension_semantics=semantics, vmem_limit_bytes=VMEM_LIMIT_BYTES)


def _layer_norm(r, g, b):
    mu = jnp.mean(r, axis=-1, keepdims=True)
    c = r - mu
    var = jnp.mean(c * c, axis=-1, keepdims=True)
    return c * lax.rsqrt(var + LN_EPS) * g + b


def _dot(a, b):
    return jnp.dot(a, b, preferred_element_type=F32)


def _dot_nt(a, b):
    return lax.dot_general(a, b, (((1,), (1,)), ((), ())), preferred_element_type=F32)


def _head_lane_mask(width, head):
    lane = lax.broadcasted_iota(jnp.int32, (1, width), 1)
    return (lane >= head * HEAD_DIM) & (lane < (head + 1) * HEAD_DIM)


def _mem_attention(q_mem, mem_k, mem_v):
    out = jnp.zeros(q_mem.shape, F32)
    for head in range(MEM_HEADS):
        hm = _head_lane_mask(MEM_WIDTH, head)
        qh = jnp.where(hm, q_mem * ATTN_SCALE, 0.0).astype(BF16)
        s = _dot_nt(qh, mem_k)
        m = jnp.max(s, axis=-1, keepdims=True)
        p = jnp.exp(s - m)
        l = jnp.sum(p, axis=-1, keepdims=True)
        o = _dot(p.astype(BF16), mem_v) / l
        out = jnp.where(hm, o, out)
    return out


def _rope_tables(pos, freq, reps):
    angle = pos * freq
    cos = jnp.cos(angle)
    sin = jnp.sin(angle)
    return (jnp.concatenate([cos] * reps, axis=1), jnp.concatenate([sin] * reps, axis=1))


def _matmul_kernel(x_ref, w_ref, o_ref):
    o_ref[...] = _dot(x_ref[...].astype(BF16), w_ref[...]).astype(o_ref.dtype)


def _matmul(x, w, out_dtype, tile_m=256):
    m, k = x.shape
    n = w.shape[1]
    return pl.pallas_call(
        _matmul_kernel,
        grid=(m // tile_m,),
        in_specs=[pl.BlockSpec((tile_m, k), lambda i: (i, 0)),
                  pl.BlockSpec((k, n), lambda i: (0, 0))],
        out_specs=pl.BlockSpec((tile_m, n), lambda i: (i, 0)),
        out_shape=jax.ShapeDtypeStruct((m, n), out_dtype),
        compiler_params=_params(("arbitrary",)),
        name="mem_kv_proj",
    )(x, w)


def _mixer_a_kernel(x_ref, win_ref, conv_ref, mk_ref, mv_ref, wo_ref, g_ref, b_ref, o_ref,
                    ztail_ref, *, alpha):
    tile = x_ref.shape[1]

    @pl.when(pl.program_id(1) == 0)
    def _():
        ztail_ref[...] = jnp.zeros_like(ztail_ref)

    x = x_ref[0]
    proj = _dot(x.astype(BF16), win_ref[...])
    u_b = proj[:, :MIX_WIDTH]
    u_c = proj[:, MIX_WIDTH:2 * MIX_WIDTH]
    u_h = proj[:, 2 * MIX_WIDTH:3 * MIX_WIDTH]
    q_mem = proj[:, 3 * MIX_WIDTH:]

    z = u_c * u_h
    tail = ztail_ref[...]
    ztail_ref[...] = z[tile - 8:, :]
    row = lax.broadcasted_iota(jnp.int32, z.shape, 0)
    z1 = jnp.where(row == 0, tail[7:8, :], pltpu.roll(z, 1, 0))
    z2 = jnp.where(row == 0, tail[6:7, :], jnp.where(row == 1, tail[7:8, :], pltpu.roll(z, 2, 0)))
    w = conv_ref[...]
    main = u_b * (w[0:1, :] * z2 + w[1:2, :] * z1 + w[2:3, :] * z)

    mem_out = _mem_attention(q_mem, mk_ref[0], mv_ref[0])
    mix = (_dot(main.astype(BF16), wo_ref[:MIX_WIDTH, :])
           + _dot(mem_out.astype(BF16), wo_ref[MIX_WIDTH:, :]))
    o_ref[0] = _layer_norm(alpha * x + mix, g_ref[...], b_ref[...])


def _mixer_a(x, w_in, conv_w, mem_kv, layer, w_o, g, b, alpha):
    bsz, seq, d = x.shape
    tile = min(SEQ_TILE, seq)
    mem_len = mem_kv.shape[1]
    return pl.pallas_call(
        functools.partial(_mixer_a_kernel, alpha=alpha),
        grid=(bsz, seq // tile),
        in_specs=[
            pl.BlockSpec((1, tile, d), lambda bi, ti: (bi, ti, 0)),
            pl.BlockSpec(w_in.shape, lambda bi, ti: (0, 0)),
            pl.BlockSpec(conv_w.shape, lambda bi, ti: (0, 0)),
            pl.BlockSpec((1, mem_len, MEM_WIDTH), lambda bi, ti: (bi, 0, 2 * layer)),
            pl.BlockSpec((1, mem_len, MEM_WIDTH), lambda bi, ti: (bi, 0, 2 * layer + 1)),
            pl.BlockSpec(w_o.shape, lambda bi, ti: (0, 0)),
            pl.BlockSpec((1, d), lambda bi, ti: (0, 0)),
            pl.BlockSpec((1, d), lambda bi, ti: (0, 0)),
        ],
        out_specs=pl.BlockSpec((1, tile, d), lambda bi, ti: (bi, ti, 0)),
        out_shape=jax.ShapeDtypeStruct(x.shape, F32),
        scratch_shapes=[pltpu.VMEM((8, MIX_WIDTH), F32)],
        compiler_params=_params(("arbitrary", "arbitrary")),
        name="mixer_a",
    )(x, w_in, conv_w, mem_kv, mem_kv, w_o, g, b)


def _shared_kv_kernel(x_ref, pos_ref, freq_ref, wk_ref, wkr_ref, wv_ref, k_ref, v_ref, km_ref):
    xb = x_ref[0].astype(BF16)
    cos, sin = _rope_tables(pos_ref[0], freq_ref[...], MIX_WIDTH // LANES)
    k = _dot(xb, wk_ref[...]) * cos + _dot(xb, wkr_ref[...]) * sin
    k_ref[0] = k.astype(BF16)
    v_ref[0] = _dot(xb, wv_ref[...]).astype(BF16)
    km_ref[0, 0] = jnp.mean(k, axis=0, keepdims=True)


def _shared_kv(x, pos, freq, w_k, w_kr, w_v):
    bsz, seq, d = x.shape
    n_blocks = seq // MOBA_BLOCK
    full = lambda bi, ti: (0, 0)
    return pl.pallas_call(
        _shared_kv_kernel,
        grid=(bsz, n_blocks),
        in_specs=[
            pl.BlockSpec((1, MOBA_BLOCK, d), lambda bi, ti: (bi, ti, 0)),
            pl.BlockSpec((1, MOBA_BLOCK, 1), lambda bi, ti: (bi, ti, 0)),
            pl.BlockSpec(freq.shape, full),
            pl.BlockSpec(w_k.shape, full),
            pl.BlockSpec(w_kr.shape, full),
            pl.BlockSpec(w_v.shape, full),
        ],
        out_specs=[
            pl.BlockSpec((1, MOBA_BLOCK, MIX_WIDTH), lambda bi, ti: (bi, ti, 0)),
            pl.BlockSpec((1, MOBA_BLOCK, MIX_WIDTH), lambda bi, ti: (bi, ti, 0)),
            pl.BlockSpec((1, 1, 1, MIX_WIDTH), lambda bi, ti: (bi, ti, 0, 0)),
        ],
        out_shape=[
            jax.ShapeDtypeStruct((bsz, seq, MIX_WIDTH), BF16),
            jax.ShapeDtypeStruct((bsz, seq, MIX_WIDTH), BF16),
            jax.ShapeDtypeStruct((bsz, n_blocks, 1, MIX_WIDTH), F32),
        ],
        compiler_params=_params(("arbitrary", "arbitrary")),
        name="shared_kv",
    )(x, pos, freq, w_k, w_kr, w_v)


def _query_b_kernel(x_ref, pos_ref, freq_ref, wq_ref, wqr_ref, mk_ref, mv_ref, q_ref, mo_ref):
    xb = x_ref[0].astype(BF16)
    proj = _dot(xb, wq_ref[...])
    cos, sin = _rope_tables(pos_ref[0], freq_ref[...], MIX_WIDTH // LANES)
    q = proj[:, :MIX_WIDTH] * cos + _dot(xb, wqr_ref[...]) * sin
    q_ref[0] = (q * ATTN_SCALE).astype(BF16)
    mo_ref[0] = _mem_attention(proj[:, MIX_WIDTH:], mk_ref[0], mv_ref[0]).astype(BF16)


def _query_b(x, pos, freq, w_q, w_qr, mem_kv, layer):
    bsz, seq, d = x.shape
    tile = min(SEQ_TILE, seq)
    mem_len = mem_kv.shape[1]
    full = lambda bi, ti: (0, 0)
    return pl.pallas_call(
        _query_b_kernel,
        grid=(bsz, seq // tile),
        in_specs=[
            pl.BlockSpec((1, tile, d), lambda bi, ti: (bi, ti, 0)),
            pl.BlockSpec((1, tile, 1), lambda bi, ti: (bi, ti, 0)),
            pl.BlockSpec(freq.shape, full),
            pl.BlockSpec(w_q.shape, full),
            pl.BlockSpec(w_qr.shape, full),
            pl.BlockSpec((1, mem_len, MEM_WIDTH), lambda bi, ti: (bi, 0, 2 * layer)),
            pl.BlockSpec((1, mem_len, MEM_WIDTH), lambda bi, ti: (bi, 0, 2 * layer + 1)),
        ],
        out_specs=[
            pl.BlockSpec((1, tile, MIX_WIDTH), lambda bi, ti: (bi, ti, 0)),
            pl.BlockSpec((1, tile, MEM_WIDTH), lambda bi, ti: (bi, ti, 0)),
        ],
        out_shape=[
            jax.ShapeDtypeStruct((bsz, seq, MIX_WIDTH), BF16),
            jax.ShapeDtypeStruct((bsz, seq, MEM_WIDTH), BF16),
        ],
        compiler_params=_params(("arbitrary", "arbitrary")),
        name="query_b",
    )(x, pos, freq, w_q, w_qr, mem_kv, mem_kv)


def _moba_kernel(q_ref, k_ref, v_ref, km_ref, o_ref, kaug_ref, vaug_ref, m_ref, acc_ref, p_ref,
                 scale_ref, *, chunk_blocks):
    blk = MOBA_BLOCK
    own = pl.program_id(2)
    rows = HEADS_PER_LANE_GROUP * blk
    span = chunk_blocks * blk

    @pl.when(own == 0)
    def _():
        kaug_ref[:, :LANES] = k_ref[0]
        vaug_ref[:, :LANES] = v_ref[0]
        lane = lax.broadcasted_iota(jnp.int32, (blk, LANES), 1)
        ones_lane = (lane == 0).astype(BF16)

        def fill(j, carry):
            off = pl.multiple_of(j * blk, blk)
            kaug_ref[pl.ds(off, blk), LANES:] = (lane == j).astype(BF16)
            vaug_ref[pl.ds(off, blk), LANES:] = ones_lane
            return carry

        lax.fori_loop(0, k_ref.shape[1] // blk, fill, 0)

    q = q_ref[0]
    q_heads = jnp.concatenate(
        [jnp.where(_head_lane_mask(LANES, h), q, jnp.zeros_like(q))
         for h in range(HEADS_PER_LANE_GROUP)], axis=0)

    block_id = lax.broadcasted_iota(jnp.int32, (rows, LANES), 1)
    is_past = block_id < own
    gate = jnp.where(is_past, _dot_nt(q_heads, km_ref[0].astype(BF16)), -jnp.inf)
    kth = gate
    for _ in range(MOBA_TOPK - 1):
        kth = jnp.where(kth == jnp.max(kth, axis=-1, keepdims=True), -jnp.inf, kth)
    threshold = jnp.max(kth, axis=-1, keepdims=True)
    allowed = (is_past & (gate >= threshold)) | (block_id == own)
    block_bias = jnp.where(allowed, 0.0, MASK_VALUE).astype(BF16)
    q_aug = jnp.concatenate([q_heads, block_bias], axis=1)

    head_rows = [slice(h * blk, (h + 1) * blk) for h in range(HEADS_PER_LANE_GROUP)]
    m_ref[...] = jnp.full(m_ref.shape, MASK_VALUE, F32)
    acc_ref[...] = jnp.zeros_like(acc_ref)

    def score_phase(c, causal):
        k_chunk = kaug_ref[pl.ds(pl.multiple_of(c * span, span), span), :]
        for r in head_rows:
            s = _dot_nt(q_aug[r], k_chunk)
            if causal:
                k_pos = c * span + lax.broadcasted_iota(jnp.int32, (blk, span), 1)
                q_pos = own * blk + lax.broadcasted_iota(jnp.int32, (blk, span), 0)
                s = jnp.where(k_pos <= q_pos, s, MASK_VALUE)
            m_old = m_ref[r, :]
            m_new = jnp.maximum(m_old, jnp.max(s, axis=-1, keepdims=True))
            p_ref[r, :] = jnp.exp(s - m_new).astype(BF16)
            scale_ref[r, :] = jnp.exp(m_old - m_new)
            m_ref[r, :] = m_new

    def value_phase(c):
        v_chunk = vaug_ref[pl.ds(pl.multiple_of(c * span, span), span), :]
        for r in head_rows:
            acc_ref[r, :] = scale_ref[r, :] * acc_ref[r, :] + _dot(p_ref[r, :], v_chunk)

    own_chunk = own // chunk_blocks
    score_phase(own_chunk, causal=True)

    def pipelined(c, carry):
        value_phase(jnp.where(c == 0, own_chunk, c - 1))
        score_phase(c, causal=False)
        return carry

    lax.fori_loop(0, own_chunk, pipelined, 0)
    value_phase(jnp.where(own_chunk == 0, 0, own_chunk - 1))

    acc = acc_ref[...]
    o = acc[:, :LANES] / acc[:, LANES:LANES + 1]
    out = o[:blk]
    for h in range(1, HEADS_PER_LANE_GROUP):
        out = jnp.where(_head_lane_mask(LANES, h), o[h * blk:(h + 1) * blk], out)
    o_ref[0] = out.astype(o_ref.dtype)


def _moba(q, k, v, k_means):
    bsz, seq, _ = q.shape
    n_blocks = seq // MOBA_BLOCK
    groups = MIX_WIDTH // LANES
    chunk_blocks = MOBA_CHUNK_BLOCKS if n_blocks % MOBA_CHUNK_BLOCKS == 0 else 1
    rows = HEADS_PER_LANE_GROUP * MOBA_BLOCK
    return pl.pallas_call(
        functools.partial(_moba_kernel, chunk_blocks=chunk_blocks),
        grid=(bsz, groups, n_blocks),
        in_specs=[
            pl.BlockSpec((1, MOBA_BLOCK, LANES), lambda bi, gi, qi: (bi, qi, gi)),
            pl.BlockSpec((1, seq, LANES), lambda bi, gi, qi: (bi, 0, gi)),
            pl.BlockSpec((1, seq, LANES), lambda bi, gi, qi: (bi, 0, gi)),
            pl.BlockSpec((1, LANES, LANES), lambda bi, gi, qi: (bi, 0, gi)),
        ],
        out_specs=pl.BlockSpec((1, MOBA_BLOCK, LANES), lambda bi, gi, qi: (bi, qi, gi)),
        out_shape=jax.ShapeDtypeStruct(q.shape, BF16),
        scratch_shapes=[pltpu.VMEM((seq, 2 * LANES), BF16), pltpu.VMEM((seq, 2 * LANES), BF16),
                        pltpu.VMEM((rows, 1), F32), pltpu.VMEM((rows, 2 * LANES), F32),
                        pltpu.VMEM((rows, chunk_blocks * MOBA_BLOCK), BF16),
                        pltpu.VMEM((rows, 1), F32)],
        compiler_params=_params(("arbitrary", "arbitrary", "arbitrary")),
        name="moba",
    )(q, k, v, k_means)


def _mixer_out_kernel(x_ref, main_ref, mo_ref, wo_ref, g_ref, b_ref, o_ref, *, alpha):
    mix = _dot(main_ref[0], wo_ref[:MIX_WIDTH, :]) + _dot(mo_ref[0], wo_ref[MIX_WIDTH:, :])
    o_ref[0] = _layer_norm(alpha * x_ref[0] + mix, g_ref[...], b_ref[...])


def _mixer_out(x, main, mem_out, w_o, g, b, alpha):
    bsz, seq, d = x.shape
    tile = min(SEQ_TILE, seq)
    full = lambda bi, ti: (0, 0)
    return pl.pallas_call(
        functools.partial(_mixer_out_kernel, alpha=alpha),
        grid=(bsz, seq // tile),
        in_specs=[
            pl.BlockSpec((1, tile, d), lambda bi, ti: (bi, ti, 0)),
            pl.BlockSpec((1, tile, MIX_WIDTH), lambda bi, ti: (bi, ti, 0)),
            pl.BlockSpec((1, tile, MEM_WIDTH), lambda bi, ti: (bi, ti, 0)),
            pl.BlockSpec(w_o.shape, full),
            pl.BlockSpec((1, d), full),
            pl.BlockSpec((1, d), full),
        ],
        out_specs=pl.BlockSpec((1, tile, d), lambda bi, ti: (bi, ti, 0)),
        out_shape=jax.ShapeDtypeStruct(x.shape, F32),
        compiler_params=_params(("arbitrary", "arbitrary")),
        name="mixer_out",
    )(x, main, mem_out, w_o, g, b)


def _ffn_kernel(x_ref, wr_ref, wg_ref, wu_ref, wd_ref, g_ref, b_ref, o_ref,
                acc_ref, xb_ref, gate_ref, *, alpha, n_experts):
    e = pl.program_id(1)
    c = pl.program_id(2)
    first = (e == 0) & (c == 0)
    last = (e == pl.num_programs(1) - 1) & (c == pl.num_programs(2) - 1)

    @pl.when(first)
    def _():
        x = x_ref[...]
        acc_ref[...] = jnp.zeros_like(acc_ref)
        xb_ref[...] = x.astype(BF16)
        if n_experts > 1:
            lane = lax.broadcasted_iota(jnp.int32, (x.shape[0], LANES), 1)
            logits = jnp.dot(x, wr_ref[...], precision=lax.Precision.HIGHEST,
                             preferred_element_type=F32)
            logits = jnp.where(lane < n_experts, logits, -jnp.inf)
            top1 = jnp.max(logits, axis=-1, keepdims=True)
            idx1 = jnp.min(jnp.where(logits == top1, lane, LANES), axis=-1, keepdims=True)
            rest = jnp.where(lane == idx1, -jnp.inf, logits)
            top2 = jnp.max(rest, axis=-1, keepdims=True)
            idx2 = jnp.min(jnp.where(rest == top2, lane, LANES), axis=-1, keepdims=True)
            e2 = jnp.exp(top2 - top1)
            gate_ref[...] = jnp.where(lane == idx1, 1.0 / (1.0 + e2),
                                      jnp.where(lane == idx2, e2 / (1.0 + e2), 0.0))

    xb = xb_ref[...]
    gp = _dot(xb, wg_ref[0])
    up = _dot(xb, wu_ref[0])
    h = gp * jax.nn.sigmoid(gp) * up
    if n_experts > 1:
        lane = lax.broadcasted_iota(jnp.int32, gate_ref.shape, 1)
        h = h * jnp.sum(jnp.where(lane == e, gate_ref[...], 0.0), axis=-1, keepdims=True)
    acc_ref[...] += _dot(h.astype(BF16), wd_ref[0])

    @pl.when(last)
    def _():
        o_ref[...] = _layer_norm(alpha * x_ref[...] + acc_ref[...], g_ref[...], b_ref[...])


def _ffn(x, w_router, w_gu, w_down, g, b, alpha):
    n, d = x.shape
    n_experts, d_ff, _ = w_down.shape
    tile = min(FFN_TILE, n)
    chunk = FF_CHUNK if d_ff % FF_CHUNK == 0 else d_ff
    n_chunks = d_ff // chunk
    return pl.pallas_call(
        functools.partial(_ffn_kernel, alpha=alpha, n_experts=n_experts),
        grid=(n // tile, n_experts, n_chunks),
        in_specs=[
            pl.BlockSpec((tile, d), lambda i, e, c: (i, 0)),
            pl.BlockSpec(w_router.shape, lambda i, e, c: (0, 0)),
            pl.BlockSpec((1, d, chunk), lambda i, e, c: (e, 0, c)),
            pl.BlockSpec((1, d, chunk), lambda i, e, c: (e, 0, c + n_chunks)),
            pl.BlockSpec((1, chunk, d), lambda i, e, c: (e, c, 0)),
            pl.BlockSpec((1, d), lambda i, e, c: (0, 0)),
            pl.BlockSpec((1, d), lambda i, e, c: (0, 0)),
        ],
        out_specs=pl.BlockSpec((tile, d), lambda i, e, c: (i, 0)),
        out_shape=jax.ShapeDtypeStruct(x.shape, F32),
        scratch_shapes=[pltpu.VMEM((tile, d), F32), pltpu.VMEM((tile, d), BF16),
                        pltpu.VMEM((tile, LANES), F32)],
        compiler_params=_params(("arbitrary", "arbitrary", "arbitrary")),
        name="ffn_moe" if n_experts > 1 else "ffn_dense",
    )(x, w_router, w_gu, w_gu, w_down, g, b)


def _rotate_half_columns(w):
    half = ROPE_DIM // 2
    col = jnp.arange(w.shape[1])
    c = col % HEAD_DIM
    src = jnp.where(c < half, col + half, col - half)
    sign = jnp.where(c < half, -1.0, jnp.where(c < ROPE_DIM, 1.0, 0.0)).astype(w.dtype)
    return w[:, jnp.clip(src, 0, w.shape[1] - 1)] * sign


def _rope_lane_frequencies():
    inv_freq = ROPE_THETA ** (-jnp.arange(0, ROPE_DIM, 2, dtype=F32) / ROPE_DIM)
    c = jnp.arange(LANES) % HEAD_DIM
    return jnp.where(c < ROPE_DIM, inv_freq[c % (ROPE_DIM // 2)], 0.0).astype(F32)[None, :]


def kernel(x, mem, positions, w_in_a, conv_a, w_q_b, w_kv_shared, w_mem_kv, w_o, ln1_g, ln1_b,
           ln2_g, ln2_b, w_gu_dense, w_down_dense, w_router, w_gu_moe, w_down_moe):
    bsz, seq, d = x.shape
    depth = w_o.shape[0]
    n_a = w_in_a.shape[0]
    alpha = float((2 * depth) ** 0.25)
    assert seq % MOBA_BLOCK == 0 and seq // MOBA_BLOCK <= LANES

    pos = positions.astype(F32)[:, :, None]
    freq = _rope_lane_frequencies()
    mem_len = mem.shape[1]
    w_mem_all = jnp.transpose(w_mem_kv, (1, 0, 2)).reshape(d, depth * 2 * MEM_WIDTH).astype(BF16)
    mem_kv = _matmul(mem.reshape(bsz * mem_len, d), w_mem_all, BF16).reshape(bsz, mem_len, -1)

    w_router_pad = jnp.pad(w_router, ((0, 0), (0, 0), (0, LANES - w_router.shape[-1])))
    no_router = jnp.zeros((d, LANES), F32)
    shared = None
    for layer in range(depth):
        g1, b1 = ln1_g[layer][None, :], ln1_b[layer][None, :]
        g2, b2 = ln2_g[layer][None, :], ln2_b[layer][None, :]
        wo = w_o[layer].astype(BF16)
        if layer < n_a:
            x = _mixer_a(x, w_in_a[layer].astype(BF16), conv_a[layer], mem_kv, layer, wo, g1, b1, alpha)
        else:
            wq = w_q_b[layer - n_a]
            q, mem_out = _query_b(x, pos, freq, wq.astype(BF16),
                                  _rotate_half_columns(wq[:, :MIX_WIDTH]).astype(BF16), mem_kv, layer)
            main = _moba(q, *shared)
            x = _mixer_out(x, main, mem_out, wo, g1, b1, alpha)
        xf = x.reshape(bsz * seq, d)
        if layer % 2 == 0:
            xf = _ffn(xf, no_router, w_gu_dense[layer // 2][None].astype(BF16),
                      w_down_dense[layer // 2][None].astype(BF16), g2, b2, alpha)
        else:
            xf = _ffn(xf, w_router_pad[layer // 2], w_gu_moe[layer // 2].astype(BF16),
                      w_down_moe[layer // 2].astype(BF16), g2, b2, alpha)
        x = xf.reshape(bsz, seq, d)
        if layer == n_a - 1:
            wk = w_kv_shared[:, :MIX_WIDTH]
            k, v, km = _shared_kv(x, pos, freq, wk.astype(BF16), _rotate_half_columns(wk).astype(BF16),
                                  w_kv_shared[:, MIX_WIDTH:].astype(BF16))
            n_blocks = seq // MOBA_BLOCK
            km = jnp.pad(km.reshape(bsz, n_blocks, MIX_WIDTH), ((0, 0), (0, LANES - n_blocks), (0, 0)))
            shared = (k, v, km)
    return x
```

```python
import functools

import jax
import jax.numpy as jnp
from jax import lax
from jax.experimental import pallas as pl
from jax.experimental.pallas import tpu as pltpu

F32 = jnp.float32
BF16 = jnp.bfloat16

HEAD_DIM = 64
MIX_HEADS = 12
MIX_WIDTH = MIX_HEADS * HEAD_DIM
MEM_HEADS = 4
MEM_WIDTH = MEM_HEADS * HEAD_DIM
CONV_WIDTH = 3
ROPE_DIM = HEAD_DIM // 4
ROPE_THETA = 500000.0
MOBA_BLOCK = 256
MOBA_TOPK = 3
TOP_K = 2
LN_EPS = 1e-5
ATTN_SCALE = HEAD_DIM ** -0.5

LANES = 128
HEADS_PER_LANE_GROUP = LANES // HEAD_DIM
MASK_VALUE = -1e30
VMEM_LIMIT_BYTES = 56 * 1024 * 1024

SEQ_TILE = 512
FFN_TILE = 1024
FF_CHUNK = 1408
MOBA_CHUNK_BLOCKS = 4
ROUTE_TILE = 1024
EXPERT_TILE = 512
COMBINE_TILE = 512


def _params(semantics):
    return pltpu.CompilerParams(dimension_semantics=semantics, vmem_limit_bytes=VMEM_LIMIT_BYTES)


def _layer_norm(r, g, b):
    mu = jnp.mean(r, axis=-1, keepdims=True)
    c = r - mu
    var = jnp.mean(c * c, axis=-1, keepdims=True)
    return c * lax.rsqrt(var + LN_EPS) * g + b


def _dot(a, b):
    return jnp.dot(a, b, preferred_element_type=F32)


def _dot_nt(a, b):
    return lax.dot_general(a, b, (((1,), (1,)), ((), ())), preferred_element_type=F32)


def _head_lane_mask(width, head):
    lane = lax.broadcasted_iota(jnp.int32, (1, width), 1)
    return (lane >= head * HEAD_DIM) & (lane < (head + 1) * HEAD_DIM)


def _mem_attention(q_mem, mem_k, mem_v):
    out = jnp.zeros(q_mem.shape, F32)
    for head in range(MEM_HEADS):
        hm = _head_lane_mask(MEM_WIDTH, head)
        qh = jnp.where(hm, q_mem * ATTN_SCALE, 0.0).astype(BF16)
        s = _dot_nt(qh, mem_k)
        m = jnp.max(s, axis=-1, keepdims=True)
        p = jnp.exp(s - m)
        l = jnp.sum(p, axis=-1, keepdims=True)
        o = _dot(p.astype(BF16), mem_v) / l
        out = jnp.where(hm, o, out)
    return out


def _rope_tables(pos, freq, reps):
    angle = pos * freq
    cos = jnp.cos(angle)
    sin = jnp.sin(angle)
    return (jnp.concatenate([cos] * reps, axis=1), jnp.concatenate([sin] * reps, axis=1))


def _matmul_kernel(x_ref, w_ref, o_ref):
    o_ref[...] = _dot(x_ref[...].astype(BF16), w_ref[...]).astype(o_ref.dtype)


def _matmul(x, w, out_dtype, tile_m=256):
    m, k = x.shape
    n = w.shape[1]
    return pl.pallas_call(
        _matmul_kernel,
        grid=(m // tile_m,),
        in_specs=[pl.BlockSpec((tile_m, k), lambda i: (i, 0)),
                  pl.BlockSpec((k, n), lambda i: (0, 0))],
        out_specs=pl.BlockSpec((tile_m, n), lambda i: (i, 0)),
        out_shape=jax.ShapeDtypeStruct((m, n), out_dtype),
        compiler_params=_params(("arbitrary",)),
        name="mem_kv_proj",
    )(x, w)


def _mixer_a_kernel(x_ref, win_ref, conv_ref, mk_ref, mv_ref, wo_ref, g_ref, b_ref, o_ref,
                    ztail_ref, *, alpha):
    tile = x_ref.shape[1]

    @pl.when(pl.program_id(1) == 0)
    def _():
        ztail_ref[...] = jnp.zeros_like(ztail_ref)

    x = x_ref[0]
    proj = _dot(x.astype(BF16), win_ref[...])
    u_b = proj[:, :MIX_WIDTH]
    u_c = proj[:, MIX_WIDTH:2 * MIX_WIDTH]
    u_h = proj[:, 2 * MIX_WIDTH:3 * MIX_WIDTH]
    q_mem = proj[:, 3 * MIX_WIDTH:]

    z = u_c * u_h
    tail = ztail_ref[...]
    ztail_ref[...] = z[tile - 8:, :]
    row = lax.broadcasted_iota(jnp.int32, z.shape, 0)
    z1 = jnp.where(row == 0, tail[7:8, :], pltpu.roll(z, 1, 0))
    z2 = jnp.where(row == 0, tail[6:7, :], jnp.where(row == 1, tail[7:8, :], pltpu.roll(z, 2, 0)))
    w = conv_ref[...]
    main = u_b * (w[0:1, :] * z2 + w[1:2, :] * z1 + w[2:3, :] * z)

    mem_out = _mem_attention(q_mem, mk_ref[0], mv_ref[0])
    mix = (_dot(main.astype(BF16), wo_ref[:MIX_WIDTH, :])
           + _dot(mem_out.astype(BF16), wo_ref[MIX_WIDTH:, :]))
    o_ref[0] = _layer_norm(alpha * x + mix, g_ref[...], b_ref[...])


def _mixer_a(x, w_in, conv_w, mem_kv, layer, w_o, g, b, alpha):
    bsz, seq, d = x.shape
    tile = min(SEQ_TILE, seq)
    mem_len = mem_kv.shape[1]
    return pl.pallas_call(
        functools.partial(_mixer_a_kernel, alpha=alpha),
        grid=(bsz, seq // tile),
        in_specs=[
            pl.BlockSpec((1, tile, d), lambda bi, ti: (bi, ti, 0)),
            pl.BlockSpec(w_in.shape, lambda bi, ti: (0, 0)),
            pl.BlockSpec(conv_w.shape, lambda bi, ti: (0, 0)),
            pl.BlockSpec((1, mem_len, MEM_WIDTH), lambda bi, ti: (bi, 0, 2 * layer)),
            pl.BlockSpec((1, mem_len, MEM_WIDTH), lambda bi, ti: (bi, 0, 2 * layer + 1)),
            pl.BlockSpec(w_o.shape, lambda bi, ti: (0, 0)),
            pl.BlockSpec((1, d), lambda bi, ti: (0, 0)),
            pl.BlockSpec((1, d), lambda bi, ti: (0, 0)),
        ],
        out_specs=pl.BlockSpec((1, tile, d), lambda bi, ti: (bi, ti, 0)),
        out_shape=jax.ShapeDtypeStruct(x.shape, F32),
        scratch_shapes=[pltpu.VMEM((8, MIX_WIDTH), F32)],
        compiler_params=_params(("arbitrary", "arbitrary")),
        name="mixer_a",
    )(x, w_in, conv_w, mem_kv, mem_kv, w_o, g, b)


def _shared_kv_kernel(x_ref, pos_ref, freq_ref, wk_ref, wkr_ref, wv_ref, k_ref, v_ref, km_ref):
    xb = x_ref[0].astype(BF16)
    cos, sin = _rope_tables(pos_ref[0], freq_ref[...], MIX_WIDTH // LANES)
    k = _dot(xb, wk_ref[...]) * cos + _dot(xb, wkr_ref[...]) * sin
    k_ref[0] = k.astype(BF16)
    v_ref[0] = _dot(xb, wv_ref[...]).astype(BF16)
    km_ref[0, 0] = jnp.mean(k, axis=0, keepdims=True)


def _shared_kv(x, pos, freq, w_k, w_kr, w_v):
    bsz, seq, d = x.shape
    n_blocks = seq // MOBA_BLOCK
    full = lambda bi, ti: (0, 0)
    return pl.pallas_call(
        _shared_kv_kernel,
        grid=(bsz, n_blocks),
        in_specs=[
            pl.BlockSpec((1, MOBA_BLOCK, d), lambda bi, ti: (bi, ti, 0)),
            pl.BlockSpec((1, MOBA_BLOCK, 1), lambda bi, ti: (bi, ti, 0)),
            pl.BlockSpec(freq.shape, full),
            pl.BlockSpec(w_k.shape, full),
            pl.BlockSpec(w_kr.shape, full),
            pl.BlockSpec(w_v.shape, full),
        ],
        out_specs=[
            pl.BlockSpec((1, MOBA_BLOCK, MIX_WIDTH), lambda bi, ti: (bi, ti, 0)),
            pl.BlockSpec((1, MOBA_BLOCK, MIX_WIDTH), lambda bi, ti: (bi, ti, 0)),
            pl.BlockSpec((1, 1, 1, MIX_WIDTH), lambda bi, ti: (bi, ti, 0, 0)),
        ],
        out_shape=[
            jax.ShapeDtypeStruct((bsz, seq, MIX_WIDTH), BF16),
            jax.ShapeDtypeStruct((bsz, seq, MIX_WIDTH), BF16),
            jax.ShapeDtypeStruct((bsz, n_blocks, 1, MIX_WIDTH), F32),
        ],
        compiler_params=_params(("arbitrary", "arbitrary")),
        name="shared_kv",
    )(x, pos, freq, w_k, w_kr, w_v)


def _query_b_kernel(x_ref, pos_ref, freq_ref, wq_ref, wqr_ref, mk_ref, mv_ref, q_ref, mo_ref):
    xb = x_ref[0].astype(BF16)
    proj = _dot(xb, wq_ref[...])
    cos, sin = _rope_tables(pos_ref[0], freq_ref[...], MIX_WIDTH // LANES)
    q = proj[:, :MIX_WIDTH] * cos + _dot(xb, wqr_ref[...]) * sin
    q_ref[0] = (q * ATTN_SCALE).astype(BF16)
    mo_ref[0] = _mem_attention(proj[:, MIX_WIDTH:], mk_ref[0], mv_ref[0]).astype(BF16)


def _query_b(x, pos, freq, w_q, w_qr, mem_kv, layer):
    bsz, seq, d = x.shape
    tile = min(SEQ_TILE, seq)
    mem_len = mem_kv.shape[1]
    full = lambda bi, ti: (0, 0)
    return pl.pallas_call(
        _query_b_kernel,
        grid=(bsz, seq // tile),
        in_specs=[
            pl.BlockSpec((1, tile, d), lambda bi, ti: (bi, ti, 0)),
            pl.BlockSpec((1, tile, 1), lambda bi, ti: (bi, ti, 0)),
            pl.BlockSpec(freq.shape, full),
            pl.BlockSpec(w_q.shape, full),
            pl.BlockSpec(w_qr.shape, full),
            pl.BlockSpec((1, mem_len, MEM_WIDTH), lambda bi, ti: (bi, 0, 2 * layer)),
            pl.BlockSpec((1, mem_len, MEM_WIDTH), lambda bi, ti: (bi, 0, 2 * layer + 1)),
        ],
        out_specs=[
            pl.BlockSpec((1, tile, MIX_WIDTH), lambda bi, ti: (bi, ti, 0)),
            pl.BlockSpec((1, tile, MEM_WIDTH), lambda bi, ti: (bi, ti, 0)),
        ],
        out_shape=[
            jax.ShapeDtypeStruct((bsz, seq, MIX_WIDTH), BF16),
            jax.ShapeDtypeStruct((bsz, seq, MEM_WIDTH), BF16),
        ],
        compiler_params=_params(("arbitrary", "arbitrary")),
        name="query_b",
    )(x, pos, freq, w_q, w_qr, mem_kv, mem_kv)


def _moba_kernel(q_ref, k_ref, v_ref, km_ref, o_ref, kaug_ref, vaug_ref, m_ref, acc_ref, p_ref,
                 scale_ref, *, chunk_blocks):
    blk = MOBA_BLOCK
    own = pl.program_id(2)
    rows = HEADS_PER_LANE_GROUP * blk
    span = chunk_blocks * blk

    @pl.when(own == 0)
    def _():
        kaug_ref[:, :LANES] = k_ref[0]
        vaug_ref[:, :LANES] = v_ref[0]
        lane = lax.broadcasted_iota(jnp.int32, (blk, LANES), 1)
        ones_lane = (lane == 0).astype(BF16)

        def fill(j, carry):
            off = pl.multiple_of(j * blk, blk)
            kaug_ref[pl.ds(off, blk), LANES:] = (lane == j).astype(BF16)
            vaug_ref[pl.ds(off, blk), LANES:] = ones_lane
            return carry

        lax.fori_loop(0, k_ref.shape[1] // blk, fill, 0)

    q = q_ref[0]
    q_heads = jnp.concatenate(
        [jnp.where(_head_lane_mask(LANES, h), q, jnp.zeros_like(q))
         for h in range(HEADS_PER_LANE_GROUP)], axis=0)

    block_id = lax.broadcasted_iota(jnp.int32, (rows, LANES), 1)
    is_past = block_id < own
    gate = jnp.where(is_past, _dot_nt(q_heads, km_ref[0].astype(BF16)), -jnp.inf)
    kth = gate
    for _ in range(MOBA_TOPK - 1):
        kth = jnp.where(kth == jnp.max(kth, axis=-1, keepdims=True), -jnp.inf, kth)
    threshold = jnp.max(kth, axis=-1, keepdims=True)
    allowed = (is_past & (gate >= threshold)) | (block_id == own)
    block_bias = jnp.where(allowed, 0.0, MASK_VALUE).astype(BF16)
    q_aug = jnp.concatenate([q_heads, block_bias], axis=1)

    head_rows = [slice(h * blk, (h + 1) * blk) for h in range(HEADS_PER_LANE_GROUP)]
    m_ref[...] = jnp.full(m_ref.shape, MASK_VALUE, F32)
    acc_ref[...] = jnp.zeros_like(acc_ref)

    def score_phase(c, causal):
        k_chunk = kaug_ref[pl.ds(pl.multiple_of(c * span, span), span), :]
        for r in head_rows:
            s = _dot_nt(q_aug[r], k_chunk)
            if causal:
                k_pos = c * span + lax.broadcasted_iota(jnp.int32, (blk, span), 1)
                q_pos = own * blk + lax.broadcasted_iota(jnp.int32, (blk, span), 0)
                s = jnp.where(k_pos <= q_pos, s, MASK_VALUE)
            m_old = m_ref[r, :]
            m_new = jnp.maximum(m_old, jnp.max(s, axis=-1, keepdims=True))
            p_ref[r, :] = jnp.exp(s - m_new).astype(BF16)
            scale_ref[r, :] = jnp.exp(m_old - m_new)
            m_ref[r, :] = m_new

    def value_phase(c):
        v_chunk = vaug_ref[pl.ds(pl.multiple_of(c * span, span), span), :]
        for r in head_rows:
            acc_ref[r, :] = scale_ref[r, :] * acc_ref[r, :] + _dot(p_ref[r, :], v_chunk)

    own_chunk = own // chunk_blocks
    score_phase(own_chunk, causal=True)

    def pipelined(c, carry):
        value_phase(jnp.where(c == 0, own_chunk, c - 1))
        score_phase(c, causal=False)
        return carry

    lax.fori_loop(0, own_chunk, pipelined, 0)
    value_phase(jnp.where(own_chunk == 0, 0, own_chunk - 1))

    acc = acc_ref[...]
    o = acc[:, :LANES] / acc[:, LANES:LANES + 1]
    out = o[:blk]
    for h in range(1, HEADS_PER_LANE_GROUP):
        out = jnp.where(_head_lane_mask(LANES, h), o[h * blk:(h + 1) * blk], out)
    o_ref[0] = out.astype(o_ref.dtype)


def _moba(q, k, v, k_means):
    bsz, seq, _ = q.shape
    n_blocks = seq // MOBA_BLOCK
    groups = MIX_WIDTH // LANES
    chunk_blocks = MOBA_CHUNK_BLOCKS if n_blocks % MOBA_CHUNK_BLOCKS == 0 else 1
    rows = HEADS_PER_LANE_GROUP * MOBA_BLOCK
    return pl.pallas_call(
        functools.partial(_moba_kernel, chunk_blocks=chunk_blocks),
        grid=(bsz, groups, n_blocks),
        in_specs=[
            pl.BlockSpec((1, MOBA_BLOCK, LANES), lambda bi, gi, qi: (bi, qi, gi)),
            pl.BlockSpec((1, seq, LANES), lambda bi, gi, qi: (bi, 0, gi)),
            pl.BlockSpec((1, seq, LANES), lambda bi, gi, qi: (bi, 0, gi)),
            pl.BlockSpec((1, LANES, LANES), lambda bi, gi, qi: (bi, 0, gi)),
        ],
        out_specs=pl.BlockSpec((1, MOBA_BLOCK, LANES), lambda bi, gi, qi: (bi, qi, gi)),
        out_shape=jax.ShapeDtypeStruct(q.shape, BF16),
        scratch_shapes=[pltpu.VMEM((seq, 2 * LANES), BF16), pltpu.VMEM((seq, 2 * LANES), BF16),
                        pltpu.VMEM((rows, 1), F32), pltpu.VMEM((rows, 2 * LANES), F32),
                        pltpu.VMEM((rows, chunk_blocks * MOBA_BLOCK), BF16),
                        pltpu.VMEM((rows, 1), F32)],
        compiler_params=_params(("arbitrary", "arbitrary", "arbitrary")),
        name="moba",
    )(q, k, v, k_means)


def _mixer_out_kernel(x_ref, main_ref, mo_ref, wo_ref, g_ref, b_ref, o_ref, *, alpha):
    mix = _dot(main_ref[0], wo_ref[:MIX_WIDTH, :]) + _dot(mo_ref[0], wo_ref[MIX_WIDTH:, :])
    o_ref[0] = _layer_norm(alpha * x_ref[0] + mix, g_ref[...], b_ref[...])


def _mixer_out(x, main, mem_out, w_o, g, b, alpha):
    bsz, seq, d = x.shape
    tile = min(SEQ_TILE, seq)
    full = lambda bi, ti: (0, 0)
    return pl.pallas_call(
        functools.partial(_mixer_out_kernel, alpha=alpha),
        grid=(bsz, seq // tile),
        in_specs=[
            pl.BlockSpec((1, tile, d), lambda bi, ti: (bi, ti, 0)),
            pl.BlockSpec((1, tile, MIX_WIDTH), lambda bi, ti: (bi, ti, 0)),
            pl.BlockSpec((1, tile, MEM_WIDTH), lambda bi, ti: (bi, ti, 0)),
            pl.BlockSpec(w_o.shape, full),
            pl.BlockSpec((1, d), full),
            pl.BlockSpec((1, d), full),
        ],
        out_specs=pl.BlockSpec((1, tile, d), lambda bi, ti: (bi, ti, 0)),
        out_shape=jax.ShapeDtypeStruct(x.shape, F32),
        compiler_params=_params(("arbitrary", "arbitrary")),
        name="mixer_out",
    )(x, main, mem_out, w_o, g, b)


def _swiglu_chunk(xb, wg, wu, wd):
    gp = _dot(xb, wg)
    up = _dot(xb, wu)
    return _dot((gp * jax.nn.sigmoid(gp) * up).astype(BF16), wd)


def _ffn_chunks(d_ff):
    chunk = FF_CHUNK if d_ff % FF_CHUNK == 0 else d_ff
    return chunk, d_ff // chunk


def _ffn_kernel(x_ref, wg_ref, wu_ref, wd_ref, g_ref, b_ref, o_ref, acc_ref, xb_ref, *, alpha):
    c = pl.program_id(1)

    @pl.when(c == 0)
    def _():
        acc_ref[...] = jnp.zeros_like(acc_ref)
        xb_ref[...] = x_ref[...].astype(BF16)

    acc_ref[...] += _swiglu_chunk(xb_ref[...], wg_ref[...], wu_ref[...], wd_ref[...])

    @pl.when(c == pl.num_programs(1) - 1)
    def _():
        o_ref[...] = _layer_norm(alpha * x_ref[...] + acc_ref[...], g_ref[...], b_ref[...])


def _ffn(x, w_gu, w_down, g, b, alpha):
    n, d = x.shape
    d_ff = w_down.shape[0]
    tile = min(FFN_TILE, n)
    chunk, n_chunks = _ffn_chunks(d_ff)
    return pl.pallas_call(
        functools.partial(_ffn_kernel, alpha=alpha),
        grid=(n // tile, n_chunks),
        in_specs=[
            pl.BlockSpec((tile, d), lambda i, c: (i, 0)),
            pl.BlockSpec((d, chunk), lambda i, c: (0, c)),
            pl.BlockSpec((d, chunk), lambda i, c: (0, c + n_chunks)),
            pl.BlockSpec((chunk, d), lambda i, c: (c, 0)),
            pl.BlockSpec((1, d), lambda i, c: (0, 0)),
            pl.BlockSpec((1, d), lambda i, c: (0, 0)),
        ],
        out_specs=pl.BlockSpec((tile, d), lambda i, c: (i, 0)),
        out_shape=jax.ShapeDtypeStruct(x.shape, F32),
        scratch_shapes=[pltpu.VMEM((tile, d), F32), pltpu.VMEM((tile, d), BF16)],
        compiler_params=_params(("arbitrary", "arbitrary")),
        name="ffn_dense",
    )(x, w_gu, w_gu, w_down, g, b)


META_EXPERT, META_RANK, META_GATE = 0, 2, 4


def _route_kernel(x_ref, wr_ref, meta_ref, counts_ref, *, n_experts):
    tile = x_ref.shape[0]

    @pl.when(pl.program_id(0) == 0)
    def _():
        counts_ref[...] = jnp.zeros_like(counts_ref)

    lane = lax.broadcasted_iota(jnp.int32, (tile, LANES), 1)
    logits = jnp.dot(x_ref[...], wr_ref[...], precision=lax.Precision.HIGHEST,
                     preferred_element_type=F32)
    logits = jnp.where(lane < n_experts, logits, -jnp.inf)
    top1 = jnp.max(logits, axis=-1, keepdims=True)
    idx1 = jnp.min(jnp.where(logits == top1, lane, LANES), axis=-1, keepdims=True)
    rest = jnp.where(lane == idx1, -jnp.inf, logits)
    top2 = jnp.max(rest, axis=-1, keepdims=True)
    idx2 = jnp.min(jnp.where(rest == top2, lane, LANES), axis=-1, keepdims=True)
    e2 = jnp.exp(top2 - top1)

    chosen = (lane == idx1) | (lane == idx2)
    earlier = (lax.broadcasted_iota(jnp.int32, (tile, tile), 1)
               < lax.broadcasted_iota(jnp.int32, (tile, tile), 0))
    rank = counts_ref[...] + _dot(earlier.astype(BF16), chosen.astype(BF16))
    rank1 = jnp.sum(jnp.where(lane == idx1, rank, 0.0), axis=-1, keepdims=True)
    rank2 = jnp.sum(jnp.where(lane == idx2, rank, 0.0), axis=-1, keepdims=True)
    counts_ref[...] += jnp.sum(chosen.astype(F32), axis=0, keepdims=True)

    fields = [idx1.astype(F32), idx2.astype(F32), rank1, rank2, 1.0 / (1.0 + e2), e2 / (1.0 + e2)]
    meta = jnp.zeros((tile, LANES), F32)
    for i, field in enumerate(fields):
        meta = jnp.where(lane == i, field, meta)
    meta_ref[...] = meta


def _route(x, w_router):
    n, d = x.shape
    tile = min(ROUTE_TILE, n)
    return pl.pallas_call(
        functools.partial(_route_kernel, n_experts=w_router.shape[1]),
        grid=(n // tile,),
        in_specs=[pl.BlockSpec((tile, d), lambda i: (i, 0)),
                  pl.BlockSpec((d, LANES), lambda i: (0, 0))],
        out_specs=[pl.BlockSpec((tile, LANES), lambda i: (i, 0)),
                   pl.BlockSpec((1, LANES), lambda i: (0, 0))],
        out_shape=[jax.ShapeDtypeStruct((n, LANES), F32), jax.ShapeDtypeStruct((1, LANES), F32)],
        compiler_params=_params(("arbitrary",)),
        name="moe_route",
    )(x, jnp.pad(w_router, ((0, 0), (0, LANES - w_router.shape[1]))))


def _row_copy(src_ref, src_row, dst_ref, dst_row, sem):
    return pltpu.make_async_copy(src_ref.at[pl.ds(src_row, 1), :], dst_ref.at[pl.ds(dst_row, 1), :], sem)


def _dispatch_kernel(pos_ref, x_ref, init_ref, xs_ref, sem, *, tile):
    del init_ref
    base = pl.program_id(0) * tile

    def copies(r):
        return [_row_copy(x_ref, base + r, xs_ref, pos_ref[0, 0, TOP_K * r + k], sem)
                for k in range(TOP_K)]

    def start(r, carry):
        for cp in copies(r):
            cp.start()
        return carry

    def wait(r, carry):
        for cp in copies(r):
            cp.wait()
        return carry

    lax.fori_loop(0, tile, start, 0)
    lax.fori_loop(0, tile, wait, 0)


def _dispatch(x, pos, n_rows):
    n, d = x.shape
    tile = min(ROUTE_TILE, n)
    any_spec = pl.BlockSpec(memory_space=pl.ANY)
    return pl.pallas_call(
        functools.partial(_dispatch_kernel, tile=tile),
        grid=(n // tile,),
        in_specs=[pl.BlockSpec((1, 1, TOP_K * tile), lambda i: (i, 0, 0), memory_space=pltpu.SMEM),
                  any_spec, any_spec],
        out_specs=any_spec,
        out_shape=jax.ShapeDtypeStruct((n_rows, d), F32),
        scratch_shapes=[pltpu.SemaphoreType.DMA],
        input_output_aliases={2: 0},
        compiler_params=_params(("arbitrary",)),
        name="moe_dispatch",
    )(pos.reshape(n // tile, 1, TOP_K * tile), x, jnp.zeros((n_rows, d), F32))


def _expert_ffn_kernel(tile_expert_ref, n_tiles_ref, xs_ref, wg_ref, wu_ref, wd_ref, o_ref,
                       acc_ref, xb_ref):
    del tile_expert_ref
    c = pl.program_id(1)

    @pl.when(pl.program_id(0) < n_tiles_ref[0])
    def _():
        @pl.when(c == 0)
        def _():
            acc_ref[...] = jnp.zeros_like(acc_ref)
            xb_ref[...] = xs_ref[...].astype(BF16)

        acc_ref[...] += _swiglu_chunk(xb_ref[...], wg_ref[0], wu_ref[0], wd_ref[0])

        @pl.when(c == pl.num_programs(1) - 1)
        def _():
            o_ref[...] = acc_ref[...]


def _expert_ffn(xs, tile_expert, n_tiles, w_gu, w_down):
    n_rows, d = xs.shape
    d_ff = w_down.shape[1]
    chunk, n_chunks = _ffn_chunks(d_ff)
    row_tile = lambda t, c, te, nt: (jnp.minimum(t, nt[0] - 1), 0)
    grid_spec = pltpu.PrefetchScalarGridSpec(
        num_scalar_prefetch=2,
        grid=(n_rows // EXPERT_TILE, n_chunks),
        in_specs=[
            pl.BlockSpec((EXPERT_TILE, d), row_tile),
            pl.BlockSpec((1, d, chunk), lambda t, c, te, nt: (te[t], 0, c)),
            pl.BlockSpec((1, d, chunk), lambda t, c, te, nt: (te[t], 0, c + n_chunks)),
            pl.BlockSpec((1, chunk, d), lambda t, c, te, nt: (te[t], c, 0)),
        ],
        out_specs=pl.BlockSpec((EXPERT_TILE, d), row_tile),
        scratch_shapes=[pltpu.VMEM((EXPERT_TILE, d), F32), pltpu.VMEM((EXPERT_TILE, d), BF16)],
    )
    return pl.pallas_call(
        _expert_ffn_kernel,
        grid_spec=grid_spec,
        out_shape=jax.ShapeDtypeStruct((n_rows, d), F32),
        compiler_params=_params(("arbitrary", "arbitrary")),
        name="moe_experts",
    )(tile_expert, n_tiles, xs, w_gu, w_gu, w_down)


def _combine_kernel(pos_ref, meta_ref, x_ref, g_ref, b_ref, ys_ref, o_ref, rows_ref, sem, *, alpha):
    tile = x_ref.shape[0]

    def copies(r):
        return [_row_copy(ys_ref, pos_ref[0, 0, TOP_K * r + k], rows_ref.at[k], r, sem)
                for k in range(TOP_K)]

    def start(r, carry):
        for cp in copies(r):
            cp.start()
        return carry

    def wait(r, carry):
        for cp in copies(r):
            cp.wait()
        return carry

    lax.fori_loop(0, tile, start, 0)
    lax.fori_loop(0, tile, wait, 0)

    meta = meta_ref[...]
    lane = lax.broadcasted_iota(jnp.int32, meta.shape, 1)
    ffn = jnp.zeros(x_ref.shape, F32)
    for k in range(TOP_K):
        gate = jnp.sum(jnp.where(lane == META_GATE + k, meta, 0.0), axis=-1, keepdims=True)
        ffn = ffn + gate * rows_ref[k]
    o_ref[...] = _layer_norm(alpha * x_ref[...] + ffn, g_ref[...], b_ref[...])


def _combine(x, meta, pos, ys, g, b, alpha):
    n, d = x.shape
    tile = min(COMBINE_TILE, n)
    return pl.pallas_call(
        functools.partial(_combine_kernel, alpha=alpha),
        grid=(n // tile,),
        in_specs=[pl.BlockSpec((1, 1, TOP_K * tile), lambda i: (i, 0, 0), memory_space=pltpu.SMEM),
                  pl.BlockSpec((tile, LANES), lambda i: (i, 0)),
                  pl.BlockSpec((tile, d), lambda i: (i, 0)),
                  pl.BlockSpec((1, d), lambda i: (0, 0)),
                  pl.BlockSpec((1, d), lambda i: (0, 0)),
                  pl.BlockSpec(memory_space=pl.ANY)],
        out_specs=pl.BlockSpec((tile, d), lambda i: (i, 0)),
        out_shape=jax.ShapeDtypeStruct(x.shape, F32),
        scratch_shapes=[pltpu.VMEM((TOP_K, tile, d), F32), pltpu.SemaphoreType.DMA],
        compiler_params=_params(("arbitrary",)),
        name="moe_combine",
    )(pos.reshape(n // tile, 1, TOP_K * tile), meta, x, g, b, ys)


def _moe(x, w_router, w_gu, w_down, g, b, alpha):
    n, d = x.shape
    n_experts = w_router.shape[1]
    meta, counts = _route(x, w_router)

    counts = counts[0, :n_experts].astype(jnp.int32)
    group_tiles = (counts + EXPERT_TILE - 1) // EXPERT_TILE
    tiles_end = jnp.cumsum(group_tiles)
    group_row0 = (tiles_end - group_tiles) * EXPERT_TILE
    n_grid_tiles = (TOP_K * n) // EXPERT_TILE + n_experts
    tile_expert = jnp.minimum(
        jnp.searchsorted(tiles_end, jnp.arange(n_grid_tiles, dtype=jnp.int32), side="right"),
        n_experts - 1).astype(jnp.int32)
    expert = meta[:, META_EXPERT:META_EXPERT + TOP_K].astype(jnp.int32)
    rank = meta[:, META_RANK:META_RANK + TOP_K].astype(jnp.int32)
    pos = (group_row0[expert] + rank).reshape(-1)

    xs = _dispatch(x, pos, n_grid_tiles * EXPERT_TILE)
    ys = _expert_ffn(xs, tile_expert, tiles_end[-1:].astype(jnp.int32), w_gu, w_down)
    return _combine(x, meta, pos, ys, g, b, alpha)


def _rotate_half_columns(w):
    half = ROPE_DIM // 2
    col = jnp.arange(w.shape[1])
    c = col % HEAD_DIM
    src = jnp.where(c < half, col + half, col - half)
    sign = jnp.where(c < half, -1.0, jnp.where(c < ROPE_DIM, 1.0, 0.0)).astype(w.dtype)
    return w[:, jnp.clip(src, 0, w.shape[1] - 1)] * sign


def _rope_lane_frequencies():
    inv_freq = ROPE_THETA ** (-jnp.arange(0, ROPE_DIM, 2, dtype=F32) / ROPE_DIM)
    c = jnp.arange(LANES) % HEAD_DIM
    return jnp.where(c < ROPE_DIM, inv_freq[c % (ROPE_DIM // 2)], 0.0).astype(F32)[None, :]


def kernel(x, mem, positions, w_in_a, conv_a, w_q_b, w_kv_shared, w_mem_kv, w_o, ln1_g, ln1_b,
           ln2_g, ln2_b, w_gu_dense, w_down_dense, w_router, w_gu_moe, w_down_moe):
    bsz, seq, d = x.shape
    depth = w_o.shape[0]
    n_a = w_in_a.shape[0]
    alpha = float((2 * depth) ** 0.25)
    assert seq % MOBA_BLOCK == 0 and seq // MOBA_BLOCK <= LANES

    pos = positions.astype(F32)[:, :, None]
    freq = _rope_lane_frequencies()
    mem_len = mem.shape[1]
    w_mem_all = jnp.transpose(w_mem_kv, (1, 0, 2)).reshape(d, depth * 2 * MEM_WIDTH).astype(BF16)
    mem_kv = _matmul(mem.reshape(bsz * mem_len, d), w_mem_all, BF16).reshape(bsz, mem_len, -1)

    shared = None
    for layer in range(depth):
        g1, b1 = ln1_g[layer][None, :], ln1_b[layer][None, :]
        g2, b2 = ln2_g[layer][None, :], ln2_b[layer][None, :]
        wo = w_o[layer].astype(BF16)
        if layer < n_a:
            x = _mixer_a(x, w_in_a[layer].astype(BF16), conv_a[layer], mem_kv, layer, wo, g1, b1, alpha)
        else:
            wq = w_q_b[layer - n_a]
            q, mem_out = _query_b(x, pos, freq, wq.astype(BF16),
                                  _rotate_half_columns(wq[:, :MIX_WIDTH]).astype(BF16), mem_kv, layer)
            main = _moba(q, *shared)
            x = _mixer_out(x, main, mem_out, wo, g1, b1, alpha)
        xf = x.reshape(bsz * seq, d)
        if layer % 2 == 0:
            xf = _ffn(xf, w_gu_dense[layer // 2].astype(BF16), w_down_dense[layer // 2].astype(BF16),
                      g2, b2, alpha)
        else:
            xf = _moe(xf, w_router[layer // 2], w_gu_moe[layer // 2].astype(BF16),
                      w_down_moe[layer // 2].astype(BF16), g2, b2, alpha)
        x = xf.reshape(bsz, seq, d)
        if layer == n_a - 1:
            wk = w_kv_shared[:, :MIX_WIDTH]
            k, v, km = _shared_kv(x, pos, freq, wk.astype(BF16), _rotate_half_columns(wk).astype(BF16),
                                  w_kv_shared[:, MIX_WIDTH:].astype(BF16))
            n_blocks = seq // MOBA_BLOCK
            km = jnp.pad(km.reshape(bsz, n_blocks, MIX_WIDTH), ((0, 0), (0, LANES - n_blocks), (0, 0)))
            shared = (k, v, km)
    return x
```

```python
import functools

import jax
import jax.numpy as jnp
from jax import lax
from jax.experimental import pallas as pl
from jax.experimental.pallas import tpu as pltpu

F32 = jnp.float32
BF16 = jnp.bfloat16

HEAD_DIM = 64
MIX_HEADS = 12
MIX_WIDTH = MIX_HEADS * HEAD_DIM
MEM_HEADS = 4
MEM_WIDTH = MEM_HEADS * HEAD_DIM
CONV_WIDTH = 3
ROPE_DIM = HEAD_DIM // 4
ROPE_THETA = 500000.0
MOBA_BLOCK = 256
MOBA_TOPK = 3
TOP_K = 2
LN_EPS = 1e-5
ATTN_SCALE = HEAD_DIM ** -0.5
LOG2_E = 1.4426950408889634

LANES = 128
HEADS_PER_LANE_GROUP = LANES // HEAD_DIM
MASK_VALUE = -1e30
VMEM_LIMIT_BYTES = 56 * 1024 * 1024

SEQ_TILE = 512
FFN_TILE = 1024
FF_CHUNK = 1408
MOBA_CHUNK_BLOCKS = 4
ROUTE_TILE = 1024
EXPERT_TILE = 512
COMBINE_TILE = 512
DMA_LOOP_UNROLL = 8


def _params(semantics):
    return pltpu.CompilerParams(dimension_semantics=semantics, vmem_limit_bytes=VMEM_LIMIT_BYTES)


def _layer_norm(r, g, b):
    mu = jnp.mean(r, axis=-1, keepdims=True)
    c = r - mu
    var = jnp.mean(c * c, axis=-1, keepdims=True)
    return c * lax.rsqrt(var + LN_EPS) * g + b


def _dot(a, b):
    return jnp.dot(a, b, preferred_element_type=F32)


def _dot_nt(a, b):
    return lax.dot_general(a, b, (((1,), (1,)), ((), ())), preferred_element_type=F32)


def _head_lane_mask(width, head):
    lane = lax.broadcasted_iota(jnp.int32, (1, width), 1)
    return (lane >= head * HEAD_DIM) & (lane < (head + 1) * HEAD_DIM)


def _mem_attention(q_mem, mem_k, mem_v):
    out = jnp.zeros(q_mem.shape, F32)
    for head in range(MEM_HEADS):
        hm = _head_lane_mask(MEM_WIDTH, head)
        qh = jnp.where(hm, q_mem * ATTN_SCALE, 0.0).astype(BF16)
        s = _dot_nt(qh, mem_k)
        m = jnp.max(s, axis=-1, keepdims=True)
        p = jnp.exp(s - m)
        l = jnp.sum(p, axis=-1, keepdims=True)
        o = _dot(p.astype(BF16), mem_v) / l
        out = jnp.where(hm, o, out)
    return out


def _rope_tables(pos, freq, reps):
    angle = pos * freq
    cos = jnp.cos(angle)
    sin = jnp.sin(angle)
    return (jnp.concatenate([cos] * reps, axis=1), jnp.concatenate([sin] * reps, axis=1))


def _matmul_kernel(x_ref, w_ref, o_ref):
    o_ref[...] = _dot(x_ref[...].astype(BF16), w_ref[...]).astype(o_ref.dtype)


def _matmul(x, w, out_dtype, tile_m=256):
    m, k = x.shape
    n = w.shape[1]
    return pl.pallas_call(
        _matmul_kernel,
        grid=(m // tile_m,),
        in_specs=[pl.BlockSpec((tile_m, k), lambda i: (i, 0)),
                  pl.BlockSpec((k, n), lambda i: (0, 0))],
        out_specs=pl.BlockSpec((tile_m, n), lambda i: (i, 0)),
        out_shape=jax.ShapeDtypeStruct((m, n), out_dtype),
        compiler_params=_params(("arbitrary",)),
        name="mem_kv_proj",
    )(x, w)


def _mixer_a_kernel(x_ref, win_ref, conv_ref, mk_ref, mv_ref, wo_ref, g_ref, b_ref, o_ref,
                    ztail_ref, *, alpha):
    tile = x_ref.shape[1]

    @pl.when(pl.program_id(1) == 0)
    def _():
        ztail_ref[...] = jnp.zeros_like(ztail_ref)

    x = x_ref[0]
    proj = _dot(x.astype(BF16), win_ref[...])
    u_b = proj[:, :MIX_WIDTH]
    u_c = proj[:, MIX_WIDTH:2 * MIX_WIDTH]
    u_h = proj[:, 2 * MIX_WIDTH:3 * MIX_WIDTH]
    q_mem = proj[:, 3 * MIX_WIDTH:]

    z = u_c * u_h
    tail = ztail_ref[...]
    ztail_ref[...] = z[tile - 8:, :]
    row = lax.broadcasted_iota(jnp.int32, z.shape, 0)
    z1 = jnp.where(row == 0, tail[7:8, :], pltpu.roll(z, 1, 0))
    z2 = jnp.where(row == 0, tail[6:7, :], jnp.where(row == 1, tail[7:8, :], pltpu.roll(z, 2, 0)))
    w = conv_ref[...]
    main = u_b * (w[0:1, :] * z2 + w[1:2, :] * z1 + w[2:3, :] * z)

    mem_out = _mem_attention(q_mem, mk_ref[0], mv_ref[0])
    mix = (_dot(main.astype(BF16), wo_ref[:MIX_WIDTH, :])
           + _dot(mem_out.astype(BF16), wo_ref[MIX_WIDTH:, :]))
    o_ref[0] = _layer_norm(alpha * x + mix, g_ref[...], b_ref[...])


def _mixer_a(x, w_in, conv_w, mem_kv, layer, w_o, g, b, alpha):
    bsz, seq, d = x.shape
    tile = min(SEQ_TILE, seq)
    mem_len = mem_kv.shape[1]
    return pl.pallas_call(
        functools.partial(_mixer_a_kernel, alpha=alpha),
        grid=(bsz, seq // tile),
        in_specs=[
            pl.BlockSpec((1, tile, d), lambda bi, ti: (bi, ti, 0)),
            pl.BlockSpec(w_in.shape, lambda bi, ti: (0, 0)),
            pl.BlockSpec(conv_w.shape, lambda bi, ti: (0, 0)),
            pl.BlockSpec((1, mem_len, MEM_WIDTH), lambda bi, ti: (bi, 0, 2 * layer)),
            pl.BlockSpec((1, mem_len, MEM_WIDTH), lambda bi, ti: (bi, 0, 2 * layer + 1)),
            pl.BlockSpec(w_o.shape, lambda bi, ti: (0, 0)),
            pl.BlockSpec((1, d), lambda bi, ti: (0, 0)),
            pl.BlockSpec((1, d), lambda bi, ti: (0, 0)),
        ],
        out_specs=pl.BlockSpec((1, tile, d), lambda bi, ti: (bi, ti, 0)),
        out_shape=jax.ShapeDtypeStruct(x.shape, F32),
        scratch_shapes=[pltpu.VMEM((8, MIX_WIDTH), F32)],
        compiler_params=_params(("arbitrary", "arbitrary")),
        name="mixer_a",
    )(x, w_in, conv_w, mem_kv, mem_kv, w_o, g, b)


def _shared_kv_kernel(x_ref, pos_ref, freq_ref, wk_ref, wkr_ref, wv_ref, k_ref, v_ref, km_ref):
    xb = x_ref[0].astype(BF16)
    cos, sin = _rope_tables(pos_ref[0], freq_ref[...], MIX_WIDTH // LANES)
    k = _dot(xb, wk_ref[...]) * cos + _dot(xb, wkr_ref[...]) * sin
    k_ref[0] = k.astype(BF16)
    v_ref[0] = _dot(xb, wv_ref[...]).astype(BF16)
    km_ref[0, 0] = jnp.mean(k, axis=0, keepdims=True)


def _shared_kv(x, pos, freq, w_k, w_kr, w_v):
    bsz, seq, d = x.shape
    n_blocks = seq // MOBA_BLOCK
    full = lambda bi, ti: (0, 0)
    return pl.pallas_call(
        _shared_kv_kernel,
        grid=(bsz, n_blocks),
        in_specs=[
            pl.BlockSpec((1, MOBA_BLOCK, d), lambda bi, ti: (bi, ti, 0)),
            pl.BlockSpec((1, MOBA_BLOCK, 1), lambda bi, ti: (bi, ti, 0)),
            pl.BlockSpec(freq.shape, full),
            pl.BlockSpec(w_k.shape, full),
            pl.BlockSpec(w_kr.shape, full),
            pl.BlockSpec(w_v.shape, full),
        ],
        out_specs=[
            pl.BlockSpec((1, MOBA_BLOCK, MIX_WIDTH), lambda bi, ti: (bi, ti, 0)),
            pl.BlockSpec((1, MOBA_BLOCK, MIX_WIDTH), lambda bi, ti: (bi, ti, 0)),
            pl.BlockSpec((1, 1, 1, MIX_WIDTH), lambda bi, ti: (bi, ti, 0, 0)),
        ],
        out_shape=[
            jax.ShapeDtypeStruct((bsz, seq, MIX_WIDTH), BF16),
            jax.ShapeDtypeStruct((bsz, seq, MIX_WIDTH), BF16),
            jax.ShapeDtypeStruct((bsz, n_blocks, 1, MIX_WIDTH), F32),
        ],
        compiler_params=_params(("arbitrary", "arbitrary")),
        name="shared_kv",
    )(x, pos, freq, w_k, w_kr, w_v)


def _query_b_kernel(x_ref, pos_ref, freq_ref, wq_ref, wqr_ref, mk_ref, mv_ref, q_ref, mo_ref):
    xb = x_ref[0].astype(BF16)
    proj = _dot(xb, wq_ref[...])
    cos, sin = _rope_tables(pos_ref[0], freq_ref[...], MIX_WIDTH // LANES)
    q = proj[:, :MIX_WIDTH] * cos + _dot(xb, wqr_ref[...]) * sin
    q_ref[0] = (q * (ATTN_SCALE * LOG2_E)).astype(BF16)
    mo_ref[0] = _mem_attention(proj[:, MIX_WIDTH:], mk_ref[0], mv_ref[0]).astype(BF16)


def _query_b(x, pos, freq, w_q, w_qr, mem_kv, layer):
    bsz, seq, d = x.shape
    tile = min(SEQ_TILE, seq)
    mem_len = mem_kv.shape[1]
    full = lambda bi, ti: (0, 0)
    return pl.pallas_call(
        _query_b_kernel,
        grid=(bsz, seq // tile),
        in_specs=[
            pl.BlockSpec((1, tile, d), lambda bi, ti: (bi, ti, 0)),
            pl.BlockSpec((1, tile, 1), lambda bi, ti: (bi, ti, 0)),
            pl.BlockSpec(freq.shape, full),
            pl.BlockSpec(w_q.shape, full),
            pl.BlockSpec(w_qr.shape, full),
            pl.BlockSpec((1, mem_len, MEM_WIDTH), lambda bi, ti: (bi, 0, 2 * layer)),
            pl.BlockSpec((1, mem_len, MEM_WIDTH), lambda bi, ti: (bi, 0, 2 * layer + 1)),
        ],
        out_specs=[
            pl.BlockSpec((1, tile, MIX_WIDTH), lambda bi, ti: (bi, ti, 0)),
            pl.BlockSpec((1, tile, MEM_WIDTH), lambda bi, ti: (bi, ti, 0)),
        ],
        out_shape=[
            jax.ShapeDtypeStruct((bsz, seq, MIX_WIDTH), BF16),
            jax.ShapeDtypeStruct((bsz, seq, MEM_WIDTH), BF16),
        ],
        compiler_params=_params(("arbitrary", "arbitrary")),
        name="query_b",
    )(x, pos, freq, w_q, w_qr, mem_kv, mem_kv)


def _moba_kernel(q_ref, k_ref, v_ref, km_ref, o_ref, kaug_ref, vaug_ref, m_ref, acc_ref, p_ref,
                 scale_ref, *, chunk_blocks):
    blk = MOBA_BLOCK
    span = chunk_blocks * blk
    own_chunk = pl.program_id(2)
    chains = [(u, h) for u in range(chunk_blocks) for h in range(HEADS_PER_LANE_GROUP)]
    rows = len(chains) * blk

    @pl.when(own_chunk == 0)
    def _():
        kaug_ref[:, :LANES] = k_ref[0]
        vaug_ref[:, :LANES] = v_ref[0]
        lane = lax.broadcasted_iota(jnp.int32, (blk, LANES), 1)
        ones_lane = (lane == 0).astype(BF16)

        def fill(j, carry):
            off = pl.multiple_of(j * blk, blk)
            kaug_ref[pl.ds(off, blk), LANES:] = (lane == j).astype(BF16)
            vaug_ref[pl.ds(off, blk), LANES:] = ones_lane
            return carry

        lax.fori_loop(0, k_ref.shape[1] // blk, fill, 0)

    q = q_ref[0]
    q_heads = jnp.concatenate(
        [jnp.where(_head_lane_mask(LANES, h), q[u * blk:(u + 1) * blk], 0).astype(BF16)
         for u, h in chains], axis=0)

    block_id = lax.broadcasted_iota(jnp.int32, (rows, LANES), 1)
    own = (own_chunk * chunk_blocks
           + lax.broadcasted_iota(jnp.int32, (rows, LANES), 0) // (HEADS_PER_LANE_GROUP * blk))
    is_past = block_id < own
    gate = jnp.where(is_past, _dot_nt(q_heads, km_ref[0].astype(BF16)), -jnp.inf)
    kth = gate
    for _ in range(MOBA_TOPK - 1):
        kth = jnp.where(kth == jnp.max(kth, axis=-1, keepdims=True), -jnp.inf, kth)
    threshold = jnp.max(kth, axis=-1, keepdims=True)
    allowed = (is_past & (gate >= threshold)) | (block_id == own)
    block_bias = jnp.where(allowed, 0.0, MASK_VALUE).astype(BF16)
    q_aug = jnp.concatenate([q_heads, block_bias], axis=1)

    m_ref[...] = jnp.full(m_ref.shape, MASK_VALUE, F32)
    acc_ref[...] = jnp.zeros_like(acc_ref)

    def score_phase(c, causal):
        k_chunk = kaug_ref[pl.ds(pl.multiple_of(c * span, span), span), :]
        for i, (u, _) in enumerate(chains):
            r = slice(i * blk, (i + 1) * blk)
            s = _dot_nt(q_aug[r], k_chunk)
            if causal:
                k_pos = lax.broadcasted_iota(jnp.int32, (blk, span), 1)
                q_pos = u * blk + lax.broadcasted_iota(jnp.int32, (blk, span), 0)
                s = jnp.where(k_pos <= q_pos, s, MASK_VALUE)
            m_old = m_ref[r, :]
            m_new = jnp.maximum(m_old, jnp.max(s, axis=-1, keepdims=True))
            p_ref[r, :] = jnp.exp2((s - m_new).astype(BF16))
            scale_ref[r, :] = jnp.exp2(m_old - m_new)
            m_ref[r, :] = m_new

    def value_phase(c):
        v_chunk = vaug_ref[pl.ds(pl.multiple_of(c * span, span), span), :]
        for i in range(len(chains)):
            r = slice(i * blk, (i + 1) * blk)
            acc_ref[r, :] = scale_ref[r, :] * acc_ref[r, :] + _dot(p_ref[r, :], v_chunk)

    score_phase(own_chunk, causal=True)

    def pipelined(c, carry):
        value_phase(jnp.where(c == 0, own_chunk, c - 1))
        score_phase(c, causal=False)
        return carry

    lax.fori_loop(0, own_chunk, pipelined, 0)
    value_phase(jnp.where(own_chunk == 0, 0, own_chunk - 1))

    acc = acc_ref[...]
    o = acc[:, :LANES] / acc[:, LANES:LANES + 1]
    for u in range(chunk_blocks):
        base = u * HEADS_PER_LANE_GROUP * blk
        out = o[base:base + blk]
        for h in range(1, HEADS_PER_LANE_GROUP):
            out = jnp.where(_head_lane_mask(LANES, h), o[base + h * blk:base + (h + 1) * blk], out)
        o_ref[0, u * blk:(u + 1) * blk, :] = out.astype(o_ref.dtype)


def _moba(q, k, v, k_means):
    bsz, seq, _ = q.shape
    n_blocks = seq // MOBA_BLOCK
    groups = MIX_WIDTH // LANES
    chunk_blocks = MOBA_CHUNK_BLOCKS if n_blocks % MOBA_CHUNK_BLOCKS == 0 else 1
    span = chunk_blocks * MOBA_BLOCK
    rows = HEADS_PER_LANE_GROUP * span
    return pl.pallas_call(
        functools.partial(_moba_kernel, chunk_blocks=chunk_blocks),
        grid=(bsz, groups, seq // span),
        in_specs=[
            pl.BlockSpec((1, span, LANES), lambda bi, gi, qi: (bi, qi, gi)),
            pl.BlockSpec((1, seq, LANES), lambda bi, gi, qi: (bi, 0, gi)),
            pl.BlockSpec((1, seq, LANES), lambda bi, gi, qi: (bi, 0, gi)),
            pl.BlockSpec((1, LANES, LANES), lambda bi, gi, qi: (bi, 0, gi)),
        ],
        out_specs=pl.BlockSpec((1, span, LANES), lambda bi, gi, qi: (bi, qi, gi)),
        out_shape=jax.ShapeDtypeStruct(q.shape, BF16),
        scratch_shapes=[pltpu.VMEM((seq, 2 * LANES), BF16), pltpu.VMEM((seq, 2 * LANES), BF16),
                        pltpu.VMEM((rows, 1), F32), pltpu.VMEM((rows, 2 * LANES), F32),
                        pltpu.VMEM((rows, span), BF16),
                        pltpu.VMEM((rows, 1), F32)],
        compiler_params=_params(("arbitrary", "arbitrary", "arbitrary")),
        name="moba",
    )(q, k, v, k_means)


def _mixer_out_kernel(x_ref, main_ref, mo_ref, wo_ref, g_ref, b_ref, o_ref, *, alpha):
    mix = _dot(main_ref[0], wo_ref[:MIX_WIDTH, :]) + _dot(mo_ref[0], wo_ref[MIX_WIDTH:, :])
    o_ref[0] = _layer_norm(alpha * x_ref[0] + mix, g_ref[...], b_ref[...])


def _mixer_out(x, main, mem_out, w_o, g, b, alpha):
    bsz, seq, d = x.shape
    tile = min(SEQ_TILE, seq)
    full = lambda bi, ti: (0, 0)
    return pl.pallas_call(
        functools.partial(_mixer_out_kernel, alpha=alpha),
        grid=(bsz, seq // tile),
        in_specs=[
            pl.BlockSpec((1, tile, d), lambda bi, ti: (bi, ti, 0)),
            pl.BlockSpec((1, tile, MIX_WIDTH), lambda bi, ti: (bi, ti, 0)),
            pl.BlockSpec((1, tile, MEM_WIDTH), lambda bi, ti: (bi, ti, 0)),
            pl.BlockSpec(w_o.shape, full),
            pl.BlockSpec((1, d), full),
            pl.BlockSpec((1, d), full),
        ],
        out_specs=pl.BlockSpec((1, tile, d), lambda bi, ti: (bi, ti, 0)),
        out_shape=jax.ShapeDtypeStruct(x.shape, F32),
        compiler_params=_params(("arbitrary", "arbitrary")),
        name="mixer_out",
    )(x, main, mem_out, w_o, g, b)


def _swiglu_chunk(xb, wg, wu, wd):
    gp = _dot(xb, wg)
    up = _dot(xb, wu)
    return _dot((gp * jax.nn.sigmoid(gp) * up).astype(BF16), wd)


def _ffn_chunks(d_ff):
    chunk = FF_CHUNK if d_ff % FF_CHUNK == 0 else d_ff
    return chunk, d_ff // chunk


def _ffn_kernel(x_ref, wg_ref, wu_ref, wd_ref, g_ref, b_ref, o_ref, acc_ref, xb_ref, *, alpha):
    c = pl.program_id(1)

    @pl.when(c == 0)
    def _():
        acc_ref[...] = jnp.zeros_like(acc_ref)
        xb_ref[...] = x_ref[...].astype(BF16)

    acc_ref[...] += _swiglu_chunk(xb_ref[...], wg_ref[...], wu_ref[...], wd_ref[...])

    @pl.when(c == pl.num_programs(1) - 1)
    def _():
        o_ref[...] = _layer_norm(alpha * x_ref[...] + acc_ref[...], g_ref[...], b_ref[...])


def _ffn(x, w_gu, w_down, g, b, alpha):
    n, d = x.shape
    d_ff = w_down.shape[0]
    tile = min(FFN_TILE, n)
    chunk, n_chunks = _ffn_chunks(d_ff)
    return pl.pallas_call(
        functools.partial(_ffn_kernel, alpha=alpha),
        grid=(n // tile, n_chunks),
        in_specs=[
            pl.BlockSpec((tile, d), lambda i, c: (i, 0)),
            pl.BlockSpec((d, chunk), lambda i, c: (0, c)),
            pl.BlockSpec((d, chunk), lambda i, c: (0, c + n_chunks)),
            pl.BlockSpec((chunk, d), lambda i, c: (c, 0)),
            pl.BlockSpec((1, d), lambda i, c: (0, 0)),
            pl.BlockSpec((1, d), lambda i, c: (0, 0)),
        ],
        out_specs=pl.BlockSpec((tile, d), lambda i, c: (i, 0)),
        out_shape=jax.ShapeDtypeStruct(x.shape, F32),
        scratch_shapes=[pltpu.VMEM((tile, d), F32), pltpu.VMEM((tile, d), BF16)],
        compiler_params=_params(("arbitrary", "arbitrary")),
        name="ffn_dense",
    )(x, w_gu, w_gu, w_down, g, b)


META_EXPERT, META_RANK, META_GATE = 0, 2, 4


def _route_kernel(x_ref, wr_ref, meta_ref, counts_ref, *, n_experts):
    tile = x_ref.shape[0]

    @pl.when(pl.program_id(0) == 0)
    def _():
        counts_ref[...] = jnp.zeros_like(counts_ref)

    lane = lax.broadcasted_iota(jnp.int32, (tile, LANES), 1)
    logits = jnp.dot(x_ref[...], wr_ref[...], precision=lax.Precision.HIGHEST,
                     preferred_element_type=F32)
    logits = jnp.where(lane < n_experts, logits, -jnp.inf)
    top1 = jnp.max(logits, axis=-1, keepdims=True)
    idx1 = jnp.min(jnp.where(logits == top1, lane, LANES), axis=-1, keepdims=True)
    rest = jnp.where(lane == idx1, -jnp.inf, logits)
    top2 = jnp.max(rest, axis=-1, keepdims=True)
    idx2 = jnp.min(jnp.where(rest == top2, lane, LANES), axis=-1, keepdims=True)
    e2 = jnp.exp(top2 - top1)

    chosen = (lane == idx1) | (lane == idx2)
    earlier = (lax.broadcasted_iota(jnp.int32, (tile, tile), 1)
               < lax.broadcasted_iota(jnp.int32, (tile, tile), 0))
    rank = counts_ref[...] + _dot(earlier.astype(BF16), chosen.astype(BF16))
    rank1 = jnp.sum(jnp.where(lane == idx1, rank, 0.0), axis=-1, keepdims=True)
    rank2 = jnp.sum(jnp.where(lane == idx2, rank, 0.0), axis=-1, keepdims=True)
    counts_ref[...] += jnp.sum(chosen.astype(F32), axis=0, keepdims=True)

    fields = [idx1.astype(F32), idx2.astype(F32), rank1, rank2, 1.0 / (1.0 + e2), e2 / (1.0 + e2)]
    meta = jnp.zeros((tile, LANES), F32)
    for i, field in enumerate(fields):
        meta = jnp.where(lane == i, field, meta)
    meta_ref[...] = meta


def _route(x, w_router):
    n, d = x.shape
    tile = min(ROUTE_TILE, n)
    return pl.pallas_call(
        functools.partial(_route_kernel, n_experts=w_router.shape[1]),
        grid=(n // tile,),
        in_specs=[pl.BlockSpec((tile, d), lambda i: (i, 0)),
                  pl.BlockSpec((d, LANES), lambda i: (0, 0))],
        out_specs=[pl.BlockSpec((tile, LANES), lambda i: (i, 0)),
                   pl.BlockSpec((1, LANES), lambda i: (0, 0))],
        out_shape=[jax.ShapeDtypeStruct((n, LANES), F32), jax.ShapeDtypeStruct((1, LANES), F32)],
        compiler_params=_params(("arbitrary",)),
        name="moe_route",
    )(x, jnp.pad(w_router, ((0, 0), (0, LANES - w_router.shape[1]))))


def _row_copy(src_ref, src_row, dst_ref, dst_row, sem):
    return pltpu.make_async_copy(src_ref.at[pl.ds(src_row, 1), :], dst_ref.at[pl.ds(dst_row, 1), :], sem)


def _start_then_wait(copies, n):
    def start(r, carry):
        for cp in copies(r):
            cp.start()
        return carry

    def wait(r, carry):
        for cp in copies(r):
            cp.wait()
        return carry

    lax.fori_loop(0, n, start, 0, unroll=DMA_LOOP_UNROLL)
    lax.fori_loop(0, n, wait, 0, unroll=DMA_LOOP_UNROLL)


def _dispatch_kernel(pos_ref, x_ref, init_ref, xs_ref, sem):
    del init_ref

    def copies(r):
        return [_row_copy(x_ref, r, xs_ref, pos_ref[0, 0, TOP_K * r + k], sem) for k in range(TOP_K)]

    _start_then_wait(copies, x_ref.shape[0])


def _dispatch(x, pos, n_rows):
    n, d = x.shape
    tile = min(ROUTE_TILE, n)
    any_spec = pl.BlockSpec(memory_space=pl.ANY)
    return pl.pallas_call(
        _dispatch_kernel,
        grid=(n // tile,),
        in_specs=[pl.BlockSpec((1, 1, TOP_K * tile), lambda i: (i, 0, 0), memory_space=pltpu.SMEM),
                  pl.BlockSpec((tile, d), lambda i: (i, 0)), any_spec],
        out_specs=any_spec,
        out_shape=jax.ShapeDtypeStruct((n_rows, d), F32),
        scratch_shapes=[pltpu.SemaphoreType.DMA],
        input_output_aliases={2: 0},
        compiler_params=_params(("arbitrary",)),
        name="moe_dispatch",
    )(pos.reshape(n // tile, 1, TOP_K * tile), x, jnp.zeros((n_rows, d), F32))


def _expert_ffn_kernel(tile_expert_ref, n_tiles_ref, xs_ref, wg_ref, wu_ref, wd_ref, o_ref,
                       acc_ref, xb_ref):
    del tile_expert_ref
    c = pl.program_id(1)

    @pl.when(pl.program_id(0) < n_tiles_ref[0])
    def _():
        @pl.when(c == 0)
        def _():
            acc_ref[...] = jnp.zeros_like(acc_ref)
            xb_ref[...] = xs_ref[...].astype(BF16)

        acc_ref[...] += _swiglu_chunk(xb_ref[...], wg_ref[0], wu_ref[0], wd_ref[0])

        @pl.when(c == pl.num_programs(1) - 1)
        def _():
            o_ref[...] = acc_ref[...]


def _expert_ffn(xs, tile_expert, n_tiles, w_gu, w_down):
    n_rows, d = xs.shape
    d_ff = w_down.shape[1]
    chunk, n_chunks = _ffn_chunks(d_ff)
    row_tile = lambda t, c, te, nt: (jnp.minimum(t, nt[0] - 1), 0)
    grid_spec = pltpu.PrefetchScalarGridSpec(
        num_scalar_prefetch=2,
        grid=(n_rows // EXPERT_TILE, n_chunks),
        in_specs=[
            pl.BlockSpec((EXPERT_TILE, d), row_tile),
            pl.BlockSpec((1, d, chunk), lambda t, c, te, nt: (te[t], 0, c)),
            pl.BlockSpec((1, d, chunk), lambda t, c, te, nt: (te[t], 0, c + n_chunks)),
            pl.BlockSpec((1, chunk, d), lambda t, c, te, nt: (te[t], c, 0)),
        ],
        out_specs=pl.BlockSpec((EXPERT_TILE, d), row_tile),
        scratch_shapes=[pltpu.VMEM((EXPERT_TILE, d), F32), pltpu.VMEM((EXPERT_TILE, d), BF16)],
    )
    return pl.pallas_call(
        _expert_ffn_kernel,
        grid_spec=grid_spec,
        out_shape=jax.ShapeDtypeStruct((n_rows, d), F32),
        compiler_params=_params(("arbitrary", "arbitrary")),
        name="moe_experts",
    )(tile_expert, n_tiles, xs, w_gu, w_gu, w_down)


def _combine_kernel(pos_ref, meta_ref, x_ref, g_ref, b_ref, ys_ref, o_ref, rows_ref, sem, *, alpha):
    tile = x_ref.shape[0]

    def copies(r):
        return [_row_copy(ys_ref, pos_ref[0, 0, TOP_K * r + k], rows_ref.at[k], r, sem)
                for k in range(TOP_K)]

    _start_then_wait(copies, tile)

    meta = meta_ref[...]
    lane = lax.broadcasted_iota(jnp.int32, meta.shape, 1)
    ffn = jnp.zeros(x_ref.shape, F32)
    for k in range(TOP_K):
        gate = jnp.sum(jnp.where(lane == META_GATE + k, meta, 0.0), axis=-1, keepdims=True)
        ffn = ffn + gate * rows_ref[k]
    o_ref[...] = _layer_norm(alpha * x_ref[...] + ffn, g_ref[...], b_ref[...])


def _combine(x, meta, pos, ys, g, b, alpha):
    n, d = x.shape
    tile = min(COMBINE_TILE, n)
    return pl.pallas_call(
        functools.partial(_combine_kernel, alpha=alpha),
        grid=(n // tile,),
        in_specs=[pl.BlockSpec((1, 1, TOP_K * tile), lambda i: (i, 0, 0), memory_space=pltpu.SMEM),
                  pl.BlockSpec((tile, LANES), lambda i: (i, 0)),
                  pl.BlockSpec((tile, d), lambda i: (i, 0)),
                  pl.BlockSpec((1, d), lambda i: (0, 0)),
                  pl.BlockSpec((1, d), lambda i: (0, 0)),
                  pl.BlockSpec(memory_space=pl.ANY)],
        out_specs=pl.BlockSpec((tile, d), lambda i: (i, 0)),
        out_shape=jax.ShapeDtypeStruct(x.shape, F32),
        scratch_shapes=[pltpu.VMEM((TOP_K, tile, d), F32), pltpu.SemaphoreType.DMA],
        compiler_params=_params(("arbitrary",)),
        name="moe_combine",
    )(pos.reshape(n // tile, 1, TOP_K * tile), meta, x, g, b, ys)


def _moe(x, w_router, w_gu, w_down, g, b, alpha):
    n, d = x.shape
    n_experts = w_router.shape[1]
    meta, counts = _route(x, w_router)

    counts = counts[0, :n_experts].astype(jnp.int32)
    group_tiles = (counts + EXPERT_TILE - 1) // EXPERT_TILE
    tiles_end = jnp.cumsum(group_tiles)
    group_row0 = (tiles_end - group_tiles) * EXPERT_TILE
    n_grid_tiles = (TOP_K * n) // EXPERT_TILE + n_experts
    tile_expert = jnp.minimum(
        jnp.searchsorted(tiles_end, jnp.arange(n_grid_tiles, dtype=jnp.int32), side="right"),
        n_experts - 1).astype(jnp.int32)
    expert = meta[:, META_EXPERT:META_EXPERT + TOP_K].astype(jnp.int32)
    rank = meta[:, META_RANK:META_RANK + TOP_K].astype(jnp.int32)
    pos = (group_row0[expert] + rank).reshape(-1)

    xs = _dispatch(x, pos, n_grid_tiles * EXPERT_TILE)
    ys = _expert_ffn(xs, tile_expert, tiles_end[-1:].astype(jnp.int32), w_gu, w_down)
    return _combine(x, meta, pos, ys, g, b, alpha)


def _rotate_half_columns(w):
    half = ROPE_DIM // 2
    col = jnp.arange(w.shape[1])
    c = col % HEAD_DIM
    src = jnp.where(c < half, col + half, col - half)
    sign = jnp.where(c < half, -1.0, jnp.where(c < ROPE_DIM, 1.0, 0.0)).astype(w.dtype)
    return w[:, jnp.clip(src, 0, w.shape[1] - 1)] * sign


def _rope_lane_frequencies():
    inv_freq = ROPE_THETA ** (-jnp.arange(0, ROPE_DIM, 2, dtype=F32) / ROPE_DIM)
    c = jnp.arange(LANES) % HEAD_DIM
    return jnp.where(c < ROPE_DIM, inv_freq[c % (ROPE_DIM // 2)], 0.0).astype(F32)[None, :]


def kernel(x, mem, positions, w_in_a, conv_a, w_q_b, w_kv_shared, w_mem_kv, w_o, ln1_g, ln1_b,
           ln2_g, ln2_b, w_gu_dense, w_down_dense, w_router, w_gu_moe, w_down_moe):
    bsz, seq, d = x.shape
    depth = w_o.shape[0]
    n_a = w_in_a.shape[0]
    alpha = float((2 * depth) ** 0.25)
    assert seq % MOBA_BLOCK == 0 and seq // MOBA_BLOCK <= LANES

    pos = positions.astype(F32)[:, :, None]
    freq = _rope_lane_frequencies()
    mem_len = mem.shape[1]
    w_mem_all = jnp.transpose(w_mem_kv, (1, 0, 2)).reshape(d, depth * 2 * MEM_WIDTH).astype(BF16)
    mem_kv = _matmul(mem.reshape(bsz * mem_len, d), w_mem_all, BF16).reshape(bsz, mem_len, -1)

    shared = None
    for layer in range(depth):
        g1, b1 = ln1_g[layer][None, :], ln1_b[layer][None, :]
        g2, b2 = ln2_g[layer][None, :], ln2_b[layer][None, :]
        wo = w_o[layer].astype(BF16)
        if layer < n_a:
            x = _mixer_a(x, w_in_a[layer].astype(BF16), conv_a[layer], mem_kv, layer, wo, g1, b1, alpha)
        else:
            wq = w_q_b[layer - n_a]
            q, mem_out = _query_b(x, pos, freq, wq.astype(BF16),
                                  _rotate_half_columns(wq[:, :MIX_WIDTH]).astype(BF16), mem_kv, layer)
            main = _moba(q, *shared)
            x = _mixer_out(x, main, mem_out, wo, g1, b1, alpha)
        xf = x.reshape(bsz * seq, d)
        if layer % 2 == 0:
            xf = _ffn(xf, w_gu_dense[layer // 2].astype(BF16), w_down_dense[layer // 2].astype(BF16),
                      g2, b2, alpha)
        else:
            xf = _moe(xf, w_router[layer // 2], w_gu_moe[layer // 2].astype(BF16),
                      w_down_moe[layer // 2].astype(BF16), g2, b2, alpha)
        x = xf.reshape(bsz, seq, d)
        if layer == n_a - 1:
            wk = w_kv_shared[:, :MIX_WIDTH]
            k, v, km = _shared_kv(x, pos, freq, wk.astype(BF16), _rotate_half_columns(wk).astype(BF16),
                                  w_kv_shared[:, MIX_WIDTH:].astype(BF16))
            n_blocks = seq // MOBA_BLOCK
            km = jnp.pad(km.reshape(bsz, n_blocks, MIX_WIDTH), ((0, 0), (0, LANES - n_blocks), (0, 0)))
            shared = (k, v, km)
    return x
```

```python
import functools

import jax
import jax.numpy as jnp
from jax import lax
from jax.experimental import pallas as pl
from jax.experimental.pallas import tpu as pltpu

F32 = jnp.float32
BF16 = jnp.bfloat16

HEAD_DIM = 64
MIX_HEADS = 12
MIX_WIDTH = MIX_HEADS * HEAD_DIM
MEM_HEADS = 4
MEM_WIDTH = MEM_HEADS * HEAD_DIM
CONV_WIDTH = 3
ROPE_DIM = HEAD_DIM // 4
ROPE_THETA = 500000.0
MOBA_BLOCK = 256
MOBA_TOPK = 3
TOP_K = 2
LN_EPS = 1e-5
ATTN_SCALE = HEAD_DIM ** -0.5
LOG2_E = 1.4426950408889634

LANES = 128
HEADS_PER_LANE_GROUP = LANES // HEAD_DIM
MASK_VALUE = -1e30
VMEM_LIMIT_BYTES = 56 * 1024 * 1024

SEQ_TILE = 512
FFN_TILE = 1024
FF_CHUNK = 1408
MOBA_CHUNK_BLOCKS = 4
ROUTE_TILE = 1024
EXPERT_TILE = 512
COMBINE_TILE = 512
DMA_LOOP_UNROLL = 8


def _params(semantics):
    return pltpu.CompilerParams(dimension_semantics=semantics, vmem_limit_bytes=VMEM_LIMIT_BYTES)


def _layer_norm(r, g, b):
    mu = jnp.mean(r, axis=-1, keepdims=True)
    c = r - mu
    var = jnp.mean(c * c, axis=-1, keepdims=True)
    return c * lax.rsqrt(var + LN_EPS) * g + b


def _dot(a, b):
    return jnp.dot(a, b, preferred_element_type=F32)


def _dot_nt(a, b):
    return lax.dot_general(a, b, (((1,), (1,)), ((), ())), preferred_element_type=F32)


def _head_lane_mask(width, head):
    lane = lax.broadcasted_iota(jnp.int32, (1, width), 1)
    return (lane >= head * HEAD_DIM) & (lane < (head + 1) * HEAD_DIM)


def _mem_attention(q_mem, mem_k, mem_v):
    out = jnp.zeros(q_mem.shape, F32)
    for head in range(MEM_HEADS):
        hm = _head_lane_mask(MEM_WIDTH, head)
        qh = jnp.where(hm, q_mem * ATTN_SCALE, 0.0).astype(BF16)
        s = _dot_nt(qh, mem_k)
        m = jnp.max(s, axis=-1, keepdims=True)
        p = jnp.exp(s - m)
        l = jnp.sum(p, axis=-1, keepdims=True)
        o = _dot(p.astype(BF16), mem_v) / l
        out = jnp.where(hm, o, out)
    return out


def _rope_tables(pos, freq, reps):
    angle = pos * freq
    cos = jnp.cos(angle)
    sin = jnp.sin(angle)
    return (jnp.concatenate([cos] * reps, axis=1), jnp.concatenate([sin] * reps, axis=1))


def _matmul_kernel(x_ref, w_ref, o_ref):
    o_ref[...] = _dot(x_ref[...].astype(BF16), w_ref[...]).astype(o_ref.dtype)


def _matmul(x, w, out_dtype, tile_m=256):
    m, k = x.shape
    n = w.shape[1]
    return pl.pallas_call(
        _matmul_kernel,
        grid=(m // tile_m,),
        in_specs=[pl.BlockSpec((tile_m, k), lambda i: (i, 0)),
                  pl.BlockSpec((k, n), lambda i: (0, 0))],
        out_specs=pl.BlockSpec((tile_m, n), lambda i: (i, 0)),
        out_shape=jax.ShapeDtypeStruct((m, n), out_dtype),
        compiler_params=_params(("arbitrary",)),
        name="mem_kv_proj",
    )(x, w)


def _mixer_a_kernel(x_ref, win_ref, conv_ref, mk_ref, mv_ref, wo_ref, g_ref, b_ref, o_ref,
                    ztail_ref, *, alpha):
    tile = x_ref.shape[1]

    @pl.when(pl.program_id(1) == 0)
    def _():
        ztail_ref[...] = jnp.zeros_like(ztail_ref)

    x = x_ref[0]
    proj = _dot(x.astype(BF16), win_ref[...])
    u_b = proj[:, :MIX_WIDTH]
    u_c = proj[:, MIX_WIDTH:2 * MIX_WIDTH]
    u_h = proj[:, 2 * MIX_WIDTH:3 * MIX_WIDTH]
    q_mem = proj[:, 3 * MIX_WIDTH:]

    z = u_c * u_h
    tail = ztail_ref[...]
    ztail_ref[...] = z[tile - 8:, :]
    row = lax.broadcasted_iota(jnp.int32, z.shape, 0)
    z1 = jnp.where(row == 0, tail[7:8, :], pltpu.roll(z, 1, 0))
    z2 = jnp.where(row == 0, tail[6:7, :], jnp.where(row == 1, tail[7:8, :], pltpu.roll(z, 2, 0)))
    w = conv_ref[...]
    main = u_b * (w[0:1, :] * z2 + w[1:2, :] * z1 + w[2:3, :] * z)

    mem_out = _mem_attention(q_mem, mk_ref[0], mv_ref[0])
    mix = (_dot(main.astype(BF16), wo_ref[:MIX_WIDTH, :])
           + _dot(mem_out.astype(BF16), wo_ref[MIX_WIDTH:, :]))
    o_ref[0] = _layer_norm(alpha * x + mix, g_ref[...], b_ref[...])


def _mixer_a(x, w_in, conv_w, mem_kv, layer, w_o, g, b, alpha):
    bsz, seq, d = x.shape
    tile = min(SEQ_TILE, seq)
    mem_len = mem_kv.shape[1]
    return pl.pallas_call(
        functools.partial(_mixer_a_kernel, alpha=alpha),
        grid=(bsz, seq // tile),
        in_specs=[
            pl.BlockSpec((1, tile, d), lambda bi, ti: (bi, ti, 0)),
            pl.BlockSpec(w_in.shape, lambda bi, ti: (0, 0)),
            pl.BlockSpec(conv_w.shape, lambda bi, ti: (0, 0)),
            pl.BlockSpec((1, mem_len, MEM_WIDTH), lambda bi, ti: (bi, 0, 2 * layer)),
            pl.BlockSpec((1, mem_len, MEM_WIDTH), lambda bi, ti: (bi, 0, 2 * layer + 1)),
            pl.BlockSpec(w_o.shape, lambda bi, ti: (0, 0)),
            pl.BlockSpec((1, d), lambda bi, ti: (0, 0)),
            pl.BlockSpec((1, d), lambda bi, ti: (0, 0)),
        ],
        out_specs=pl.BlockSpec((1, tile, d), lambda bi, ti: (bi, ti, 0)),
        out_shape=jax.ShapeDtypeStruct(x.shape, F32),
        scratch_shapes=[pltpu.VMEM((8, MIX_WIDTH), F32)],
        compiler_params=_params(("arbitrary", "arbitrary")),
        name="mixer_a",
    )(x, w_in, conv_w, mem_kv, mem_kv, w_o, g, b)


def _shared_kv_kernel(x_ref, pos_ref, freq_ref, wk_ref, wkr_ref, wv_ref, k_ref, v_ref, km_ref):
    xb = x_ref[0].astype(BF16)
    cos, sin = _rope_tables(pos_ref[0], freq_ref[...], MIX_WIDTH // LANES)
    k = _dot(xb, wk_ref[...]) * cos + _dot(xb, wkr_ref[...]) * sin
    k_ref[0] = k.astype(BF16)
    v_ref[0] = _dot(xb, wv_ref[...]).astype(BF16)
    km_ref[0, 0] = jnp.mean(k, axis=0, keepdims=True)


def _shared_kv(x, pos, freq, w_k, w_kr, w_v):
    bsz, seq, d = x.shape
    n_blocks = seq // MOBA_BLOCK
    full = lambda bi, ti: (0, 0)
    return pl.pallas_call(
        _shared_kv_kernel,
        grid=(bsz, n_blocks),
        in_specs=[
            pl.BlockSpec((1, MOBA_BLOCK, d), lambda bi, ti: (bi, ti, 0)),
            pl.BlockSpec((1, MOBA_BLOCK, 1), lambda bi, ti: (bi, ti, 0)),
            pl.BlockSpec(freq.shape, full),
            pl.BlockSpec(w_k.shape, full),
            pl.BlockSpec(w_kr.shape, full),
            pl.BlockSpec(w_v.shape, full),
        ],
        out_specs=[
            pl.BlockSpec((1, MOBA_BLOCK, MIX_WIDTH), lambda bi, ti: (bi, ti, 0)),
            pl.BlockSpec((1, MOBA_BLOCK, MIX_WIDTH), lambda bi, ti: (bi, ti, 0)),
            pl.BlockSpec((1, 1, 1, MIX_WIDTH), lambda bi, ti: (bi, ti, 0, 0)),
        ],
        out_shape=[
            jax.ShapeDtypeStruct((bsz, seq, MIX_WIDTH), BF16),
            jax.ShapeDtypeStruct((bsz, seq, MIX_WIDTH), BF16),
            jax.ShapeDtypeStruct((bsz, n_blocks, 1, MIX_WIDTH), F32),
        ],
        compiler_params=_params(("arbitrary", "arbitrary")),
        name="shared_kv",
    )(x, pos, freq, w_k, w_kr, w_v)


def _query_b_kernel(x_ref, pos_ref, freq_ref, wq_ref, wqr_ref, mk_ref, mv_ref, q_ref, mo_ref):
    xb = x_ref[0].astype(BF16)
    proj = _dot(xb, wq_ref[...])
    cos, sin = _rope_tables(pos_ref[0], freq_ref[...], MIX_WIDTH // LANES)
    q = proj[:, :MIX_WIDTH] * cos + _dot(xb, wqr_ref[...]) * sin
    q_ref[0] = (q * (ATTN_SCALE * LOG2_E)).astype(BF16)
    mo_ref[0] = _mem_attention(proj[:, MIX_WIDTH:], mk_ref[0], mv_ref[0]).astype(BF16)


def _query_b(x, pos, freq, w_q, w_qr, mem_kv, layer):
    bsz, seq, d = x.shape
    tile = min(SEQ_TILE, seq)
    mem_len = mem_kv.shape[1]
    full = lambda bi, ti: (0, 0)
    return pl.pallas_call(
        _query_b_kernel,
        grid=(bsz, seq // tile),
        in_specs=[
            pl.BlockSpec((1, tile, d), lambda bi, ti: (bi, ti, 0)),
            pl.BlockSpec((1, tile, 1), lambda bi, ti: (bi, ti, 0)),
            pl.BlockSpec(freq.shape, full),
            pl.BlockSpec(w_q.shape, full),
            pl.BlockSpec(w_qr.shape, full),
            pl.BlockSpec((1, mem_len, MEM_WIDTH), lambda bi, ti: (bi, 0, 2 * layer)),
            pl.BlockSpec((1, mem_len, MEM_WIDTH), lambda bi, ti: (bi, 0, 2 * layer + 1)),
        ],
        out_specs=[
            pl.BlockSpec((1, tile, MIX_WIDTH), lambda bi, ti: (bi, ti, 0)),
            pl.BlockSpec((1, tile, MEM_WIDTH), lambda bi, ti: (bi, ti, 0)),
        ],
        out_shape=[
            jax.ShapeDtypeStruct((bsz, seq, MIX_WIDTH), BF16),
            jax.ShapeDtypeStruct((bsz, seq, MEM_WIDTH), BF16),
        ],
        compiler_params=_params(("arbitrary", "arbitrary")),
        name="query_b",
    )(x, pos, freq, w_q, w_qr, mem_kv, mem_kv)


def _moba_kernel(q_ref, k_ref, v_ref, km_ref, o_ref, kaug_ref, vaug_ref, m_ref, acc_ref, s0_ref,
                 s1_ref, *, chunk_blocks):
    blk = MOBA_BLOCK
    span = chunk_blocks * blk
    own_chunk = pl.program_id(2)
    chains = [(u, h) for u in range(chunk_blocks) for h in range(HEADS_PER_LANE_GROUP)]
    rows = len(chains) * blk

    @pl.when(own_chunk == 0)
    def _():
        kaug_ref[:, :LANES] = k_ref[0]
        vaug_ref[:, :LANES] = v_ref[0]
        lane = lax.broadcasted_iota(jnp.int32, (blk, LANES), 1)
        ones_lane = (lane == 0).astype(BF16)

        def fill(j, carry):
            off = pl.multiple_of(j * blk, blk)
            kaug_ref[pl.ds(off, blk), LANES:] = (lane == j).astype(BF16)
            vaug_ref[pl.ds(off, blk), LANES:] = ones_lane
            return carry

        lax.fori_loop(0, k_ref.shape[1] // blk, fill, 0)

    q = q_ref[0]
    q_heads = jnp.concatenate(
        [jnp.where(_head_lane_mask(LANES, h), q[u * blk:(u + 1) * blk], 0).astype(BF16)
         for u, h in chains], axis=0)

    block_id = lax.broadcasted_iota(jnp.int32, (rows, LANES), 1)
    own = (own_chunk * chunk_blocks
           + lax.broadcasted_iota(jnp.int32, (rows, LANES), 0) // (HEADS_PER_LANE_GROUP * blk))
    is_past = block_id < own
    gate = jnp.where(is_past, _dot_nt(q_heads, km_ref[0].astype(BF16)), -jnp.inf)
    kth = gate
    for _ in range(MOBA_TOPK - 1):
        kth = jnp.where(kth == jnp.max(kth, axis=-1, keepdims=True), -jnp.inf, kth)
    threshold = jnp.max(kth, axis=-1, keepdims=True)
    allowed = (is_past & (gate >= threshold)) | (block_id == own)
    block_bias = jnp.where(allowed, 0.0, MASK_VALUE).astype(BF16)
    q_aug = jnp.concatenate([q_heads, block_bias], axis=1)

    m_ref[...] = jnp.full(m_ref.shape, MASK_VALUE, F32)
    acc_ref[...] = jnp.zeros_like(acc_ref)
    chain_rows = [slice(i * blk, (i + 1) * blk) for i in range(len(chains))]

    def chunk_rows(c):
        return pl.ds(pl.multiple_of(c * span, span), span)

    def score_own(s_ref):
        for r, (u, _) in zip(chain_rows, chains):
            width = (u + 1) * blk
            k_part = kaug_ref[pl.ds(pl.multiple_of(own_chunk * span, span), width), :]
            k_pos = lax.broadcasted_iota(jnp.int32, (blk, width), 1)
            q_pos = u * blk + lax.broadcasted_iota(jnp.int32, (blk, width), 0)
            s_ref[r, :width] = jnp.where(k_pos <= q_pos, _dot_nt(q_aug[r], k_part), MASK_VALUE)
            if width < span:
                s_ref[r, width:] = jnp.full((blk, span - width), MASK_VALUE, F32)

    def score_past(c, s_ref):
        k_chunk = kaug_ref[chunk_rows(c), :]
        for r in chain_rows:
            s_ref[r, :] = _dot_nt(q_aug[r], k_chunk)

    def softmax_value(c, s_ref):
        v_chunk = vaug_ref[chunk_rows(c), :]
        for r in chain_rows:
            s = s_ref[r, :]
            m_old = m_ref[r, :]
            m_new = jnp.maximum(m_old, jnp.max(s, axis=-1, keepdims=True))
            p = jnp.exp2((s - m_new).astype(BF16))
            acc_ref[r, :] = jnp.exp2(m_old - m_new) * acc_ref[r, :] + _dot(p, v_chunk)
            m_ref[r, :] = m_new

    def step_chunk(j):
        return jnp.where(j == 0, own_chunk, j - 1)

    n_past = own_chunk
    score_own(s0_ref)

    def two_steps(i, carry):
        j = 2 * i + 1
        score_past(j - 1, s1_ref)
        softmax_value(step_chunk(j - 1), s0_ref)
        score_past(j, s0_ref)
        softmax_value(j - 1, s1_ref)
        return carry

    lax.fori_loop(0, n_past // 2, two_steps, 0)

    @pl.when(n_past % 2 == 1)
    def _():
        score_past(n_past - 1, s1_ref)
        softmax_value(step_chunk(n_past - 1), s0_ref)
        softmax_value(n_past - 1, s1_ref)

    @pl.when(n_past % 2 == 0)
    def _():
        softmax_value(step_chunk(n_past), s0_ref)

    acc = acc_ref[...]
    o = acc[:, :LANES] / acc[:, LANES:LANES + 1]
    for u in range(chunk_blocks):
        base = u * HEADS_PER_LANE_GROUP * blk
        out = o[base:base + blk]
        for h in range(1, HEADS_PER_LANE_GROUP):
            out = jnp.where(_head_lane_mask(LANES, h), o[base + h * blk:base + (h + 1) * blk], out)
        o_ref[0, u * blk:(u + 1) * blk, :] = out.astype(o_ref.dtype)


def _moba(q, k, v, k_means):
    bsz, seq, _ = q.shape
    n_blocks = seq // MOBA_BLOCK
    groups = MIX_WIDTH // LANES
    chunk_blocks = MOBA_CHUNK_BLOCKS if n_blocks % MOBA_CHUNK_BLOCKS == 0 else 1
    span = chunk_blocks * MOBA_BLOCK
    rows = HEADS_PER_LANE_GROUP * span
    return pl.pallas_call(
        functools.partial(_moba_kernel, chunk_blocks=chunk_blocks),
        grid=(bsz, groups, seq // span),
        in_specs=[
            pl.BlockSpec((1, span, LANES), lambda bi, gi, qi: (bi, qi, gi)),
            pl.BlockSpec((1, seq, LANES), lambda bi, gi, qi: (bi, 0, gi)),
            pl.BlockSpec((1, seq, LANES), lambda bi, gi, qi: (bi, 0, gi)),
            pl.BlockSpec((1, LANES, LANES), lambda bi, gi, qi: (bi, 0, gi)),
        ],
        out_specs=pl.BlockSpec((1, span, LANES), lambda bi, gi, qi: (bi, qi, gi)),
        out_shape=jax.ShapeDtypeStruct(q.shape, BF16),
        scratch_shapes=[pltpu.VMEM((seq, 2 * LANES), BF16), pltpu.VMEM((seq, 2 * LANES), BF16),
                        pltpu.VMEM((rows, 1), F32), pltpu.VMEM((rows, 2 * LANES), F32),
                        pltpu.VMEM((rows, span), F32), pltpu.VMEM((rows, span), F32)],
        compiler_params=_params(("arbitrary", "arbitrary", "arbitrary")),
        name="moba",
    )(q, k, v, k_means)


def _mixer_out_kernel(x_ref, main_ref, mo_ref, wo_ref, g_ref, b_ref, o_ref, *, alpha):
    mix = _dot(main_ref[0], wo_ref[:MIX_WIDTH, :]) + _dot(mo_ref[0], wo_ref[MIX_WIDTH:, :])
    o_ref[0] = _layer_norm(alpha * x_ref[0] + mix, g_ref[...], b_ref[...])


def _mixer_out(x, main, mem_out, w_o, g, b, alpha):
    bsz, seq, d = x.shape
    tile = min(SEQ_TILE, seq)
    full = lambda bi, ti: (0, 0)
    return pl.pallas_call(
        functools.partial(_mixer_out_kernel, alpha=alpha),
        grid=(bsz, seq // tile),
        in_specs=[
            pl.BlockSpec((1, tile, d), lambda bi, ti: (bi, ti, 0)),
            pl.BlockSpec((1, tile, MIX_WIDTH), lambda bi, ti: (bi, ti, 0)),
            pl.BlockSpec((1, tile, MEM_WIDTH), lambda bi, ti: (bi, ti, 0)),
            pl.BlockSpec(w_o.shape, full),
            pl.BlockSpec((1, d), full),
            pl.BlockSpec((1, d), full),
        ],
        out_specs=pl.BlockSpec((1, tile, d), lambda bi, ti: (bi, ti, 0)),
        out_shape=jax.ShapeDtypeStruct(x.shape, F32),
        compiler_params=_params(("arbitrary", "arbitrary")),
        name="mixer_out",
    )(x, main, mem_out, w_o, g, b)


def _swiglu_chunk(xb, wg, wu, wd):
    gp = _dot(xb, wg)
    up = _dot(xb, wu)
    return _dot((gp * jax.nn.sigmoid(gp) * up).astype(BF16), wd)


def _ffn_chunks(d_ff):
    chunk = FF_CHUNK if d_ff % FF_CHUNK == 0 else d_ff
    return chunk, d_ff // chunk


def _ffn_kernel(x_ref, wg_ref, wu_ref, wd_ref, g_ref, b_ref, o_ref, acc_ref, xb_ref, *, alpha):
    c = pl.program_id(1)

    @pl.when(c == 0)
    def _():
        acc_ref[...] = jnp.zeros_like(acc_ref)
        xb_ref[...] = x_ref[...].astype(BF16)

    acc_ref[...] += _swiglu_chunk(xb_ref[...], wg_ref[...], wu_ref[...], wd_ref[...])

    @pl.when(c == pl.num_programs(1) - 1)
    def _():
        o_ref[...] = _layer_norm(alpha * x_ref[...] + acc_ref[...], g_ref[...], b_ref[...])


def _ffn(x, w_gu, w_down, g, b, alpha):
    n, d = x.shape
    d_ff = w_down.shape[0]
    tile = min(FFN_TILE, n)
    chunk, n_chunks = _ffn_chunks(d_ff)
    return pl.pallas_call(
        functools.partial(_ffn_kernel, alpha=alpha),
        grid=(n // tile, n_chunks),
        in_specs=[
            pl.BlockSpec((tile, d), lambda i, c: (i, 0)),
            pl.BlockSpec((d, chunk), lambda i, c: (0, c)),
            pl.BlockSpec((d, chunk), lambda i, c: (0, c + n_chunks)),
            pl.BlockSpec((chunk, d), lambda i, c: (c, 0)),
            pl.BlockSpec((1, d), lambda i, c: (0, 0)),
            pl.BlockSpec((1, d), lambda i, c: (0, 0)),
        ],
        out_specs=pl.BlockSpec((tile, d), lambda i, c: (i, 0)),
        out_shape=jax.ShapeDtypeStruct(x.shape, F32),
        scratch_shapes=[pltpu.VMEM((tile, d), F32), pltpu.VMEM((tile, d), BF16)],
        compiler_params=_params(("arbitrary", "arbitrary")),
        name="ffn_dense",
    )(x, w_gu, w_gu, w_down, g, b)


META_EXPERT, META_RANK, META_GATE = 0, 2, 4


def _route_kernel(x_ref, wr_ref, meta_ref, counts_ref, *, n_experts):
    tile = x_ref.shape[0]

    @pl.when(pl.program_id(0) == 0)
    def _():
        counts_ref[...] = jnp.zeros_like(counts_ref)

    lane = lax.broadcasted_iota(jnp.int32, (tile, LANES), 1)
    logits = jnp.dot(x_ref[...], wr_ref[...], precision=lax.Precision.HIGHEST,
                     preferred_element_type=F32)
    logits = jnp.where(lane < n_experts, logits, -jnp.inf)
    top1 = jnp.max(logits, axis=-1, keepdims=True)
    idx1 = jnp.min(jnp.where(logits == top1, lane, LANES), axis=-1, keepdims=True)
    rest = jnp.where(lane == idx1, -jnp.inf, logits)
    top2 = jnp.max(rest, axis=-1, keepdims=True)
    idx2 = jnp.min(jnp.where(rest == top2, lane, LANES), axis=-1, keepdims=True)
    e2 = jnp.exp(top2 - top1)

    chosen = (lane == idx1) | (lane == idx2)
    earlier = (lax.broadcasted_iota(jnp.int32, (tile, tile), 1)
               < lax.broadcasted_iota(jnp.int32, (tile, tile), 0))
    rank = counts_ref[...] + _dot(earlier.astype(BF16), chosen.astype(BF16))
    rank1 = jnp.sum(jnp.where(lane == idx1, rank, 0.0), axis=-1, keepdims=True)
    rank2 = jnp.sum(jnp.where(lane == idx2, rank, 0.0), axis=-1, keepdims=True)
    counts_ref[...] += jnp.sum(chosen.astype(F32), axis=0, keepdims=True)

    fields = [idx1.astype(F32), idx2.astype(F32), rank1, rank2, 1.0 / (1.0 + e2), e2 / (1.0 + e2)]
    meta = jnp.zeros((tile, LANES), F32)
    for i, field in enumerate(fields):
        meta = jnp.where(lane == i, field, meta)
    meta_ref[...] = meta


def _route(x, w_router):
    n, d = x.shape
    tile = min(ROUTE_TILE, n)
    return pl.pallas_call(
        functools.partial(_route_kernel, n_experts=w_router.shape[1]),
        grid=(n // tile,),
        in_specs=[pl.BlockSpec((tile, d), lambda i: (i, 0)),
                  pl.BlockSpec((d, LANES), lambda i: (0, 0))],
        out_specs=[pl.BlockSpec((tile, LANES), lambda i: (i, 0)),
                   pl.BlockSpec((1, LANES), lambda i: (0, 0))],
        out_shape=[jax.ShapeDtypeStruct((n, LANES), F32), jax.ShapeDtypeStruct((1, LANES), F32)],
        compiler_params=_params(("arbitrary",)),
        name="moe_route",
    )(x, jnp.pad(w_router, ((0, 0), (0, LANES - w_router.shape[1]))))


def _row_copy(src_ref, src_row, dst_ref, dst_row, sem):
    return pltpu.make_async_copy(src_ref.at[pl.ds(src_row, 1), :], dst_ref.at[pl.ds(dst_row, 1), :], sem)


def _start_then_wait(copies, n):
    def start(r, carry):
        for cp in copies(r):
            cp.start()
        return carry

    def wait(r, carry):
        for cp in copies(r):
            cp.wait()
        return carry

    lax.fori_loop(0, n, start, 0, unroll=DMA_LOOP_UNROLL)
    lax.fori_loop(0, n, wait, 0, unroll=DMA_LOOP_UNROLL)


def _dispatch_kernel(pos_ref, x_ref, init_ref, xs_ref, sem):
    del init_ref

    def copies(r):
        return [_row_copy(x_ref, r, xs_ref, pos_ref[0, 0, TOP_K * r + k], sem) for k in range(TOP_K)]

    _start_then_wait(copies, x_ref.shape[0])


def _dispatch(x, pos, n_rows):
    n, d = x.shape
    tile = min(ROUTE_TILE, n)
    any_spec = pl.BlockSpec(memory_space=pl.ANY)
    return pl.pallas_call(
        _dispatch_kernel,
        grid=(n // tile,),
        in_specs=[pl.BlockSpec((1, 1, TOP_K * tile), lambda i: (i, 0, 0), memory_space=pltpu.SMEM),
                  pl.BlockSpec((tile, d), lambda i: (i, 0)), any_spec],
        out_specs=any_spec,
        out_shape=jax.ShapeDtypeStruct((n_rows, d), F32),
        scratch_shapes=[pltpu.SemaphoreType.DMA],
        input_output_aliases={2: 0},
        compiler_params=_params(("arbitrary",)),
        name="moe_dispatch",
    )(pos.reshape(n // tile, 1, TOP_K * tile), x, jnp.zeros((n_rows, d), F32))


def _expert_ffn_kernel(tile_expert_ref, n_tiles_ref, xs_ref, wg_ref, wu_ref, wd_ref, o_ref,
                       acc_ref, xb_ref):
    del tile_expert_ref
    c = pl.program_id(1)

    @pl.when(pl.program_id(0) < n_tiles_ref[0])
    def _():
        @pl.when(c == 0)
        def _():
            acc_ref[...] = jnp.zeros_like(acc_ref)
            xb_ref[...] = xs_ref[...].astype(BF16)

        acc_ref[...] += _swiglu_chunk(xb_ref[...], wg_ref[0], wu_ref[0], wd_ref[0])

        @pl.when(c == pl.num_programs(1) - 1)
        def _():
            o_ref[...] = acc_ref[...]


def _expert_ffn(xs, tile_expert, n_tiles, w_gu, w_down):
    n_rows, d = xs.shape
    d_ff = w_down.shape[1]
    chunk, n_chunks = _ffn_chunks(d_ff)
    row_tile = lambda t, c, te, nt: (jnp.minimum(t, nt[0] - 1), 0)
    grid_spec = pltpu.PrefetchScalarGridSpec(
        num_scalar_prefetch=2,
        grid=(n_rows // EXPERT_TILE, n_chunks),
        in_specs=[
            pl.BlockSpec((EXPERT_TILE, d), row_tile),
            pl.BlockSpec((1, d, chunk), lambda t, c, te, nt: (te[t], 0, c)),
            pl.BlockSpec((1, d, chunk), lambda t, c, te, nt: (te[t], 0, c + n_chunks)),
            pl.BlockSpec((1, chunk, d), lambda t, c, te, nt: (te[t], c, 0)),
        ],
        out_specs=pl.BlockSpec((EXPERT_TILE, d), row_tile),
        scratch_shapes=[pltpu.VMEM((EXPERT_TILE, d), F32), pltpu.VMEM((EXPERT_TILE, d), BF16)],
    )
    return pl.pallas_call(
        _expert_ffn_kernel,
        grid_spec=grid_spec,
        out_shape=jax.ShapeDtypeStruct((n_rows, d), F32),
        compiler_params=_params(("arbitrary", "arbitrary")),
        name="moe_experts",
    )(tile_expert, n_tiles, xs, w_gu, w_gu, w_down)


def _combine_kernel(pos_ref, meta_ref, x_ref, g_ref, b_ref, ys_ref, o_ref, rows_ref, sem, *, alpha):
    tile = x_ref.shape[0]

    def copies(r):
        return [_row_copy(ys_ref, pos_ref[0, 0, TOP_K * r + k], rows_ref.at[k], r, sem)
                for k in range(TOP_K)]

    _start_then_wait(copies, tile)

    meta = meta_ref[...]
    lane = lax.broadcasted_iota(jnp.int32, meta.shape, 1)
    ffn = jnp.zeros(x_ref.shape, F32)
    for k in range(TOP_K):
        gate = jnp.sum(jnp.where(lane == META_GATE + k, meta, 0.0), axis=-1, keepdims=True)
        ffn = ffn + gate * rows_ref[k]
    o_ref[...] = _layer_norm(alpha * x_ref[...] + ffn, g_ref[...], b_ref[...])


def _combine(x, meta, pos, ys, g, b, alpha):
    n, d = x.shape
    tile = min(COMBINE_TILE, n)
    return pl.pallas_call(
        functools.partial(_combine_kernel, alpha=alpha),
        grid=(n // tile,),
        in_specs=[pl.BlockSpec((1, 1, TOP_K * tile), lambda i: (i, 0, 0), memory_space=pltpu.SMEM),
                  pl.BlockSpec((tile, LANES), lambda i: (i, 0)),
                  pl.BlockSpec((tile, d), lambda i: (i, 0)),
                  pl.BlockSpec((1, d), lambda i: (0, 0)),
                  pl.BlockSpec((1, d), lambda i: (0, 0)),
                  pl.BlockSpec(memory_space=pl.ANY)],
        out_specs=pl.BlockSpec((tile, d), lambda i: (i, 0)),
        out_shape=jax.ShapeDtypeStruct(x.shape, F32),
        scratch_shapes=[pltpu.VMEM((TOP_K, tile, d), F32), pltpu.SemaphoreType.DMA],
        compiler_params=_params(("arbitrary",)),
        name="moe_combine",
    )(pos.reshape(n // tile, 1, TOP_K * tile), meta, x, g, b, ys)


def _moe(x, w_router, w_gu, w_down, g, b, alpha):
    n, d = x.shape
    n_experts = w_router.shape[1]
    meta, counts = _route(x, w_router)

    counts = counts[0, :n_experts].astype(jnp.int32)
    group_tiles = (counts + EXPERT_TILE - 1) // EXPERT_TILE
    tiles_end = jnp.cumsum(group_tiles)
    group_row0 = (tiles_end - group_tiles) * EXPERT_TILE
    n_grid_tiles = (TOP_K * n) // EXPERT_TILE + n_experts
    tile_expert = jnp.minimum(
        jnp.searchsorted(tiles_end, jnp.arange(n_grid_tiles, dtype=jnp.int32), side="right"),
        n_experts - 1).astype(jnp.int32)
    expert = meta[:, META_EXPERT:META_EXPERT + TOP_K].astype(jnp.int32)
    rank = meta[:, META_RANK:META_RANK + TOP_K].astype(jnp.int32)
    pos = (group_row0[expert] + rank).reshape(-1)

    xs = _dispatch(x, pos, n_grid_tiles * EXPERT_TILE)
    ys = _expert_ffn(xs, tile_expert, tiles_end[-1:].astype(jnp.int32), w_gu, w_down)
    return _combine(x, meta, pos, ys, g, b, alpha)


def _rotate_half_columns(w):
    half = ROPE_DIM // 2
    col = jnp.arange(w.shape[1])
    c = col % HEAD_DIM
    src = jnp.where(c < half, col + half, col - half)
    sign = jnp.where(c < half, -1.0, jnp.where(c < ROPE_DIM, 1.0, 0.0)).astype(w.dtype)
    return w[:, jnp.clip(src, 0, w.shape[1] - 1)] * sign


def _rope_lane_frequencies():
    inv_freq = ROPE_THETA ** (-jnp.arange(0, ROPE_DIM, 2, dtype=F32) / ROPE_DIM)
    c = jnp.arange(LANES) % HEAD_DIM
    return jnp.where(c < ROPE_DIM, inv_freq[c % (ROPE_DIM // 2)], 0.0).astype(F32)[None, :]


def kernel(x, mem, positions, w_in_a, conv_a, w_q_b, w_kv_shared, w_mem_kv, w_o, ln1_g, ln1_b,
           ln2_g, ln2_b, w_gu_dense, w_down_dense, w_router, w_gu_moe, w_down_moe):
    bsz, seq, d = x.shape
    depth = w_o.shape[0]
    n_a = w_in_a.shape[0]
    alpha = float((2 * depth) ** 0.25)
    assert seq % MOBA_BLOCK == 0 and seq // MOBA_BLOCK <= LANES

    pos = positions.astype(F32)[:, :, None]
    freq = _rope_lane_frequencies()
    mem_len = mem.shape[1]
    w_mem_all = jnp.transpose(w_mem_kv, (1, 0, 2)).reshape(d, depth * 2 * MEM_WIDTH).astype(BF16)
    mem_kv = _matmul(mem.reshape(bsz * mem_len, d), w_mem_all, BF16).reshape(bsz, mem_len, -1)

    shared = None
    for layer in range(depth):
        g1, b1 = ln1_g[layer][None, :], ln1_b[layer][None, :]
        g2, b2 = ln2_g[layer][None, :], ln2_b[layer][None, :]
        wo = w_o[layer].astype(BF16)
        if layer < n_a:
            x = _mixer_a(x, w_in_a[layer].astype(BF16), conv_a[layer], mem_kv, layer, wo, g1, b1, alpha)
        else:
            wq = w_q_b[layer - n_a]
            q, mem_out = _query_b(x, pos, freq, wq.astype(BF16),
                                  _rotate_half_columns(wq[:, :MIX_WIDTH]).astype(BF16), mem_kv, layer)
            main = _moba(q, *shared)
            x = _mixer_out(x, main, mem_out, wo, g1, b1, alpha)
        xf = x.reshape(bsz * seq, d)
        if layer % 2 == 0:
            xf = _ffn(xf, w_gu_dense[layer // 2].astype(BF16), w_down_dense[layer // 2].astype(BF16),
                      g2, b2, alpha)
        else:
            xf = _moe(xf, w_router[layer // 2], w_gu_moe[layer // 2].astype(BF16),
                      w_down_moe[layer // 2].astype(BF16), g2, b2, alpha)
        x = xf.reshape(bsz, seq, d)
        if layer == n_a - 1:
            wk = w_kv_shared[:, :MIX_WIDTH]
            k, v, km = _shared_kv(x, pos, freq, wk.astype(BF16), _rotate_half_columns(wk).astype(BF16),
                                  w_kv_shared[:, MIX_WIDTH:].astype(BF16))
            n_blocks = seq // MOBA_BLOCK
            km = jnp.pad(km.reshape(bsz, n_blocks, MIX_WIDTH), ((0, 0), (0, LANES - n_blocks), (0, 0)))
            shared = (k, v, km)
    return x
```

```python
import functools

import jax
import jax.numpy as jnp
from jax import lax
from jax.experimental import pallas as pl
from jax.experimental.pallas import tpu as pltpu

F32 = jnp.float32
BF16 = jnp.bfloat16

HEAD_DIM = 64
MIX_HEADS = 12
MIX_WIDTH = MIX_HEADS * HEAD_DIM
MEM_HEADS = 4
MEM_WIDTH = MEM_HEADS * HEAD_DIM
CONV_WIDTH = 3
ROPE_DIM = HEAD_DIM // 4
ROPE_THETA = 500000.0
MOBA_BLOCK = 256
MOBA_TOPK = 3
TOP_K = 2
LN_EPS = 1e-5
ATTN_SCALE = HEAD_DIM ** -0.5
LOG2_E = 1.4426950408889634

LANES = 128
HEADS_PER_LANE_GROUP = LANES // HEAD_DIM
MASK_VALUE = -1e30
VMEM_LIMIT_BYTES = 56 * 1024 * 1024

SEQ_TILE = 512
FFN_TILE = 1024
FF_CHUNK = 1408
MOBA_CHUNK_BLOCKS = 4
ROUTE_TILE = 1024
EXPERT_TILE = 512
COMBINE_TILE = 512
DMA_LOOP_UNROLL = 8


def _params(semantics):
    return pltpu.CompilerParams(dimension_semantics=semantics, vmem_limit_bytes=VMEM_LIMIT_BYTES)


def _layer_norm(r, g, b):
    mu = jnp.mean(r, axis=-1, keepdims=True)
    c = r - mu
    var = jnp.mean(c * c, axis=-1, keepdims=True)
    return c * lax.rsqrt(var + LN_EPS) * g + b


def _dot(a, b):
    return jnp.dot(a, b, preferred_element_type=F32)


def _dot_nt(a, b):
    return lax.dot_general(a, b, (((1,), (1,)), ((), ())), preferred_element_type=F32)


def _head_lane_mask(width, head):
    lane = lax.broadcasted_iota(jnp.int32, (1, width), 1)
    return (lane >= head * HEAD_DIM) & (lane < (head + 1) * HEAD_DIM)


def _mem_attention(q_mem, mem_k, mem_v):
    out = jnp.zeros(q_mem.shape, F32)
    for head in range(MEM_HEADS):
        hm = _head_lane_mask(MEM_WIDTH, head)
        qh = jnp.where(hm, q_mem * ATTN_SCALE, 0.0).astype(BF16)
        s = _dot_nt(qh, mem_k)
        m = jnp.max(s, axis=-1, keepdims=True)
        p = jnp.exp(s - m)
        l = jnp.sum(p, axis=-1, keepdims=True)
        o = _dot(p.astype(BF16), mem_v) / l
        out = jnp.where(hm, o, out)
    return out


def _rope_tables(pos, freq, reps):
    angle = pos * freq
    cos = jnp.cos(angle)
    sin = jnp.sin(angle)
    return (jnp.concatenate([cos] * reps, axis=1), jnp.concatenate([sin] * reps, axis=1))


def _matmul_kernel(x_ref, w_ref, o_ref):
    o_ref[...] = _dot(x_ref[...].astype(BF16), w_ref[...]).astype(o_ref.dtype)


def _matmul(x, w, out_dtype, tile_m=256):
    m, k = x.shape
    n = w.shape[1]
    return pl.pallas_call(
        _matmul_kernel,
        grid=(m // tile_m,),
        in_specs=[pl.BlockSpec((tile_m, k), lambda i: (i, 0)),
                  pl.BlockSpec((k, n), lambda i: (0, 0))],
        out_specs=pl.BlockSpec((tile_m, n), lambda i: (i, 0)),
        out_shape=jax.ShapeDtypeStruct((m, n), out_dtype),
        compiler_params=_params(("arbitrary",)),
        name="mem_kv_proj",
    )(x, w)


def _mixer_a_kernel(x_ref, win_ref, conv_ref, mk_ref, mv_ref, wo_ref, g_ref, b_ref, o_ref,
                    ztail_ref, *, alpha):
    tile = x_ref.shape[1]

    @pl.when(pl.program_id(1) == 0)
    def _():
        ztail_ref[...] = jnp.zeros_like(ztail_ref)

    x = x_ref[0]
    proj = _dot(x.astype(BF16), win_ref[...])
    u_b = proj[:, :MIX_WIDTH]
    u_c = proj[:, MIX_WIDTH:2 * MIX_WIDTH]
    u_h = proj[:, 2 * MIX_WIDTH:3 * MIX_WIDTH]
    q_mem = proj[:, 3 * MIX_WIDTH:]

    z = u_c * u_h
    tail = ztail_ref[...]
    ztail_ref[...] = z[tile - 8:, :]
    row = lax.broadcasted_iota(jnp.int32, z.shape, 0)
    z1 = jnp.where(row == 0, tail[7:8, :], pltpu.roll(z, 1, 0))
    z2 = jnp.where(row == 0, tail[6:7, :], jnp.where(row == 1, tail[7:8, :], pltpu.roll(z, 2, 0)))
    w = conv_ref[...]
    main = u_b * (w[0:1, :] * z2 + w[1:2, :] * z1 + w[2:3, :] * z)

    mem_out = _mem_attention(q_mem, mk_ref[0], mv_ref[0])
    mix = (_dot(main.astype(BF16), wo_ref[:MIX_WIDTH, :])
           + _dot(mem_out.astype(BF16), wo_ref[MIX_WIDTH:, :]))
    o_ref[0] = _layer_norm(alpha * x + mix, g_ref[...], b_ref[...])


def _mixer_a(x, w_in, conv_w, mem_kv, layer, w_o, g, b, alpha):
    bsz, seq, d = x.shape
    tile = min(SEQ_TILE, seq)
    mem_len = mem_kv.shape[1]
    return pl.pallas_call(
        functools.partial(_mixer_a_kernel, alpha=alpha),
        grid=(bsz, seq // tile),
        in_specs=[
            pl.BlockSpec((1, tile, d), lambda bi, ti: (bi, ti, 0)),
            pl.BlockSpec(w_in.shape, lambda bi, ti: (0, 0)),
            pl.BlockSpec(conv_w.shape, lambda bi, ti: (0, 0)),
            pl.BlockSpec((1, mem_len, MEM_WIDTH), lambda bi, ti: (bi, 0, 2 * layer)),
            pl.BlockSpec((1, mem_len, MEM_WIDTH), lambda bi, ti: (bi, 0, 2 * layer + 1)),
            pl.BlockSpec(w_o.shape, lambda bi, ti: (0, 0)),
            pl.BlockSpec((1, d), lambda bi, ti: (0, 0)),
            pl.BlockSpec((1, d), lambda bi, ti: (0, 0)),
        ],
        out_specs=pl.BlockSpec((1, tile, d), lambda bi, ti: (bi, ti, 0)),
        out_shape=jax.ShapeDtypeStruct(x.shape, F32),
        scratch_shapes=[pltpu.VMEM((8, MIX_WIDTH), F32)],
        compiler_params=_params(("arbitrary", "arbitrary")),
        name="mixer_a",
    )(x, w_in, conv_w, mem_kv, mem_kv, w_o, g, b)


def _shared_kv_kernel(x_ref, pos_ref, freq_ref, wk_ref, wkr_ref, wv_ref, k_ref, v_ref, km_ref):
    xb = x_ref[0].astype(BF16)
    cos, sin = _rope_tables(pos_ref[0], freq_ref[...], MIX_WIDTH // LANES)
    k = _dot(xb, wk_ref[...]) * cos + _dot(xb, wkr_ref[...]) * sin
    k_ref[0] = k.astype(BF16)
    v_ref[0] = _dot(xb, wv_ref[...]).astype(BF16)
    km_ref[0, 0] = jnp.mean(k, axis=0, keepdims=True)


def _shared_kv(x, pos, freq, w_k, w_kr, w_v):
    bsz, seq, d = x.shape
    n_blocks = seq // MOBA_BLOCK
    full = lambda bi, ti: (0, 0)
    return pl.pallas_call(
        _shared_kv_kernel,
        grid=(bsz, n_blocks),
        in_specs=[
            pl.BlockSpec((1, MOBA_BLOCK, d), lambda bi, ti: (bi, ti, 0)),
            pl.BlockSpec((1, MOBA_BLOCK, 1), lambda bi, ti: (bi, ti, 0)),
            pl.BlockSpec(freq.shape, full),
            pl.BlockSpec(w_k.shape, full),
            pl.BlockSpec(w_kr.shape, full),
            pl.BlockSpec(w_v.shape, full),
        ],
        out_specs=[
            pl.BlockSpec((1, MOBA_BLOCK, MIX_WIDTH), lambda bi, ti: (bi, ti, 0)),
            pl.BlockSpec((1, MOBA_BLOCK, MIX_WIDTH), lambda bi, ti: (bi, ti, 0)),
            pl.BlockSpec((1, 1, 1, MIX_WIDTH), lambda bi, ti: (bi, ti, 0, 0)),
        ],
        out_shape=[
            jax.ShapeDtypeStruct((bsz, seq, MIX_WIDTH), BF16),
            jax.ShapeDtypeStruct((bsz, seq, MIX_WIDTH), BF16),
            jax.ShapeDtypeStruct((bsz, n_blocks, 1, MIX_WIDTH), F32),
        ],
        compiler_params=_params(("arbitrary", "arbitrary")),
        name="shared_kv",
    )(x, pos, freq, w_k, w_kr, w_v)


def _query_b_kernel(x_ref, pos_ref, freq_ref, wq_ref, wqr_ref, mk_ref, mv_ref, q_ref, mo_ref):
    xb = x_ref[0].astype(BF16)
    proj = _dot(xb, wq_ref[...])
    cos, sin = _rope_tables(pos_ref[0], freq_ref[...], MIX_WIDTH // LANES)
    q = proj[:, :MIX_WIDTH] * cos + _dot(xb, wqr_ref[...]) * sin
    q_ref[0] = (q * (ATTN_SCALE * LOG2_E)).astype(BF16)
    mo_ref[0] = _mem_attention(proj[:, MIX_WIDTH:], mk_ref[0], mv_ref[0]).astype(BF16)


def _query_b(x, pos, freq, w_q, w_qr, mem_kv, layer):
    bsz, seq, d = x.shape
    tile = min(SEQ_TILE, seq)
    mem_len = mem_kv.shape[1]
    full = lambda bi, ti: (0, 0)
    return pl.pallas_call(
        _query_b_kernel,
        grid=(bsz, seq // tile),
        in_specs=[
            pl.BlockSpec((1, tile, d), lambda bi, ti: (bi, ti, 0)),
            pl.BlockSpec((1, tile, 1), lambda bi, ti: (bi, ti, 0)),
            pl.BlockSpec(freq.shape, full),
            pl.BlockSpec(w_q.shape, full),
            pl.BlockSpec(w_qr.shape, full),
            pl.BlockSpec((1, mem_len, MEM_WIDTH), lambda bi, ti: (bi, 0, 2 * layer)),
            pl.BlockSpec((1, mem_len, MEM_WIDTH), lambda bi, ti: (bi, 0, 2 * layer + 1)),
        ],
        out_specs=[
            pl.BlockSpec((1, tile, MIX_WIDTH), lambda bi, ti: (bi, ti, 0)),
            pl.BlockSpec((1, tile, MEM_WIDTH), lambda bi, ti: (bi, ti, 0)),
        ],
        out_shape=[
            jax.ShapeDtypeStruct((bsz, seq, MIX_WIDTH), BF16),
            jax.ShapeDtypeStruct((bsz, seq, MEM_WIDTH), BF16),
        ],
        compiler_params=_params(("arbitrary", "arbitrary")),
        name="query_b",
    )(x, pos, freq, w_q, w_qr, mem_kv, mem_kv)


def _moba_kernel(q_ref, k_ref, v_ref, km_ref, o_ref, kaug_ref, vaug_ref, m_ref, acc_ref, s0_ref,
                 s1_ref, *, chunk_blocks):
    blk = MOBA_BLOCK
    span = chunk_blocks * blk
    own_chunk = pl.program_id(2)
    chains = [(u, h) for u in range(chunk_blocks) for h in range(HEADS_PER_LANE_GROUP)]
    rows = len(chains) * blk

    @pl.when(own_chunk == 0)
    def _():
        kaug_ref[:, :LANES] = k_ref[0]
        vaug_ref[:, :LANES] = v_ref[0]
        lane = lax.broadcasted_iota(jnp.int32, (blk, LANES), 1)
        ones_lane = (lane == 0).astype(BF16)

        def fill(j, carry):
            off = pl.multiple_of(j * blk, blk)
            kaug_ref[pl.ds(off, blk), LANES:] = (lane == j).astype(BF16)
            vaug_ref[pl.ds(off, blk), LANES:] = ones_lane
            return carry

        lax.fori_loop(0, k_ref.shape[1] // blk, fill, 0)

    q = q_ref[0]
    q_heads = jnp.concatenate(
        [jnp.where(_head_lane_mask(LANES, h), q[u * blk:(u + 1) * blk], 0).astype(BF16)
         for u, h in chains], axis=0)

    block_id = lax.broadcasted_iota(jnp.int32, (rows, LANES), 1)
    own = (own_chunk * chunk_blocks
           + lax.broadcasted_iota(jnp.int32, (rows, LANES), 0) // (HEADS_PER_LANE_GROUP * blk))
    is_past = block_id < own
    gate = jnp.where(is_past, _dot_nt(q_heads, km_ref[0].astype(BF16)), -jnp.inf)
    kth = gate
    for _ in range(MOBA_TOPK - 1):
        kth = jnp.where(kth == jnp.max(kth, axis=-1, keepdims=True), -jnp.inf, kth)
    threshold = jnp.max(kth, axis=-1, keepdims=True)
    allowed = (is_past & (gate >= threshold)) | (block_id == own)
    block_bias = jnp.where(allowed, 0.0, MASK_VALUE).astype(BF16)
    q_aug = jnp.concatenate([q_heads, block_bias], axis=1)

    m_ref[...] = jnp.full(m_ref.shape, MASK_VALUE, F32)
    acc_ref[...] = jnp.zeros_like(acc_ref)
    chain_rows = [slice(i * blk, (i + 1) * blk) for i in range(len(chains))]

    def chunk_rows(c):
        return pl.ds(pl.multiple_of(c * span, span), span)

    def score_own(s_ref):
        for r, (u, _) in zip(chain_rows, chains):
            width = (u + 1) * blk
            k_part = kaug_ref[pl.ds(pl.multiple_of(own_chunk * span, span), width), :]
            k_pos = lax.broadcasted_iota(jnp.int32, (blk, width), 1)
            q_pos = u * blk + lax.broadcasted_iota(jnp.int32, (blk, width), 0)
            s_ref[r, :width] = jnp.where(k_pos <= q_pos, _dot_nt(q_aug[r], k_part), MASK_VALUE)
            if width < span:
                s_ref[r, width:] = jnp.full((blk, span - width), MASK_VALUE, F32)

    def score_past(c, s_ref):
        k_chunk = kaug_ref[chunk_rows(c), :]
        for r in chain_rows:
            s_ref[r, :] = _dot_nt(q_aug[r], k_chunk)

    def softmax_value(c, s_ref):
        v_chunk = vaug_ref[chunk_rows(c), :]
        for r in chain_rows:
            s = s_ref[r, :]
            m_old = m_ref[r, :]
            m_new = jnp.maximum(m_old, jnp.max(s, axis=-1, keepdims=True))
            p = jnp.exp2((s - m_new).astype(BF16))
            acc_ref[r, :] = jnp.exp2(m_old - m_new) * acc_ref[r, :] + _dot(p, v_chunk)
            m_ref[r, :] = m_new

    def step_chunk(j):
        return jnp.where(j == 0, own_chunk, j - 1)

    n_past = own_chunk
    score_own(s0_ref)

    def two_steps(i, carry):
        j = 2 * i + 1
        score_past(j - 1, s1_ref)
        softmax_value(step_chunk(j - 1), s0_ref)
        score_past(j, s0_ref)
        softmax_value(j - 1, s1_ref)
        return carry

    lax.fori_loop(0, n_past // 2, two_steps, 0)

    @pl.when(n_past % 2 == 1)
    def _():
        score_past(n_past - 1, s1_ref)
        softmax_value(step_chunk(n_past - 1), s0_ref)
        softmax_value(n_past - 1, s1_ref)

    @pl.when(n_past % 2 == 0)
    def _():
        softmax_value(step_chunk(n_past), s0_ref)

    acc = acc_ref[...]
    o = acc[:, :LANES] / acc[:, LANES:LANES + 1]
    for u in range(chunk_blocks):
        base = u * HEADS_PER_LANE_GROUP * blk
        out = o[base:base + blk]
        for h in range(1, HEADS_PER_LANE_GROUP):
            out = jnp.where(_head_lane_mask(LANES, h), o[base + h * blk:base + (h + 1) * blk], out)
        o_ref[0, u * blk:(u + 1) * blk, :] = out.astype(o_ref.dtype)


def _moba(q, k, v, k_means):
    bsz, seq, _ = q.shape
    n_blocks = seq // MOBA_BLOCK
    groups = MIX_WIDTH // LANES
    chunk_blocks = MOBA_CHUNK_BLOCKS if n_blocks % MOBA_CHUNK_BLOCKS == 0 else 1
    span = chunk_blocks * MOBA_BLOCK
    rows = HEADS_PER_LANE_GROUP * span
    return pl.pallas_call(
        functools.partial(_moba_kernel, chunk_blocks=chunk_blocks),
        grid=(bsz, groups, seq // span),
        in_specs=[
            pl.BlockSpec((1, span, LANES), lambda bi, gi, qi: (bi, qi, gi)),
            pl.BlockSpec((1, seq, LANES), lambda bi, gi, qi: (bi, 0, gi)),
            pl.BlockSpec((1, seq, LANES), lambda bi, gi, qi: (bi, 0, gi)),
            pl.BlockSpec((1, LANES, LANES), lambda bi, gi, qi: (bi, 0, gi)),
        ],
        out_specs=pl.BlockSpec((1, span, LANES), lambda bi, gi, qi: (bi, qi, gi)),
        out_shape=jax.ShapeDtypeStruct(q.shape, BF16),
        scratch_shapes=[pltpu.VMEM((seq, 2 * LANES), BF16), pltpu.VMEM((seq, 2 * LANES), BF16),
                        pltpu.VMEM((rows, 1), F32), pltpu.VMEM((rows, 2 * LANES), F32),
                        pltpu.VMEM((rows, span), F32), pltpu.VMEM((rows, span), F32)],
        compiler_params=_params(("arbitrary", "arbitrary", "arbitrary")),
        name="moba",
    )(q, k, v, k_means)


def _mixer_out_kernel(x_ref, main_ref, mo_ref, wo_ref, g_ref, b_ref, o_ref, *, alpha):
    mix = _dot(main_ref[0], wo_ref[:MIX_WIDTH, :]) + _dot(mo_ref[0], wo_ref[MIX_WIDTH:, :])
    o_ref[0] = _layer_norm(alpha * x_ref[0] + mix, g_ref[...], b_ref[...])


def _mixer_out(x, main, mem_out, w_o, g, b, alpha):
    bsz, seq, d = x.shape
    tile = min(SEQ_TILE, seq)
    full = lambda bi, ti: (0, 0)
    return pl.pallas_call(
        functools.partial(_mixer_out_kernel, alpha=alpha),
        grid=(bsz, seq // tile),
        in_specs=[
            pl.BlockSpec((1, tile, d), lambda bi, ti: (bi, ti, 0)),
            pl.BlockSpec((1, tile, MIX_WIDTH), lambda bi, ti: (bi, ti, 0)),
            pl.BlockSpec((1, tile, MEM_WIDTH), lambda bi, ti: (bi, ti, 0)),
            pl.BlockSpec(w_o.shape, full),
            pl.BlockSpec((1, d), full),
            pl.BlockSpec((1, d), full),
        ],
        out_specs=pl.BlockSpec((1, tile, d), lambda bi, ti: (bi, ti, 0)),
        out_shape=jax.ShapeDtypeStruct(x.shape, F32),
        compiler_params=_params(("arbitrary", "arbitrary")),
        name="mixer_out",
    )(x, main, mem_out, w_o, g, b)


def _swiglu_chunk(xb, wg, wu, wd):
    gp = _dot(xb, wg)
    up = _dot(xb, wu)
    return _dot((gp * jax.nn.sigmoid(gp) * up).astype(BF16), wd)


def _ffn_chunks(d_ff):
    chunk = FF_CHUNK if d_ff % FF_CHUNK == 0 else d_ff
    return chunk, d_ff // chunk


def _ffn_kernel(x_ref, wg_ref, wu_ref, wd_ref, g_ref, b_ref, o_ref, acc_ref, xb_ref, *, alpha):
    c = pl.program_id(1)

    @pl.when(c == 0)
    def _():
        acc_ref[...] = jnp.zeros_like(acc_ref)
        xb_ref[...] = x_ref[...].astype(BF16)

    acc_ref[...] += _swiglu_chunk(xb_ref[...], wg_ref[...], wu_ref[...], wd_ref[...])

    @pl.when(c == pl.num_programs(1) - 1)
    def _():
        o_ref[...] = _layer_norm(alpha * x_ref[...] + acc_ref[...], g_ref[...], b_ref[...])


def _ffn(x, w_gu, w_down, layer, g, b, alpha):
    n, d = x.shape
    d_ff = w_down.shape[1]
    tile = min(FFN_TILE, n)
    chunk, n_chunks = _ffn_chunks(d_ff)
    return pl.pallas_call(
        functools.partial(_ffn_kernel, alpha=alpha),
        grid=(n // tile, n_chunks),
        in_specs=[
            pl.BlockSpec((tile, d), lambda i, c: (i, 0)),
            pl.BlockSpec((None, d, chunk), lambda i, c: (layer, 0, c)),
            pl.BlockSpec((None, d, chunk), lambda i, c: (layer, 0, c + n_chunks)),
            pl.BlockSpec((None, chunk, d), lambda i, c: (layer, c, 0)),
            pl.BlockSpec((1, d), lambda i, c: (0, 0)),
            pl.BlockSpec((1, d), lambda i, c: (0, 0)),
        ],
        out_specs=pl.BlockSpec((tile, d), lambda i, c: (i, 0)),
        out_shape=jax.ShapeDtypeStruct(x.shape, F32),
        scratch_shapes=[pltpu.VMEM((tile, d), F32), pltpu.VMEM((tile, d), BF16)],
        compiler_params=_params(("arbitrary", "arbitrary")),
        name="ffn_dense",
    )(x, w_gu, w_gu, w_down, g, b)


META_EXPERT, META_RANK, META_GATE = 0, 2, 4


def _split_bf16(a):
    hi = a.astype(BF16)
    return hi, (a - hi.astype(F32)).astype(BF16)


def _route_kernel(x_ref, wr_ref, meta_ref, counts_ref, *, n_experts):
    tile = x_ref.shape[0]

    @pl.when(pl.program_id(0) == 0)
    def _():
        counts_ref[...] = jnp.zeros_like(counts_ref)

    lane = lax.broadcasted_iota(jnp.int32, (tile, LANES), 1)
    x_hi, x_lo = _split_bf16(x_ref[...])
    w_hi, w_lo = _split_bf16(wr_ref[...])
    logits = _dot(x_hi, w_hi) + (_dot(x_hi, w_lo) + _dot(x_lo, w_hi))
    logits = jnp.where(lane < n_experts, logits, -jnp.inf)
    top1 = jnp.max(logits, axis=-1, keepdims=True)
    idx1 = jnp.min(jnp.where(logits == top1, lane, LANES), axis=-1, keepdims=True)
    rest = jnp.where(lane == idx1, -jnp.inf, logits)
    top2 = jnp.max(rest, axis=-1, keepdims=True)
    idx2 = jnp.min(jnp.where(rest == top2, lane, LANES), axis=-1, keepdims=True)
    e2 = jnp.exp(top2 - top1)

    chosen = (lane == idx1) | (lane == idx2)
    earlier = (lax.broadcasted_iota(jnp.int32, (tile, tile), 1)
               < lax.broadcasted_iota(jnp.int32, (tile, tile), 0))
    rank = counts_ref[...] + _dot(earlier.astype(BF16), chosen.astype(BF16))
    rank1 = jnp.sum(jnp.where(lane == idx1, rank, 0.0), axis=-1, keepdims=True)
    rank2 = jnp.sum(jnp.where(lane == idx2, rank, 0.0), axis=-1, keepdims=True)
    counts_ref[...] += jnp.sum(chosen.astype(F32), axis=0, keepdims=True)

    fields = [idx1.astype(F32), idx2.astype(F32), rank1, rank2, 1.0 / (1.0 + e2), e2 / (1.0 + e2)]
    meta = jnp.zeros((tile, LANES), F32)
    for i, field in enumerate(fields):
        meta = jnp.where(lane == i, field, meta)
    meta_ref[...] = meta


def _route(x, w_router):
    n, d = x.shape
    tile = min(ROUTE_TILE, n)
    return pl.pallas_call(
        functools.partial(_route_kernel, n_experts=w_router.shape[1]),
        grid=(n // tile,),
        in_specs=[pl.BlockSpec((tile, d), lambda i: (i, 0)),
                  pl.BlockSpec((d, LANES), lambda i: (0, 0))],
        out_specs=[pl.BlockSpec((tile, LANES), lambda i: (i, 0)),
                   pl.BlockSpec((1, LANES), lambda i: (0, 0))],
        out_shape=[jax.ShapeDtypeStruct((n, LANES), F32), jax.ShapeDtypeStruct((1, LANES), F32)],
        compiler_params=_params(("arbitrary",)),
        name="moe_route",
    )(x, jnp.pad(w_router, ((0, 0), (0, LANES - w_router.shape[1]))))


def _row_copy(src_ref, src_row, dst_ref, dst_row, sem):
    return pltpu.make_async_copy(src_ref.at[pl.ds(src_row, 1), :], dst_ref.at[pl.ds(dst_row, 1), :], sem)


def _start_then_wait(copies, n):
    def start(r, carry):
        for cp in copies(r):
            cp.start()
        return carry

    def wait(r, carry):
        for cp in copies(r):
            cp.wait()
        return carry

    lax.fori_loop(0, n, start, 0, unroll=DMA_LOOP_UNROLL)
    lax.fori_loop(0, n, wait, 0, unroll=DMA_LOOP_UNROLL)


def _dispatch_kernel(pos_ref, x_ref, init_ref, xs_ref, sem):
    del init_ref

    def copies(r):
        return [_row_copy(x_ref, r, xs_ref, pos_ref[0, 0, TOP_K * r + k], sem) for k in range(TOP_K)]

    _start_then_wait(copies, x_ref.shape[0])


def _dispatch(x, pos, grouped_init):
    n, d = x.shape
    tile = min(ROUTE_TILE, n)
    any_spec = pl.BlockSpec(memory_space=pl.ANY)
    return pl.pallas_call(
        _dispatch_kernel,
        grid=(n // tile,),
        in_specs=[pl.BlockSpec((1, 1, TOP_K * tile), lambda i: (i, 0, 0), memory_space=pltpu.SMEM),
                  pl.BlockSpec((tile, d), lambda i: (i, 0)), any_spec],
        out_specs=any_spec,
        out_shape=jax.ShapeDtypeStruct(grouped_init.shape, F32),
        scratch_shapes=[pltpu.SemaphoreType.DMA],
        input_output_aliases={2: 0},
        compiler_params=_params(("arbitrary",)),
        name="moe_dispatch",
    )(pos.reshape(n // tile, 1, TOP_K * tile), x, grouped_init)


def _expert_ffn_kernel(tile_expert_ref, n_tiles_ref, xs_ref, wg_ref, wu_ref, wd_ref, o_ref,
                       acc_ref, xb_ref):
    del tile_expert_ref
    c = pl.program_id(1)

    @pl.when(pl.program_id(0) < n_tiles_ref[0])
    def _():
        @pl.when(c == 0)
        def _():
            acc_ref[...] = jnp.zeros_like(acc_ref)
            xb_ref[...] = xs_ref[...].astype(BF16)

        acc_ref[...] += _swiglu_chunk(xb_ref[...], wg_ref[0], wu_ref[0], wd_ref[0])

        @pl.when(c == pl.num_programs(1) - 1)
        def _():
            o_ref[...] = acc_ref[...]


def _expert_ffn(xs, tile_expert, n_tiles, w_gu, w_down, layer):
    n_rows, d = xs.shape
    d_ff = w_down.shape[2]
    chunk, n_chunks = _ffn_chunks(d_ff)
    row_tile = lambda t, c, te, nt: (jnp.minimum(t, nt[0] - 1), 0)
    grid_spec = pltpu.PrefetchScalarGridSpec(
        num_scalar_prefetch=2,
        grid=(n_rows // EXPERT_TILE, n_chunks),
        in_specs=[
            pl.BlockSpec((EXPERT_TILE, d), row_tile),
            pl.BlockSpec((None, 1, d, chunk), lambda t, c, te, nt: (layer, te[t], 0, c)),
            pl.BlockSpec((None, 1, d, chunk), lambda t, c, te, nt: (layer, te[t], 0, c + n_chunks)),
            pl.BlockSpec((None, 1, chunk, d), lambda t, c, te, nt: (layer, te[t], c, 0)),
        ],
        out_specs=pl.BlockSpec((EXPERT_TILE, d), row_tile),
        scratch_shapes=[pltpu.VMEM((EXPERT_TILE, d), F32), pltpu.VMEM((EXPERT_TILE, d), BF16)],
    )
    return pl.pallas_call(
        _expert_ffn_kernel,
        grid_spec=grid_spec,
        out_shape=jax.ShapeDtypeStruct((n_rows, d), F32),
        compiler_params=_params(("arbitrary", "arbitrary")),
        name="moe_experts",
    )(tile_expert, n_tiles, xs, w_gu, w_gu, w_down)


def _combine_kernel(pos_ref, meta_ref, x_ref, g_ref, b_ref, ys_ref, o_ref, rows_ref, sem, *, alpha):
    tile = x_ref.shape[0]

    def copies(r):
        return [_row_copy(ys_ref, pos_ref[0, 0, TOP_K * r + k], rows_ref.at[k], r, sem)
                for k in range(TOP_K)]

    _start_then_wait(copies, tile)

    meta = meta_ref[...]
    lane = lax.broadcasted_iota(jnp.int32, meta.shape, 1)
    ffn = jnp.zeros(x_ref.shape, F32)
    for k in range(TOP_K):
        gate = jnp.sum(jnp.where(lane == META_GATE + k, meta, 0.0), axis=-1, keepdims=True)
        ffn = ffn + gate * rows_ref[k]
    o_ref[...] = _layer_norm(alpha * x_ref[...] + ffn, g_ref[...], b_ref[...])


def _combine(x, meta, pos, ys, g, b, alpha):
    n, d = x.shape
    tile = min(COMBINE_TILE, n)
    return pl.pallas_call(
        functools.partial(_combine_kernel, alpha=alpha),
        grid=(n // tile,),
        in_specs=[pl.BlockSpec((1, 1, TOP_K * tile), lambda i: (i, 0, 0), memory_space=pltpu.SMEM),
                  pl.BlockSpec((tile, LANES), lambda i: (i, 0)),
                  pl.BlockSpec((tile, d), lambda i: (i, 0)),
                  pl.BlockSpec((1, d), lambda i: (0, 0)),
                  pl.BlockSpec((1, d), lambda i: (0, 0)),
                  pl.BlockSpec(memory_space=pl.ANY)],
        out_specs=pl.BlockSpec((tile, d), lambda i: (i, 0)),
        out_shape=jax.ShapeDtypeStruct(x.shape, F32),
        scratch_shapes=[pltpu.VMEM((TOP_K, tile, d), F32), pltpu.SemaphoreType.DMA],
        compiler_params=_params(("arbitrary",)),
        name="moe_combine",
    )(pos.reshape(n // tile, 1, TOP_K * tile), meta, x, g, b, ys)


def _moe(x, w_router, w_gu, w_down, layer, g, b, alpha, grouped_init):
    n, d = x.shape
    n_experts = w_router.shape[1]
    meta, counts = _route(x, w_router)

    counts = counts[0, :n_experts].astype(jnp.int32)
    group_tiles = (counts + EXPERT_TILE - 1) // EXPERT_TILE
    tiles_end = jnp.cumsum(group_tiles)
    group_row0 = (tiles_end - group_tiles) * EXPERT_TILE
    n_grid_tiles = (TOP_K * n) // EXPERT_TILE + n_experts
    tile_expert = jnp.minimum(
        jnp.searchsorted(tiles_end, jnp.arange(n_grid_tiles, dtype=jnp.int32), side="right"),
        n_experts - 1).astype(jnp.int32)
    expert = meta[:, META_EXPERT:META_EXPERT + TOP_K].astype(jnp.int32)
    rank = meta[:, META_RANK:META_RANK + TOP_K].astype(jnp.int32)
    pos = (group_row0[expert] + rank).reshape(-1)

    if grouped_init is None:
        grouped_init = jnp.zeros((n_grid_tiles * EXPERT_TILE, d), F32)
    xs = _dispatch(x, pos, grouped_init)
    ys = _expert_ffn(xs, tile_expert, tiles_end[-1:].astype(jnp.int32), w_gu, w_down, layer)
    return _combine(x, meta, pos, ys, g, b, alpha), xs


def _rotate_half_columns(w):
    half = ROPE_DIM // 2
    col = jnp.arange(w.shape[1])
    c = col % HEAD_DIM
    src = jnp.where(c < half, col + half, col - half)
    sign = jnp.where(c < half, -1.0, jnp.where(c < ROPE_DIM, 1.0, 0.0)).astype(w.dtype)
    return w[:, jnp.clip(src, 0, w.shape[1] - 1)] * sign


def _rope_lane_frequencies():
    inv_freq = ROPE_THETA ** (-jnp.arange(0, ROPE_DIM, 2, dtype=F32) / ROPE_DIM)
    c = jnp.arange(LANES) % HEAD_DIM
    return jnp.where(c < ROPE_DIM, inv_freq[c % (ROPE_DIM // 2)], 0.0).astype(F32)[None, :]


def kernel(x, mem, positions, w_in_a, conv_a, w_q_b, w_kv_shared, w_mem_kv, w_o, ln1_g, ln1_b,
           ln2_g, ln2_b, w_gu_dense, w_down_dense, w_router, w_gu_moe, w_down_moe):
    bsz, seq, d = x.shape
    depth = w_o.shape[0]
    n_a = w_in_a.shape[0]
    alpha = float((2 * depth) ** 0.25)
    assert seq % MOBA_BLOCK == 0 and seq // MOBA_BLOCK <= LANES

    pos = positions.astype(F32)[:, :, None]
    freq = _rope_lane_frequencies()
    mem_len = mem.shape[1]
    w_mem_all = jnp.transpose(w_mem_kv, (1, 0, 2)).reshape(d, depth * 2 * MEM_WIDTH).astype(BF16)
    mem_kv = _matmul(mem.reshape(bsz * mem_len, d), w_mem_all, BF16).reshape(bsz, mem_len, -1)

    w_gu_dense_bf16, w_down_dense_bf16 = w_gu_dense.astype(BF16), w_down_dense.astype(BF16)
    w_gu_moe_bf16, w_down_moe_bf16 = w_gu_moe.astype(BF16), w_down_moe.astype(BF16)
    shared = None
    grouped = None
    for layer in range(depth):
        g1, b1 = ln1_g[layer][None, :], ln1_b[layer][None, :]
        g2, b2 = ln2_g[layer][None, :], ln2_b[layer][None, :]
        wo = w_o[layer].astype(BF16)
        if layer < n_a:
            x = _mixer_a(x, w_in_a[layer].astype(BF16), conv_a[layer], mem_kv, layer, wo, g1, b1, alpha)
        else:
            wq = w_q_b[layer - n_a]
            q, mem_out = _query_b(x, pos, freq, wq.astype(BF16),
                                  _rotate_half_columns(wq[:, :MIX_WIDTH]).astype(BF16), mem_kv, layer)
            main = _moba(q, *shared)
            x = _mixer_out(x, main, mem_out, wo, g1, b1, alpha)
        xf = x.reshape(bsz * seq, d)
        if layer % 2 == 0:
            xf = _ffn(xf, w_gu_dense_bf16, w_down_dense_bf16, layer // 2, g2, b2, alpha)
        else:
            xf, grouped = _moe(xf, w_router[layer // 2], w_gu_moe_bf16, w_down_moe_bf16, layer // 2,
                               g2, b2, alpha, grouped)
        x = xf.reshape(bsz, seq, d)
        if layer == n_a - 1:
            wk = w_kv_shared[:, :MIX_WIDTH]
            k, v, km = _shared_kv(x, pos, freq, wk.astype(BF16), _rotate_half_columns(wk).astype(BF16),
                                  w_kv_shared[:, MIX_WIDTH:].astype(BF16))
            n_blocks = seq // MOBA_BLOCK
            km = jnp.pad(km.reshape(bsz, n_blocks, MIX_WIDTH), ((0, 0), (0, LANES - n_blocks), (0, 0)))
            shared = (k, v, km)
    return x
```

```python
import functools

import jax
import jax.numpy as jnp
from jax import lax
from jax.experimental import pallas as pl
from jax.experimental.pallas import tpu as pltpu

F32 = jnp.float32
BF16 = jnp.bfloat16

HEAD_DIM = 64
MIX_HEADS = 12
MIX_WIDTH = MIX_HEADS * HEAD_DIM
MEM_HEADS = 4
MEM_WIDTH = MEM_HEADS * HEAD_DIM
CONV_WIDTH = 3
ROPE_DIM = HEAD_DIM // 4
ROPE_THETA = 500000.0
MOBA_BLOCK = 256
MOBA_TOPK = 3
TOP_K = 2
LN_EPS = 1e-5
ATTN_SCALE = HEAD_DIM ** -0.5
LOG2_E = 1.4426950408889634

LANES = 128
HEADS_PER_LANE_GROUP = LANES // HEAD_DIM
MASK_VALUE = -1e30
VMEM_LIMIT_BYTES = 56 * 1024 * 1024

SEQ_TILE = 512
FFN_TILE = 1024
FF_CHUNK = 1408
MOBA_CHUNK_BLOCKS = 4
ROUTE_TILE = 1024
EXPERT_TILE = 512
COMBINE_TILE = 1024
DMA_LOOP_UNROLL = 8


def _params(semantics):
    return pltpu.CompilerParams(dimension_semantics=semantics, vmem_limit_bytes=VMEM_LIMIT_BYTES)


def _layer_norm(r, g, b):
    mu = jnp.mean(r, axis=-1, keepdims=True)
    c = r - mu
    var = jnp.mean(c * c, axis=-1, keepdims=True)
    return c * lax.rsqrt(var + LN_EPS) * g + b


def _dot(a, b):
    return jnp.dot(a, b, preferred_element_type=F32)


def _dot_nt(a, b):
    return lax.dot_general(a, b, (((1,), (1,)), ((), ())), preferred_element_type=F32)


def _head_lane_mask(width, head):
    lane = lax.broadcasted_iota(jnp.int32, (1, width), 1)
    return (lane >= head * HEAD_DIM) & (lane < (head + 1) * HEAD_DIM)


def _mem_attention(q_mem, mem_k, mem_v):
    out = jnp.zeros(q_mem.shape, F32)
    for head in range(MEM_HEADS):
        hm = _head_lane_mask(MEM_WIDTH, head)
        qh = jnp.where(hm, q_mem * ATTN_SCALE, 0.0).astype(BF16)
        s = _dot_nt(qh, mem_k)
        m = jnp.max(s, axis=-1, keepdims=True)
        p = jnp.exp(s - m)
        l = jnp.sum(p, axis=-1, keepdims=True)
        o = _dot(p.astype(BF16), mem_v) / l
        out = jnp.where(hm, o, out)
    return out


def _rope_tables(pos, freq, reps):
    angle = pos * freq
    cos = jnp.cos(angle)
    sin = jnp.sin(angle)
    return (jnp.concatenate([cos] * reps, axis=1), jnp.concatenate([sin] * reps, axis=1))


def _matmul_kernel(x_ref, w_ref, o_ref):
    o_ref[...] = _dot(x_ref[...].astype(BF16), w_ref[...]).astype(o_ref.dtype)


def _matmul(x, w, out_dtype, tile_m=256):
    m, k = x.shape
    n = w.shape[1]
    return pl.pallas_call(
        _matmul_kernel,
        grid=(m // tile_m,),
        in_specs=[pl.BlockSpec((tile_m, k), lambda i: (i, 0)),
                  pl.BlockSpec((k, n), lambda i: (0, 0))],
        out_specs=pl.BlockSpec((tile_m, n), lambda i: (i, 0)),
        out_shape=jax.ShapeDtypeStruct((m, n), out_dtype),
        compiler_params=_params(("arbitrary",)),
        name="mem_kv_proj",
    )(x, w)


def _mixer_a_kernel(x_ref, win_ref, conv_ref, mk_ref, mv_ref, wo_ref, g_ref, b_ref, o_ref,
                    ztail_ref, *, alpha):
    tile = x_ref.shape[1]

    @pl.when(pl.program_id(1) == 0)
    def _():
        ztail_ref[...] = jnp.zeros_like(ztail_ref)

    x = x_ref[0]
    proj = _dot(x.astype(BF16), win_ref[...])
    u_b = proj[:, :MIX_WIDTH]
    u_c = proj[:, MIX_WIDTH:2 * MIX_WIDTH]
    u_h = proj[:, 2 * MIX_WIDTH:3 * MIX_WIDTH]
    q_mem = proj[:, 3 * MIX_WIDTH:]

    z = u_c * u_h
    tail = ztail_ref[...]
    ztail_ref[...] = z[tile - 8:, :]
    row = lax.broadcasted_iota(jnp.int32, z.shape, 0)
    z1 = jnp.where(row == 0, tail[7:8, :], pltpu.roll(z, 1, 0))
    z2 = jnp.where(row == 0, tail[6:7, :], jnp.where(row == 1, tail[7:8, :], pltpu.roll(z, 2, 0)))
    w = conv_ref[...]
    main = u_b * (w[0:1, :] * z2 + w[1:2, :] * z1 + w[2:3, :] * z)

    mem_out = _mem_attention(q_mem, mk_ref[0], mv_ref[0])
    mix = (_dot(main.astype(BF16), wo_ref[:MIX_WIDTH, :])
           + _dot(mem_out.astype(BF16), wo_ref[MIX_WIDTH:, :]))
    o_ref[0] = _layer_norm(alpha * x + mix, g_ref[...], b_ref[...])


def _mixer_a(x, w_in, conv_w, mem_kv, layer, w_o, g, b, alpha):
    bsz, seq, d = x.shape
    tile = min(SEQ_TILE, seq)
    mem_len = mem_kv.shape[1]
    return pl.pallas_call(
        functools.partial(_mixer_a_kernel, alpha=alpha),
        grid=(bsz, seq // tile),
        in_specs=[
            pl.BlockSpec((1, tile, d), lambda bi, ti: (bi, ti, 0)),
            pl.BlockSpec(w_in.shape, lambda bi, ti: (0, 0)),
            pl.BlockSpec(conv_w.shape, lambda bi, ti: (0, 0)),
            pl.BlockSpec((1, mem_len, MEM_WIDTH), lambda bi, ti: (bi, 0, 2 * layer)),
            pl.BlockSpec((1, mem_len, MEM_WIDTH), lambda bi, ti: (bi, 0, 2 * layer + 1)),
            pl.BlockSpec(w_o.shape, lambda bi, ti: (0, 0)),
            pl.BlockSpec((1, d), lambda bi, ti: (0, 0)),
            pl.BlockSpec((1, d), lambda bi, ti: (0, 0)),
        ],
        out_specs=pl.BlockSpec((1, tile, d), lambda bi, ti: (bi, ti, 0)),
        out_shape=jax.ShapeDtypeStruct(x.shape, F32),
        scratch_shapes=[pltpu.VMEM((8, MIX_WIDTH), F32)],
        compiler_params=_params(("arbitrary", "arbitrary")),
        name="mixer_a",
    )(x, w_in, conv_w, mem_kv, mem_kv, w_o, g, b)


def _shared_kv_kernel(x_ref, pos_ref, freq_ref, wk_ref, wkr_ref, wv_ref, k_ref, v_ref, km_ref):
    xb = x_ref[0].astype(BF16)
    cos, sin = _rope_tables(pos_ref[0], freq_ref[...], MIX_WIDTH // LANES)
    k = _dot(xb, wk_ref[...]) * cos + _dot(xb, wkr_ref[...]) * sin
    k_ref[0] = k.astype(BF16)
    v_ref[0] = _dot(xb, wv_ref[...]).astype(BF16)
    km_ref[0, 0] = jnp.mean(k, axis=0, keepdims=True)


def _shared_kv(x, pos, freq, w_k, w_kr, w_v):
    bsz, seq, d = x.shape
    n_blocks = seq // MOBA_BLOCK
    full = lambda bi, ti: (0, 0)
    return pl.pallas_call(
        _shared_kv_kernel,
        grid=(bsz, n_blocks),
        in_specs=[
            pl.BlockSpec((1, MOBA_BLOCK, d), lambda bi, ti: (bi, ti, 0)),
            pl.BlockSpec((1, MOBA_BLOCK, 1), lambda bi, ti: (bi, ti, 0)),
            pl.BlockSpec(freq.shape, full),
            pl.BlockSpec(w_k.shape, full),
            pl.BlockSpec(w_kr.shape, full),
            pl.BlockSpec(w_v.shape, full),
        ],
        out_specs=[
            pl.BlockSpec((1, MOBA_BLOCK, MIX_WIDTH), lambda bi, ti: (bi, ti, 0)),
            pl.BlockSpec((1, MOBA_BLOCK, MIX_WIDTH), lambda bi, ti: (bi, ti, 0)),
            pl.BlockSpec((1, 1, 1, MIX_WIDTH), lambda bi, ti: (bi, ti, 0, 0)),
        ],
        out_shape=[
            jax.ShapeDtypeStruct((bsz, seq, MIX_WIDTH), BF16),
            jax.ShapeDtypeStruct((bsz, seq, MIX_WIDTH), BF16),
            jax.ShapeDtypeStruct((bsz, n_blocks, 1, MIX_WIDTH), F32),
        ],
        compiler_params=_params(("arbitrary", "arbitrary")),
        name="shared_kv",
    )(x, pos, freq, w_k, w_kr, w_v)


def _query_b_kernel(x_ref, pos_ref, freq_ref, wq_ref, wqr_ref, mk_ref, mv_ref, q_ref, mo_ref):
    xb = x_ref[0].astype(BF16)
    proj = _dot(xb, wq_ref[...])
    cos, sin = _rope_tables(pos_ref[0], freq_ref[...], MIX_WIDTH // LANES)
    q = proj[:, :MIX_WIDTH] * cos + _dot(xb, wqr_ref[...]) * sin
    q_ref[0] = (q * (ATTN_SCALE * LOG2_E)).astype(BF16)
    mo_ref[0] = _mem_attention(proj[:, MIX_WIDTH:], mk_ref[0], mv_ref[0]).astype(BF16)


def _query_b(x, pos, freq, w_q, w_qr, mem_kv, layer):
    bsz, seq, d = x.shape
    tile = min(SEQ_TILE, seq)
    mem_len = mem_kv.shape[1]
    full = lambda bi, ti: (0, 0)
    return pl.pallas_call(
        _query_b_kernel,
        grid=(bsz, seq // tile),
        in_specs=[
            pl.BlockSpec((1, tile, d), lambda bi, ti: (bi, ti, 0)),
            pl.BlockSpec((1, tile, 1), lambda bi, ti: (bi, ti, 0)),
            pl.BlockSpec(freq.shape, full),
            pl.BlockSpec(w_q.shape, full),
            pl.BlockSpec(w_qr.shape, full),
            pl.BlockSpec((1, mem_len, MEM_WIDTH), lambda bi, ti: (bi, 0, 2 * layer)),
            pl.BlockSpec((1, mem_len, MEM_WIDTH), lambda bi, ti: (bi, 0, 2 * layer + 1)),
        ],
        out_specs=[
            pl.BlockSpec((1, tile, MIX_WIDTH), lambda bi, ti: (bi, ti, 0)),
            pl.BlockSpec((1, tile, MEM_WIDTH), lambda bi, ti: (bi, ti, 0)),
        ],
        out_shape=[
            jax.ShapeDtypeStruct((bsz, seq, MIX_WIDTH), BF16),
            jax.ShapeDtypeStruct((bsz, seq, MEM_WIDTH), BF16),
        ],
        compiler_params=_params(("arbitrary", "arbitrary")),
        name="query_b",
    )(x, pos, freq, w_q, w_qr, mem_kv, mem_kv)


def _moba_kernel(q_ref, k_ref, v_ref, km_ref, o_ref, kaug_ref, vaug_ref, m_ref, acc_ref, s0_ref,
                 s1_ref, *, chunk_blocks):
    blk = MOBA_BLOCK
    span = chunk_blocks * blk
    own_chunk = pl.program_id(2)
    chains = [(u, h) for u in range(chunk_blocks) for h in range(HEADS_PER_LANE_GROUP)]
    rows = len(chains) * blk

    @pl.when(own_chunk == 0)
    def _():
        kaug_ref[:, :LANES] = k_ref[0]
        vaug_ref[:, :LANES] = v_ref[0]
        lane = lax.broadcasted_iota(jnp.int32, (blk, LANES), 1)
        ones_lane = (lane == 0).astype(BF16)

        def fill(j, carry):
            off = pl.multiple_of(j * blk, blk)
            kaug_ref[pl.ds(off, blk), LANES:] = (lane == j).astype(BF16)
            vaug_ref[pl.ds(off, blk), LANES:] = ones_lane
            return carry

        lax.fori_loop(0, k_ref.shape[1] // blk, fill, 0)

    q = q_ref[0]
    q_heads = jnp.concatenate(
        [jnp.where(_head_lane_mask(LANES, h), q[u * blk:(u + 1) * blk], 0).astype(BF16)
         for u, h in chains], axis=0)

    block_id = lax.broadcasted_iota(jnp.int32, (rows, LANES), 1)
    own = (own_chunk * chunk_blocks
           + lax.broadcasted_iota(jnp.int32, (rows, LANES), 0) // (HEADS_PER_LANE_GROUP * blk))
    is_past = block_id < own
    gate = jnp.where(is_past, _dot_nt(q_heads, km_ref[0].astype(BF16)), -jnp.inf)
    kth = gate
    for _ in range(MOBA_TOPK - 1):
        kth = jnp.where(kth == jnp.max(kth, axis=-1, keepdims=True), -jnp.inf, kth)
    threshold = jnp.max(kth, axis=-1, keepdims=True)
    allowed = (is_past & (gate >= threshold)) | (block_id == own)
    block_bias = jnp.where(allowed, 0.0, MASK_VALUE).astype(BF16)
    q_aug = jnp.concatenate([q_heads, block_bias], axis=1)

    m_ref[...] = jnp.full(m_ref.shape, MASK_VALUE, F32)
    acc_ref[...] = jnp.zeros_like(acc_ref)
    chain_rows = [slice(i * blk, (i + 1) * blk) for i in range(len(chains))]

    def chunk_rows(c):
        return pl.ds(pl.multiple_of(c * span, span), span)

    def score_own(s_ref):
        for r, (u, _) in zip(chain_rows, chains):
            width = (u + 1) * blk
            k_part = kaug_ref[pl.ds(pl.multiple_of(own_chunk * span, span), width), :]
            k_pos = lax.broadcasted_iota(jnp.int32, (blk, width), 1)
            q_pos = u * blk + lax.broadcasted_iota(jnp.int32, (blk, width), 0)
            s_ref[r, :width] = jnp.where(k_pos <= q_pos, _dot_nt(q_aug[r], k_part), MASK_VALUE)
            if width < span:
                s_ref[r, width:] = jnp.full((blk, span - width), MASK_VALUE, F32)

    def score_past(c, s_ref):
        k_chunk = kaug_ref[chunk_rows(c), :]
        for r in chain_rows:
            s_ref[r, :] = _dot_nt(q_aug[r], k_chunk)

    def softmax_value(c, s_ref):
        v_chunk = vaug_ref[chunk_rows(c), :]
        for r in chain_rows:
            s = s_ref[r, :]
            m_old = m_ref[r, :]
            m_new = jnp.maximum(m_old, jnp.max(s, axis=-1, keepdims=True))
            p = jnp.exp2((s - m_new).astype(BF16))
            acc_ref[r, :] = jnp.exp2(m_old - m_new) * acc_ref[r, :] + _dot(p, v_chunk)
            m_ref[r, :] = m_new

    def step_chunk(j):
        return jnp.where(j == 0, own_chunk, j - 1)

    n_past = own_chunk
    score_own(s0_ref)

    def two_steps(i, carry):
        j = 2 * i + 1
        score_past(j - 1, s1_ref)
        softmax_value(step_chunk(j - 1), s0_ref)
        score_past(j, s0_ref)
        softmax_value(j - 1, s1_ref)
        return carry

    lax.fori_loop(0, n_past // 2, two_steps, 0)

    @pl.when(n_past % 2 == 1)
    def _():
        score_past(n_past - 1, s1_ref)
        softmax_value(step_chunk(n_past - 1), s0_ref)
        softmax_value(n_past - 1, s1_ref)

    @pl.when(n_past % 2 == 0)
    def _():
        softmax_value(step_chunk(n_past), s0_ref)

    acc = acc_ref[...]
    o = acc[:, :LANES] / acc[:, LANES:LANES + 1]
    for u in range(chunk_blocks):
        base = u * HEADS_PER_LANE_GROUP * blk
        out = o[base:base + blk]
        for h in range(1, HEADS_PER_LANE_GROUP):
            out = jnp.where(_head_lane_mask(LANES, h), o[base + h * blk:base + (h + 1) * blk], out)
        o_ref[0, u * blk:(u + 1) * blk, :] = out.astype(o_ref.dtype)


def _moba(q, k, v, k_means):
    bsz, seq, _ = q.shape
    n_blocks = seq // MOBA_BLOCK
    groups = MIX_WIDTH // LANES
    chunk_blocks = MOBA_CHUNK_BLOCKS if n_blocks % MOBA_CHUNK_BLOCKS == 0 else 1
    span = chunk_blocks * MOBA_BLOCK
    rows = HEADS_PER_LANE_GROUP * span
    return pl.pallas_call(
        functools.partial(_moba_kernel, chunk_blocks=chunk_blocks),
        grid=(bsz, groups, seq // span),
        in_specs=[
            pl.BlockSpec((1, span, LANES), lambda bi, gi, qi: (bi, qi, gi)),
            pl.BlockSpec((1, seq, LANES), lambda bi, gi, qi: (bi, 0, gi)),
            pl.BlockSpec((1, seq, LANES), lambda bi, gi, qi: (bi, 0, gi)),
            pl.BlockSpec((1, LANES, LANES), lambda bi, gi, qi: (bi, 0, gi)),
        ],
        out_specs=pl.BlockSpec((1, span, LANES), lambda bi, gi, qi: (bi, qi, gi)),
        out_shape=jax.ShapeDtypeStruct(q.shape, BF16),
        scratch_shapes=[pltpu.VMEM((seq, 2 * LANES), BF16), pltpu.VMEM((seq, 2 * LANES), BF16),
                        pltpu.VMEM((rows, 1), F32), pltpu.VMEM((rows, 2 * LANES), F32),
                        pltpu.VMEM((rows, span), F32), pltpu.VMEM((rows, span), F32)],
        compiler_params=_params(("arbitrary", "arbitrary", "arbitrary")),
        name="moba",
    )(q, k, v, k_means)


def _mixer_out_kernel(x_ref, main_ref, mo_ref, wo_ref, g_ref, b_ref, o_ref, *, alpha):
    mix = _dot(main_ref[0], wo_ref[:MIX_WIDTH, :]) + _dot(mo_ref[0], wo_ref[MIX_WIDTH:, :])
    o_ref[0] = _layer_norm(alpha * x_ref[0] + mix, g_ref[...], b_ref[...])


def _mixer_out(x, main, mem_out, w_o, g, b, alpha):
    bsz, seq, d = x.shape
    tile = min(SEQ_TILE, seq)
    full = lambda bi, ti: (0, 0)
    return pl.pallas_call(
        functools.partial(_mixer_out_kernel, alpha=alpha),
        grid=(bsz, seq // tile),
        in_specs=[
            pl.BlockSpec((1, tile, d), lambda bi, ti: (bi, ti, 0)),
            pl.BlockSpec((1, tile, MIX_WIDTH), lambda bi, ti: (bi, ti, 0)),
            pl.BlockSpec((1, tile, MEM_WIDTH), lambda bi, ti: (bi, ti, 0)),
            pl.BlockSpec(w_o.shape, full),
            pl.BlockSpec((1, d), full),
            pl.BlockSpec((1, d), full),
        ],
        out_specs=pl.BlockSpec((1, tile, d), lambda bi, ti: (bi, ti, 0)),
        out_shape=jax.ShapeDtypeStruct(x.shape, F32),
        compiler_params=_params(("arbitrary", "arbitrary")),
        name="mixer_out",
    )(x, main, mem_out, w_o, g, b)


def _swiglu_chunk(xb, wg, wu, wd):
    gp = _dot(xb, wg)
    up = _dot(xb, wu)
    return _dot((gp * jax.nn.sigmoid(gp) * up).astype(BF16), wd)


def _ffn_chunks(d_ff):
    chunk = FF_CHUNK if d_ff % FF_CHUNK == 0 else d_ff
    return chunk, d_ff // chunk


def _accumulate_chunks(c, n_chunks, part, acc_ref, finish):
    if n_chunks == 1:
        finish(part())
        return

    @pl.when(c == 0)
    def _():
        acc_ref[...] = part()

    if n_chunks > 2:
        @pl.when((c > 0) & (c < n_chunks - 1))
        def _():
            acc_ref[...] += part()

    @pl.when(c == n_chunks - 1)
    def _():
        finish(acc_ref[...] + part())


def _ffn_kernel(x_ref, wg_ref, wu_ref, wd_ref, g_ref, b_ref, o_ref, acc_ref, xb_ref, *, alpha,
                n_chunks):
    c = pl.program_id(1)

    @pl.when(c == 0)
    def _():
        xb_ref[...] = x_ref[...].astype(BF16)

    def finish(ffn):
        o_ref[...] = _layer_norm(alpha * x_ref[...] + ffn, g_ref[...], b_ref[...])

    def part():
        return _swiglu_chunk(xb_ref[...], wg_ref[...], wu_ref[...], wd_ref[...])

    _accumulate_chunks(c, n_chunks, part, acc_ref, finish)


def _ffn(x, w_gu, w_down, layer, g, b, alpha):
    n, d = x.shape
    d_ff = w_down.shape[1]
    tile = min(FFN_TILE, n)
    chunk, n_chunks = _ffn_chunks(d_ff)
    return pl.pallas_call(
        functools.partial(_ffn_kernel, alpha=alpha, n_chunks=n_chunks),
        grid=(n // tile, n_chunks),
        in_specs=[
            pl.BlockSpec((tile, d), lambda i, c: (i, 0)),
            pl.BlockSpec((None, d, chunk), lambda i, c: (layer, 0, c)),
            pl.BlockSpec((None, d, chunk), lambda i, c: (layer, 0, c + n_chunks)),
            pl.BlockSpec((None, chunk, d), lambda i, c: (layer, c, 0)),
            pl.BlockSpec((1, d), lambda i, c: (0, 0)),
            pl.BlockSpec((1, d), lambda i, c: (0, 0)),
        ],
        out_specs=pl.BlockSpec((tile, d), lambda i, c: (i, 0)),
        out_shape=jax.ShapeDtypeStruct(x.shape, F32),
        scratch_shapes=[pltpu.VMEM((tile, d), F32), pltpu.VMEM((tile, d), BF16)],
        compiler_params=_params(("arbitrary", "arbitrary")),
        name="ffn_dense",
    )(x, w_gu, w_gu, w_down, g, b)


META_EXPERT, META_RANK, META_GATE = 0, 2, 4


def _split_bf16(a):
    hi = a.astype(BF16)
    return hi, (a - hi.astype(F32)).astype(BF16)


def _route_kernel(x_ref, wr_ref, meta_ref, counts_ref, *, n_experts):
    tile = x_ref.shape[0]

    @pl.when(pl.program_id(0) == 0)
    def _():
        counts_ref[...] = jnp.zeros_like(counts_ref)

    lane = lax.broadcasted_iota(jnp.int32, (tile, LANES), 1)
    x_hi, x_lo = _split_bf16(x_ref[...])
    w_hi, w_lo = _split_bf16(wr_ref[...])
    logits = _dot(x_hi, w_hi) + (_dot(x_hi, w_lo) + _dot(x_lo, w_hi))
    logits = jnp.where(lane < n_experts, logits, -jnp.inf)
    top1 = jnp.max(logits, axis=-1, keepdims=True)
    idx1 = jnp.min(jnp.where(logits == top1, lane, LANES), axis=-1, keepdims=True)
    rest = jnp.where(lane == idx1, -jnp.inf, logits)
    top2 = jnp.max(rest, axis=-1, keepdims=True)
    idx2 = jnp.min(jnp.where(rest == top2, lane, LANES), axis=-1, keepdims=True)
    e2 = jnp.exp(top2 - top1)

    chosen = (lane == idx1) | (lane == idx2)
    earlier = (lax.broadcasted_iota(jnp.int32, (tile, tile), 1)
               < lax.broadcasted_iota(jnp.int32, (tile, tile), 0))
    rank = counts_ref[...] + _dot(earlier.astype(BF16), chosen.astype(BF16))
    rank1 = jnp.sum(jnp.where(lane == idx1, rank, 0.0), axis=-1, keepdims=True)
    rank2 = jnp.sum(jnp.where(lane == idx2, rank, 0.0), axis=-1, keepdims=True)
    counts_ref[...] += jnp.sum(chosen.astype(F32), axis=0, keepdims=True)

    fields = [idx1.astype(F32), idx2.astype(F32), rank1, rank2, 1.0 / (1.0 + e2), e2 / (1.0 + e2)]
    meta = jnp.zeros((tile, LANES), F32)
    for i, field in enumerate(fields):
        meta = jnp.where(lane == i, field, meta)
    meta_ref[...] = meta


def _route(x, w_router):
    n, d = x.shape
    tile = min(ROUTE_TILE, n)
    return pl.pallas_call(
        functools.partial(_route_kernel, n_experts=w_router.shape[1]),
        grid=(n // tile,),
        in_specs=[pl.BlockSpec((tile, d), lambda i: (i, 0)),
                  pl.BlockSpec((d, LANES), lambda i: (0, 0))],
        out_specs=[pl.BlockSpec((tile, LANES), lambda i: (i, 0)),
                   pl.BlockSpec((1, LANES), lambda i: (0, 0))],
        out_shape=[jax.ShapeDtypeStruct((n, LANES), F32), jax.ShapeDtypeStruct((1, LANES), F32)],
        compiler_params=_params(("arbitrary",)),
        name="moe_route",
    )(x, jnp.pad(w_router, ((0, 0), (0, LANES - w_router.shape[1]))))


def _row_copy(src_ref, src_row, dst_ref, dst_row, sem):
    return pltpu.make_async_copy(src_ref.at[pl.ds(src_row, 1), :], dst_ref.at[pl.ds(dst_row, 1), :], sem)


def _start_then_wait(copies, n):
    def start(r, carry):
        for cp in copies(r):
            cp.start()
        return carry

    def wait(r, carry):
        for cp in copies(r):
            cp.wait()
        return carry

    lax.fori_loop(0, n, start, 0, unroll=DMA_LOOP_UNROLL)
    lax.fori_loop(0, n, wait, 0, unroll=DMA_LOOP_UNROLL)


def _dispatch_kernel(pos_ref, x_ref, init_ref, xs_ref, sem):
    del init_ref

    def copies(r):
        return [_row_copy(x_ref, r, xs_ref, pos_ref[0, 0, TOP_K * r + k], sem) for k in range(TOP_K)]

    _start_then_wait(copies, x_ref.shape[0])


def _dispatch(x, pos, grouped_init):
    n, d = x.shape
    tile = min(ROUTE_TILE, n)
    any_spec = pl.BlockSpec(memory_space=pl.ANY)
    return pl.pallas_call(
        _dispatch_kernel,
        grid=(n // tile,),
        in_specs=[pl.BlockSpec((1, 1, TOP_K * tile), lambda i: (i, 0, 0), memory_space=pltpu.SMEM),
                  pl.BlockSpec((tile, d), lambda i: (i, 0)), any_spec],
        out_specs=any_spec,
        out_shape=jax.ShapeDtypeStruct(grouped_init.shape, F32),
        scratch_shapes=[pltpu.SemaphoreType.DMA],
        input_output_aliases={2: 0},
        compiler_params=_params(("arbitrary",)),
        name="moe_dispatch",
    )(pos.reshape(n // tile, 1, TOP_K * tile), x, grouped_init)


def _expert_ffn_kernel(tile_expert_ref, n_tiles_ref, xs_ref, wg_ref, wu_ref, wd_ref, o_ref,
                       acc_ref, xb_ref, *, n_chunks):
    del tile_expert_ref
    c = pl.program_id(1)

    @pl.when(pl.program_id(0) < n_tiles_ref[0])
    def _():
        @pl.when(c == 0)
        def _():
            xb_ref[...] = xs_ref[...].astype(BF16)

        def finish(ffn):
            o_ref[...] = ffn

        def part():
            return _swiglu_chunk(xb_ref[...], wg_ref[0], wu_ref[0], wd_ref[0])

        _accumulate_chunks(c, n_chunks, part, acc_ref, finish)


def _expert_ffn(xs, tile_expert, n_tiles, w_gu, w_down, layer):
    n_rows, d = xs.shape
    d_ff = w_down.shape[2]
    chunk, n_chunks = _ffn_chunks(d_ff)
    row_tile = lambda t, c, te, nt: (jnp.minimum(t, nt[0] - 1), 0)
    grid_spec = pltpu.PrefetchScalarGridSpec(
        num_scalar_prefetch=2,
        grid=(n_rows // EXPERT_TILE, n_chunks),
        in_specs=[
            pl.BlockSpec((EXPERT_TILE, d), row_tile),
            pl.BlockSpec((None, 1, d, chunk), lambda t, c, te, nt: (layer, te[t], 0, c)),
            pl.BlockSpec((None, 1, d, chunk), lambda t, c, te, nt: (layer, te[t], 0, c + n_chunks)),
            pl.BlockSpec((None, 1, chunk, d), lambda t, c, te, nt: (layer, te[t], c, 0)),
        ],
        out_specs=pl.BlockSpec((EXPERT_TILE, d), row_tile),
        scratch_shapes=[pltpu.VMEM((EXPERT_TILE, d), F32), pltpu.VMEM((EXPERT_TILE, d), BF16)],
    )
    return pl.pallas_call(
        functools.partial(_expert_ffn_kernel, n_chunks=n_chunks),
        grid_spec=grid_spec,
        out_shape=jax.ShapeDtypeStruct((n_rows, d), F32),
        compiler_params=_params(("arbitrary", "arbitrary")),
        name="moe_experts",
    )(tile_expert, n_tiles, xs, w_gu, w_gu, w_down)


def _combine_kernel(pos_ref, meta_ref, x_ref, g_ref, b_ref, ys_ref, o_ref, rows_ref, sem, *, alpha):
    tile = x_ref.shape[0]

    def copies(r):
        return [_row_copy(ys_ref, pos_ref[0, 0, TOP_K * r + k], rows_ref.at[k], r, sem)
                for k in range(TOP_K)]

    _start_then_wait(copies, tile)

    meta = meta_ref[...]
    lane = lax.broadcasted_iota(jnp.int32, meta.shape, 1)
    ffn = jnp.zeros(x_ref.shape, F32)
    for k in range(TOP_K):
        gate = jnp.sum(jnp.where(lane == META_GATE + k, meta, 0.0), axis=-1, keepdims=True)
        ffn = ffn + gate * rows_ref[k]
    o_ref[...] = _layer_norm(alpha * x_ref[...] + ffn, g_ref[...], b_ref[...])


def _combine(x, meta, pos, ys, g, b, alpha):
    n, d = x.shape
    tile = min(COMBINE_TILE, n)
    return pl.pallas_call(
        functools.partial(_combine_kernel, alpha=alpha),
        grid=(n // tile,),
        in_specs=[pl.BlockSpec((1, 1, TOP_K * tile), lambda i: (i, 0, 0), memory_space=pltpu.SMEM),
                  pl.BlockSpec((tile, LANES), lambda i: (i, 0)),
                  pl.BlockSpec((tile, d), lambda i: (i, 0)),
                  pl.BlockSpec((1, d), lambda i: (0, 0)),
                  pl.BlockSpec((1, d), lambda i: (0, 0)),
                  pl.BlockSpec(memory_space=pl.ANY)],
        out_specs=pl.BlockSpec((tile, d), lambda i: (i, 0)),
        out_shape=jax.ShapeDtypeStruct(x.shape, F32),
        scratch_shapes=[pltpu.VMEM((TOP_K, tile, d), F32), pltpu.SemaphoreType.DMA],
        compiler_params=_params(("arbitrary",)),
        name="moe_combine",
    )(pos.reshape(n // tile, 1, TOP_K * tile), meta, x, g, b, ys)


def _moe(x, w_router, w_gu, w_down, layer, g, b, alpha, grouped_init):
    n, d = x.shape
    n_experts = w_router.shape[1]
    meta, counts = _route(x, w_router)

    counts = counts[0, :n_experts].astype(jnp.int32)
    group_tiles = (counts + EXPERT_TILE - 1) // EXPERT_TILE
    tiles_end = jnp.cumsum(group_tiles)
    group_row0 = (tiles_end - group_tiles) * EXPERT_TILE
    n_grid_tiles = (TOP_K * n) // EXPERT_TILE + n_experts
    tile_ids = jnp.arange(n_grid_tiles, dtype=jnp.int32)
    tile_expert = jnp.minimum(jnp.sum((tile_ids[:, None] >= tiles_end[None, :]).astype(jnp.int32), axis=1),
                              n_experts - 1)
    expert = meta[:, META_EXPERT:META_EXPERT + TOP_K].astype(jnp.int32)
    rank = meta[:, META_RANK:META_RANK + TOP_K].astype(jnp.int32)
    pos = (group_row0[expert] + rank).reshape(-1)

    if grouped_init is None:
        grouped_init = jnp.zeros((n_grid_tiles * EXPERT_TILE, d), F32)
    xs = _dispatch(x, pos, grouped_init)
    ys = _expert_ffn(xs, tile_expert, tiles_end[-1:].astype(jnp.int32), w_gu, w_down, layer)
    return _combine(x, meta, pos, ys, g, b, alpha), xs


def _rotate_half_columns(w):
    half = ROPE_DIM // 2
    col = jnp.arange(w.shape[1])
    c = col % HEAD_DIM
    src = jnp.where(c < half, col + half, col - half)
    sign = jnp.where(c < half, -1.0, jnp.where(c < ROPE_DIM, 1.0, 0.0)).astype(w.dtype)
    return w[:, jnp.clip(src, 0, w.shape[1] - 1)] * sign


def _rope_lane_frequencies():
    inv_freq = ROPE_THETA ** (-jnp.arange(0, ROPE_DIM, 2, dtype=F32) / ROPE_DIM)
    c = jnp.arange(LANES) % HEAD_DIM
    return jnp.where(c < ROPE_DIM, inv_freq[c % (ROPE_DIM // 2)], 0.0).astype(F32)[None, :]


def kernel(x, mem, positions, w_in_a, conv_a, w_q_b, w_kv_shared, w_mem_kv, w_o, ln1_g, ln1_b,
           ln2_g, ln2_b, w_gu_dense, w_down_dense, w_router, w_gu_moe, w_down_moe):
    bsz, seq, d = x.shape
    depth = w_o.shape[0]
    n_a = w_in_a.shape[0]
    alpha = float((2 * depth) ** 0.25)
    assert seq % MOBA_BLOCK == 0 and seq // MOBA_BLOCK <= LANES

    pos = positions.astype(F32)[:, :, None]
    freq = _rope_lane_frequencies()
    mem_len = mem.shape[1]
    w_mem_all = jnp.transpose(w_mem_kv, (1, 0, 2)).reshape(d, depth * 2 * MEM_WIDTH).astype(BF16)
    mem_kv = _matmul(mem.reshape(bsz * mem_len, d), w_mem_all, BF16).reshape(bsz, mem_len, -1)

    w_gu_dense_bf16, w_down_dense_bf16 = w_gu_dense.astype(BF16), w_down_dense.astype(BF16)
    w_gu_moe_bf16, w_down_moe_bf16 = w_gu_moe.astype(BF16), w_down_moe.astype(BF16)
    shared = None
    grouped = None
    for layer in range(depth):
        g1, b1 = ln1_g[layer][None, :], ln1_b[layer][None, :]
        g2, b2 = ln2_g[layer][None, :], ln2_b[layer][None, :]
        wo = w_o[layer].astype(BF16)
        if layer < n_a:
            x = _mixer_a(x, w_in_a[layer].astype(BF16), conv_a[layer], mem_kv, layer, wo, g1, b1, alpha)
        else:
            wq = w_q_b[layer - n_a]
            q, mem_out = _query_b(x, pos, freq, wq.astype(BF16),
                                  _rotate_half_columns(wq[:, :MIX_WIDTH]).astype(BF16), mem_kv, layer)
            main = _moba(q, *shared)
            x = _mixer_out(x, main, mem_out, wo, g1, b1, alpha)
        xf = x.reshape(bsz * seq, d)
        if layer % 2 == 0:
            xf = _ffn(xf, w_gu_dense_bf16, w_down_dense_bf16, layer // 2, g2, b2, alpha)
        else:
            xf, grouped = _moe(xf, w_router[layer // 2], w_gu_moe_bf16, w_down_moe_bf16, layer // 2,
                               g2, b2, alpha, grouped)
        x = xf.reshape(bsz, seq, d)
        if layer == n_a - 1:
            wk = w_kv_shared[:, :MIX_WIDTH]
            k, v, km = _shared_kv(x, pos, freq, wk.astype(BF16), _rotate_half_columns(wk).astype(BF16),
                                  w_kv_shared[:, MIX_WIDTH:].astype(BF16))
            n_blocks = seq // MOBA_BLOCK
            km = jnp.pad(km.reshape(bsz, n_blocks, MIX_WIDTH), ((0, 0), (0, LANES - n_blocks), (0, 0)))
            shared = (k, v, km)
    return x
```

```python
import functools

import jax
import jax.numpy as jnp
from jax import lax
from jax.experimental import pallas as pl
from jax.experimental.pallas import tpu as pltpu

F32 = jnp.float32
BF16 = jnp.bfloat16

HEAD_DIM = 64
MIX_HEADS = 12
MIX_WIDTH = MIX_HEADS * HEAD_DIM
MEM_HEADS = 4
MEM_WIDTH = MEM_HEADS * HEAD_DIM
CONV_WIDTH = 3
ROPE_DIM = HEAD_DIM // 4
ROPE_THETA = 500000.0
MOBA_BLOCK = 256
MOBA_TOPK = 3
TOP_K = 2
LN_EPS = 1e-5
ATTN_SCALE = HEAD_DIM ** -0.5
LOG2_E = 1.4426950408889634

LANES = 128
HEADS_PER_LANE_GROUP = LANES // HEAD_DIM
MASK_VALUE = -1e30
VMEM_LIMIT_BYTES = 56 * 1024 * 1024

SEQ_TILE = 512
FFN_TILE = 1024
FF_CHUNK = 1408
MOBA_CHUNK_BLOCKS = 4
ROUTE_TILE = 1024
EXPERT_TILE = 512
COMBINE_TILE = 1024
DMA_LOOP_UNROLL = 8


def _params(semantics):
    return pltpu.CompilerParams(dimension_semantics=semantics, vmem_limit_bytes=VMEM_LIMIT_BYTES)


def _layer_norm(r, g, b):
    mu = jnp.mean(r, axis=-1, keepdims=True)
    c = r - mu
    var = jnp.mean(c * c, axis=-1, keepdims=True)
    return c * lax.rsqrt(var + LN_EPS) * g + b


def _dot(a, b):
    return jnp.dot(a, b, preferred_element_type=F32)


def _dot_nt(a, b):
    return lax.dot_general(a, b, (((1,), (1,)), ((), ())), preferred_element_type=F32)


def _head_lane_mask(width, head):
    lane = lax.broadcasted_iota(jnp.int32, (1, width), 1)
    return (lane >= head * HEAD_DIM) & (lane < (head + 1) * HEAD_DIM)


def _mem_attention(q_mem, mem_k, mem_v):
    out = jnp.zeros(q_mem.shape, F32)
    for head in range(MEM_HEADS):
        hm = _head_lane_mask(MEM_WIDTH, head)
        qh = jnp.where(hm, q_mem * ATTN_SCALE, 0.0).astype(BF16)
        s = _dot_nt(qh, mem_k)
        m = jnp.max(s, axis=-1, keepdims=True)
        p = jnp.exp(s - m)
        l = jnp.sum(p, axis=-1, keepdims=True)
        o = _dot(p.astype(BF16), mem_v) / l
        out = jnp.where(hm, o, out)
    return out


def _rope_tables(pos, freq, reps):
    angle = pos * freq
    cos = jnp.cos(angle)
    sin = jnp.sin(angle)
    return (jnp.concatenate([cos] * reps, axis=1), jnp.concatenate([sin] * reps, axis=1))


def _matmul_kernel(x_ref, w_ref, o_ref):
    o_ref[...] = _dot(x_ref[...].astype(BF16), w_ref[...]).astype(o_ref.dtype)


def _matmul(x, w, out_dtype, tile_m=256):
    m, k = x.shape
    n = w.shape[1]
    return pl.pallas_call(
        _matmul_kernel,
        grid=(m // tile_m,),
        in_specs=[pl.BlockSpec((tile_m, k), lambda i: (i, 0)),
                  pl.BlockSpec((k, n), lambda i: (0, 0))],
        out_specs=pl.BlockSpec((tile_m, n), lambda i: (i, 0)),
        out_shape=jax.ShapeDtypeStruct((m, n), out_dtype),
        compiler_params=_params(("arbitrary",)),
        name="mem_kv_proj",
    )(x, w)


def _mixer_a_kernel(x_ref, win_ref, conv_ref, mk_ref, mv_ref, wo_ref, g_ref, b_ref, o_ref,
                    ztail_ref, *, alpha):
    tile = x_ref.shape[1]

    @pl.when(pl.program_id(1) == 0)
    def _():
        ztail_ref[...] = jnp.zeros_like(ztail_ref)

    x = x_ref[0]
    proj = _dot(x.astype(BF16), win_ref[...])
    u_b = proj[:, :MIX_WIDTH]
    u_c = proj[:, MIX_WIDTH:2 * MIX_WIDTH]
    u_h = proj[:, 2 * MIX_WIDTH:3 * MIX_WIDTH]
    q_mem = proj[:, 3 * MIX_WIDTH:]

    z = u_c * u_h
    tail = ztail_ref[...]
    ztail_ref[...] = z[tile - 8:, :]
    row = lax.broadcasted_iota(jnp.int32, z.shape, 0)
    z1 = jnp.where(row == 0, tail[7:8, :], pltpu.roll(z, 1, 0))
    z2 = jnp.where(row == 0, tail[6:7, :], jnp.where(row == 1, tail[7:8, :], pltpu.roll(z, 2, 0)))
    w = conv_ref[...]
    main = u_b * (w[0:1, :] * z2 + w[1:2, :] * z1 + w[2:3, :] * z)

    mem_out = _mem_attention(q_mem, mk_ref[0], mv_ref[0])
    mix = (_dot(main.astype(BF16), wo_ref[:MIX_WIDTH, :])
           + _dot(mem_out.astype(BF16), wo_ref[MIX_WIDTH:, :]))
    o_ref[0] = _layer_norm(alpha * x + mix, g_ref[...], b_ref[...])


def _mixer_a(x, w_in, conv_w, mem_kv, layer, w_o, g, b, alpha):
    bsz, seq, d = x.shape
    tile = min(SEQ_TILE, seq)
    mem_len = mem_kv.shape[1]
    return pl.pallas_call(
        functools.partial(_mixer_a_kernel, alpha=alpha),
        grid=(bsz, seq // tile),
        in_specs=[
            pl.BlockSpec((1, tile, d), lambda bi, ti: (bi, ti, 0)),
            pl.BlockSpec(w_in.shape, lambda bi, ti: (0, 0)),
            pl.BlockSpec(conv_w.shape, lambda bi, ti: (0, 0)),
            pl.BlockSpec((1, mem_len, MEM_WIDTH), lambda bi, ti: (bi, 0, 2 * layer)),
            pl.BlockSpec((1, mem_len, MEM_WIDTH), lambda bi, ti: (bi, 0, 2 * layer + 1)),
            pl.BlockSpec(w_o.shape, lambda bi, ti: (0, 0)),
            pl.BlockSpec((1, d), lambda bi, ti: (0, 0)),
            pl.BlockSpec((1, d), lambda bi, ti: (0, 0)),
        ],
        out_specs=pl.BlockSpec((1, tile, d), lambda bi, ti: (bi, ti, 0)),
        out_shape=jax.ShapeDtypeStruct(x.shape, F32),
        scratch_shapes=[pltpu.VMEM((8, MIX_WIDTH), F32)],
        compiler_params=_params(("arbitrary", "arbitrary")),
        name="mixer_a",
    )(x, w_in, conv_w, mem_kv, mem_kv, w_o, g, b)


def _shared_kv_kernel(x_ref, pos_ref, freq_ref, wk_ref, wkr_ref, wv_ref, k_ref, v_ref, km_ref):
    xb = x_ref[0].astype(BF16)
    cos, sin = _rope_tables(pos_ref[0], freq_ref[...], MIX_WIDTH // LANES)
    k = _dot(xb, wk_ref[...]) * cos + _dot(xb, wkr_ref[...]) * sin
    k_ref[0] = k.astype(BF16)
    v_ref[0] = _dot(xb, wv_ref[...]).astype(BF16)
    km_ref[0, 0] = jnp.mean(k, axis=0, keepdims=True)


def _shared_kv(x, pos, freq, w_k, w_kr, w_v):
    bsz, seq, d = x.shape
    n_blocks = seq // MOBA_BLOCK
    full = lambda bi, ti: (0, 0)
    return pl.pallas_call(
        _shared_kv_kernel,
        grid=(bsz, n_blocks),
        in_specs=[
            pl.BlockSpec((1, MOBA_BLOCK, d), lambda bi, ti: (bi, ti, 0)),
            pl.BlockSpec((1, MOBA_BLOCK, 1), lambda bi, ti: (bi, ti, 0)),
            pl.BlockSpec(freq.shape, full),
            pl.BlockSpec(w_k.shape, full),
            pl.BlockSpec(w_kr.shape, full),
            pl.BlockSpec(w_v.shape, full),
        ],
        out_specs=[
            pl.BlockSpec((1, MOBA_BLOCK, MIX_WIDTH), lambda bi, ti: (bi, ti, 0)),
            pl.BlockSpec((1, MOBA_BLOCK, MIX_WIDTH), lambda bi, ti: (bi, ti, 0)),
            pl.BlockSpec((1, 1, 1, MIX_WIDTH), lambda bi, ti: (bi, ti, 0, 0)),
        ],
        out_shape=[
            jax.ShapeDtypeStruct((bsz, seq, MIX_WIDTH), BF16),
            jax.ShapeDtypeStruct((bsz, seq, MIX_WIDTH), BF16),
            jax.ShapeDtypeStruct((bsz, n_blocks, 1, MIX_WIDTH), F32),
        ],
        compiler_params=_params(("arbitrary", "arbitrary")),
        name="shared_kv",
    )(x, pos, freq, w_k, w_kr, w_v)


def _query_b_kernel(x_ref, pos_ref, freq_ref, wq_ref, wqr_ref, mk_ref, mv_ref, q_ref, mo_ref):
    xb = x_ref[0].astype(BF16)
    proj = _dot(xb, wq_ref[...])
    cos, sin = _rope_tables(pos_ref[0], freq_ref[...], MIX_WIDTH // LANES)
    q = proj[:, :MIX_WIDTH] * cos + _dot(xb, wqr_ref[...]) * sin
    q_ref[0] = (q * (ATTN_SCALE * LOG2_E)).astype(BF16)
    mo_ref[0] = _mem_attention(proj[:, MIX_WIDTH:], mk_ref[0], mv_ref[0]).astype(BF16)


def _query_b(x, pos, freq, w_q, w_qr, mem_kv, layer):
    bsz, seq, d = x.shape
    tile = min(SEQ_TILE, seq)
    mem_len = mem_kv.shape[1]
    full = lambda bi, ti: (0, 0)
    return pl.pallas_call(
        _query_b_kernel,
        grid=(bsz, seq // tile),
        in_specs=[
            pl.BlockSpec((1, tile, d), lambda bi, ti: (bi, ti, 0)),
            pl.BlockSpec((1, tile, 1), lambda bi, ti: (bi, ti, 0)),
            pl.BlockSpec(freq.shape, full),
            pl.BlockSpec(w_q.shape, full),
            pl.BlockSpec(w_qr.shape, full),
            pl.BlockSpec((1, mem_len, MEM_WIDTH), lambda bi, ti: (bi, 0, 2 * layer)),
            pl.BlockSpec((1, mem_len, MEM_WIDTH), lambda bi, ti: (bi, 0, 2 * layer + 1)),
        ],
        out_specs=[
            pl.BlockSpec((1, tile, MIX_WIDTH), lambda bi, ti: (bi, ti, 0)),
            pl.BlockSpec((1, tile, MEM_WIDTH), lambda bi, ti: (bi, ti, 0)),
        ],
        out_shape=[
            jax.ShapeDtypeStruct((bsz, seq, MIX_WIDTH), BF16),
            jax.ShapeDtypeStruct((bsz, seq, MEM_WIDTH), BF16),
        ],
        compiler_params=_params(("arbitrary", "arbitrary")),
        name="query_b",
    )(x, pos, freq, w_q, w_qr, mem_kv, mem_kv)


def _moba_kernel(q_ref, k_ref, v_ref, km_ref, o_ref, kaug_ref, vaug_ref, m_ref, acc_ref, s0_ref,
                 s1_ref, *, chunk_blocks):
    blk = MOBA_BLOCK
    span = chunk_blocks * blk
    pair = pl.program_id(2)
    chains = [(u, h) for u in range(chunk_blocks) for h in range(HEADS_PER_LANE_GROUP)]
    rows = len(chains) * blk

    @pl.when(pair == 0)
    def _():
        kaug_ref[:, :LANES] = k_ref[0]
        vaug_ref[:, :LANES] = v_ref[0]
        lane = lax.broadcasted_iota(jnp.int32, (blk, LANES), 1)
        ones_lane = (lane == 0).astype(BF16)

        def fill(j, carry):
            off = pl.multiple_of(j * blk, blk)
            kaug_ref[pl.ds(off, blk), LANES:] = (lane == j).astype(BF16)
            vaug_ref[pl.ds(off, blk), LANES:] = ones_lane
            return carry

        lax.fori_loop(0, k_ref.shape[1] // blk, fill, 0)

    def chunk_rows(c):
        return pl.ds(pl.multiple_of(c * span, span), span)

    def sweep(half, sa_ref, sb_ref):
        own_chunk = 2 * pair + half
        q = q_ref[0, half * span:(half + 1) * span, :]
        q_heads = jnp.concatenate(
            [jnp.where(_head_lane_mask(LANES, h), q[u * blk:(u + 1) * blk], 0).astype(BF16)
             for u, h in chains], axis=0)

        block_id = lax.broadcasted_iota(jnp.int32, (rows, LANES), 1)
        own = (own_chunk * chunk_blocks
               + lax.broadcasted_iota(jnp.int32, (rows, LANES), 0) // (HEADS_PER_LANE_GROUP * blk))
        is_past = block_id < own
        gate = jnp.where(is_past, _dot_nt(q_heads, km_ref[0].astype(BF16)), -jnp.inf)
        kth = gate
        for _ in range(MOBA_TOPK - 1):
            kth = jnp.where(kth == jnp.max(kth, axis=-1, keepdims=True), -jnp.inf, kth)
        threshold = jnp.max(kth, axis=-1, keepdims=True)
        allowed = (is_past & (gate >= threshold)) | (block_id == own)
        block_bias = jnp.where(allowed, 0.0, MASK_VALUE).astype(BF16)
        q_aug = jnp.concatenate([q_heads, block_bias], axis=1)

        state_rows = [slice(half * rows + i * blk, half * rows + (i + 1) * blk)
                      for i in range(len(chains))]
        chain_rows = [slice(i * blk, (i + 1) * blk) for i in range(len(chains))]
        for r in state_rows:
            m_ref[r, :] = jnp.full((blk, 1), MASK_VALUE, F32)
            acc_ref[r, :] = jnp.zeros((blk, acc_ref.shape[1]), F32)

        def score_own(s_ref):
            for r, (u, _) in zip(chain_rows, chains):
                width = (u + 1) * blk
                k_part = kaug_ref[pl.ds(pl.multiple_of(own_chunk * span, span), width), :]
                k_pos = lax.broadcasted_iota(jnp.int32, (blk, width), 1)
                q_pos = u * blk + lax.broadcasted_iota(jnp.int32, (blk, width), 0)
                s_ref[r, :width] = jnp.where(k_pos <= q_pos, _dot_nt(q_aug[r], k_part), MASK_VALUE)
                if width < span:
                    s_ref[r, width:] = jnp.full((blk, span - width), MASK_VALUE, F32)

        def score_past(c, s_ref):
            k_chunk = kaug_ref[chunk_rows(c), :]
            for r in chain_rows:
                s_ref[r, :] = _dot_nt(q_aug[r], k_chunk)

        def softmax_value(c, s_ref):
            v_chunk = vaug_ref[chunk_rows(c), :]
            for r, sr in zip(chain_rows, state_rows):
                s = s_ref[r, :]
                m_old = m_ref[sr, :]
                m_new = jnp.maximum(m_old, jnp.max(s, axis=-1, keepdims=True))
                p = jnp.exp2((s - m_new).astype(BF16))
                acc_ref[sr, :] = jnp.exp2(m_old - m_new) * acc_ref[sr, :] + _dot(p, v_chunk)
                m_ref[sr, :] = m_new

        def step_chunk(j):
            return jnp.where(j == 0, own_chunk, j - 1)

        n_past = own_chunk
        score_own(sa_ref)

        def two_steps(i, carry):
            j = 2 * i + 1
            score_past(j - 1, sb_ref)
            softmax_value(step_chunk(j - 1), sa_ref)
            score_past(j, sa_ref)
            softmax_value(j - 1, sb_ref)
            return carry

        lax.fori_loop(0, pair, two_steps, 0)
        if half == 1:
            score_past(n_past - 1, sb_ref)
            softmax_value(step_chunk(n_past - 1), sa_ref)
            softmax_value(n_past - 1, sb_ref)
        else:
            softmax_value(step_chunk(n_past), sa_ref)

        for u in range(chunk_blocks):
            out = None
            for h in range(HEADS_PER_LANE_GROUP):
                acc = acc_ref[state_rows[u * HEADS_PER_LANE_GROUP + h], :]
                o = acc[:, :LANES] / acc[:, LANES:LANES + 1]
                out = o if out is None else jnp.where(_head_lane_mask(LANES, h), o, out)
            o_ref[0, half * span + u * blk:half * span + (u + 1) * blk, :] = out.astype(o_ref.dtype)

    sweep(0, s0_ref, s1_ref)
    sweep(1, s1_ref, s0_ref)


def _moba(q, k, v, k_means):
    bsz, seq, _ = q.shape
    n_blocks = seq // MOBA_BLOCK
    groups = MIX_WIDTH // LANES
    chunk_blocks = MOBA_CHUNK_BLOCKS
    while n_blocks % (2 * chunk_blocks):
        chunk_blocks //= 2
    assert chunk_blocks >= 1, "the sequence must hold an even number of MoBA blocks"
    span = chunk_blocks * MOBA_BLOCK
    rows = HEADS_PER_LANE_GROUP * span
    return pl.pallas_call(
        functools.partial(_moba_kernel, chunk_blocks=chunk_blocks),
        grid=(bsz, groups, seq // (2 * span)),
        in_specs=[
            pl.BlockSpec((1, 2 * span, LANES), lambda bi, gi, qi: (bi, qi, gi)),
            pl.BlockSpec((1, seq, LANES), lambda bi, gi, qi: (bi, 0, gi)),
            pl.BlockSpec((1, seq, LANES), lambda bi, gi, qi: (bi, 0, gi)),
            pl.BlockSpec((1, LANES, LANES), lambda bi, gi, qi: (bi, 0, gi)),
        ],
        out_specs=pl.BlockSpec((1, 2 * span, LANES), lambda bi, gi, qi: (bi, qi, gi)),
        out_shape=jax.ShapeDtypeStruct(q.shape, BF16),
        scratch_shapes=[pltpu.VMEM((seq, 2 * LANES), BF16), pltpu.VMEM((seq, 2 * LANES), BF16),
                        pltpu.VMEM((2 * rows, 1), F32), pltpu.VMEM((2 * rows, 2 * LANES), F32),
                        pltpu.VMEM((rows, span), F32), pltpu.VMEM((rows, span), F32)],
        compiler_params=_params(("arbitrary", "arbitrary", "arbitrary")),
        name="moba",
    )(q, k, v, k_means)


def _mixer_out_kernel(x_ref, main_ref, mo_ref, wo_ref, g_ref, b_ref, o_ref, *, alpha):
    mix = _dot(main_ref[0], wo_ref[:MIX_WIDTH, :]) + _dot(mo_ref[0], wo_ref[MIX_WIDTH:, :])
    o_ref[0] = _layer_norm(alpha * x_ref[0] + mix, g_ref[...], b_ref[...])


def _mixer_out(x, main, mem_out, w_o, g, b, alpha):
    bsz, seq, d = x.shape
    tile = min(SEQ_TILE, seq)
    full = lambda bi, ti: (0, 0)
    return pl.pallas_call(
        functools.partial(_mixer_out_kernel, alpha=alpha),
        grid=(bsz, seq // tile),
        in_specs=[
            pl.BlockSpec((1, tile, d), lambda bi, ti: (bi, ti, 0)),
            pl.BlockSpec((1, tile, MIX_WIDTH), lambda bi, ti: (bi, ti, 0)),
            pl.BlockSpec((1, tile, MEM_WIDTH), lambda bi, ti: (bi, ti, 0)),
            pl.BlockSpec(w_o.shape, full),
            pl.BlockSpec((1, d), full),
            pl.BlockSpec((1, d), full),
        ],
        out_specs=pl.BlockSpec((1, tile, d), lambda bi, ti: (bi, ti, 0)),
        out_shape=jax.ShapeDtypeStruct(x.shape, F32),
        compiler_params=_params(("arbitrary", "arbitrary")),
        name="mixer_out",
    )(x, main, mem_out, w_o, g, b)


def _swiglu_chunk(xb, wg, wu, wd):
    gp = _dot(xb, wg)
    up = _dot(xb, wu)
    return _dot((gp * jax.nn.sigmoid(gp) * up).astype(BF16), wd)


def _ffn_chunks(d_ff):
    chunk = FF_CHUNK if d_ff % FF_CHUNK == 0 else d_ff
    return chunk, d_ff // chunk


def _accumulate_chunks(c, n_chunks, part, acc_ref, finish):
    if n_chunks == 1:
        finish(part())
        return

    @pl.when(c == 0)
    def _():
        acc_ref[...] = part()

    if n_chunks > 2:
        @pl.when((c > 0) & (c < n_chunks - 1))
        def _():
            acc_ref[...] += part()

    @pl.when(c == n_chunks - 1)
    def _():
        finish(acc_ref[...] + part())


def _ffn_kernel(x_ref, wg_ref, wu_ref, wd_ref, g_ref, b_ref, o_ref, acc_ref, xb_ref, *, alpha,
                n_chunks):
    c = pl.program_id(1)

    @pl.when(c == 0)
    def _():
        xb_ref[...] = x_ref[...].astype(BF16)

    def finish(ffn):
        o_ref[...] = _layer_norm(alpha * x_ref[...] + ffn, g_ref[...], b_ref[...])

    def part():
        return _swiglu_chunk(xb_ref[...], wg_ref[...], wu_ref[...], wd_ref[...])

    _accumulate_chunks(c, n_chunks, part, acc_ref, finish)


def _ffn(x, w_gu, w_down, layer, g, b, alpha):
    n, d = x.shape
    d_ff = w_down.shape[1]
    tile = min(FFN_TILE, n)
    chunk, n_chunks = _ffn_chunks(d_ff)
    return pl.pallas_call(
        functools.partial(_ffn_kernel, alpha=alpha, n_chunks=n_chunks),
        grid=(n // tile, n_chunks),
        in_specs=[
            pl.BlockSpec((tile, d), lambda i, c: (i, 0)),
            pl.BlockSpec((None, d, chunk), lambda i, c: (layer, 0, c)),
            pl.BlockSpec((None, d, chunk), lambda i, c: (layer, 0, c + n_chunks)),
            pl.BlockSpec((None, chunk, d), lambda i, c: (layer, c, 0)),
            pl.BlockSpec((1, d), lambda i, c: (0, 0)),
            pl.BlockSpec((1, d), lambda i, c: (0, 0)),
        ],
        out_specs=pl.BlockSpec((tile, d), lambda i, c: (i, 0)),
        out_shape=jax.ShapeDtypeStruct(x.shape, F32),
        scratch_shapes=[pltpu.VMEM((tile, d), F32), pltpu.VMEM((tile, d), BF16)],
        compiler_params=_params(("arbitrary", "arbitrary")),
        name="ffn_dense",
    )(x, w_gu, w_gu, w_down, g, b)


META_EXPERT, META_RANK, META_GATE = 0, 2, 4


def _split_bf16(a):
    hi = a.astype(BF16)
    return hi, (a - hi.astype(F32)).astype(BF16)


def _route_kernel(x_ref, wr_ref, meta_ref, counts_ref, *, n_experts):
    tile = x_ref.shape[0]

    @pl.when(pl.program_id(0) == 0)
    def _():
        counts_ref[...] = jnp.zeros_like(counts_ref)

    lane = lax.broadcasted_iota(jnp.int32, (tile, LANES), 1)
    x_hi, x_lo = _split_bf16(x_ref[...])
    w_hi, w_lo = _split_bf16(wr_ref[...])
    logits = _dot(x_hi, w_hi) + (_dot(x_hi, w_lo) + _dot(x_lo, w_hi))
    logits = jnp.where(lane < n_experts, logits, -jnp.inf)
    top1 = jnp.max(logits, axis=-1, keepdims=True)
    idx1 = jnp.min(jnp.where(logits == top1, lane, LANES), axis=-1, keepdims=True)
    rest = jnp.where(lane == idx1, -jnp.inf, logits)
    top2 = jnp.max(rest, axis=-1, keepdims=True)
    idx2 = jnp.min(jnp.where(rest == top2, lane, LANES), axis=-1, keepdims=True)
    e2 = jnp.exp(top2 - top1)

    chosen = (lane == idx1) | (lane == idx2)
    earlier = (lax.broadcasted_iota(jnp.int32, (tile, tile), 1)
               < lax.broadcasted_iota(jnp.int32, (tile, tile), 0))
    rank = counts_ref[...] + _dot(earlier.astype(BF16), chosen.astype(BF16))
    rank1 = jnp.sum(jnp.where(lane == idx1, rank, 0.0), axis=-1, keepdims=True)
    rank2 = jnp.sum(jnp.where(lane == idx2, rank, 0.0), axis=-1, keepdims=True)
    counts_ref[...] += jnp.sum(chosen.astype(F32), axis=0, keepdims=True)

    fields = [idx1.astype(F32), idx2.astype(F32), rank1, rank2, 1.0 / (1.0 + e2), e2 / (1.0 + e2)]
    meta = jnp.zeros((tile, LANES), F32)
    for i, field in enumerate(fields):
        meta = jnp.where(lane == i, field, meta)
    meta_ref[...] = meta


def _route(x, w_router):
    n, d = x.shape
    tile = min(ROUTE_TILE, n)
    return pl.pallas_call(
        functools.partial(_route_kernel, n_experts=w_router.shape[1]),
        grid=(n // tile,),
        in_specs=[pl.BlockSpec((tile, d), lambda i: (i, 0)),
                  pl.BlockSpec((d, LANES), lambda i: (0, 0))],
        out_specs=[pl.BlockSpec((tile, LANES), lambda i: (i, 0)),
                   pl.BlockSpec((1, LANES), lambda i: (0, 0))],
        out_shape=[jax.ShapeDtypeStruct((n, LANES), F32), jax.ShapeDtypeStruct((1, LANES), F32)],
        compiler_params=_params(("arbitrary",)),
        name="moe_route",
    )(x, jnp.pad(w_router, ((0, 0), (0, LANES - w_router.shape[1]))))


def _row_copy(src_ref, src_row, dst_ref, dst_row, sem):
    return pltpu.make_async_copy(src_ref.at[pl.ds(src_row, 1), :], dst_ref.at[pl.ds(dst_row, 1), :], sem)


def _start_then_wait(copies, n):
    def start(r, carry):
        for cp in copies(r):
            cp.start()
        return carry

    def wait(r, carry):
        for cp in copies(r):
            cp.wait()
        return carry

    lax.fori_loop(0, n, start, 0, unroll=DMA_LOOP_UNROLL)
    lax.fori_loop(0, n, wait, 0, unroll=DMA_LOOP_UNROLL)


def _dispatch_kernel(pos_ref, x_ref, init_ref, xs_ref, sem):
    del init_ref

    def copies(r):
        return [_row_copy(x_ref, r, xs_ref, pos_ref[0, 0, TOP_K * r + k], sem) for k in range(TOP_K)]

    _start_then_wait(copies, x_ref.shape[0])


def _dispatch(x, pos, grouped_init):
    n, d = x.shape
    tile = min(ROUTE_TILE, n)
    any_spec = pl.BlockSpec(memory_space=pl.ANY)
    return pl.pallas_call(
        _dispatch_kernel,
        grid=(n // tile,),
        in_specs=[pl.BlockSpec((1, 1, TOP_K * tile), lambda i: (i, 0, 0), memory_space=pltpu.SMEM),
                  pl.BlockSpec((tile, d), lambda i: (i, 0)), any_spec],
        out_specs=any_spec,
        out_shape=jax.ShapeDtypeStruct(grouped_init.shape, F32),
        scratch_shapes=[pltpu.SemaphoreType.DMA],
        input_output_aliases={2: 0},
        compiler_params=_params(("arbitrary",)),
        name="moe_dispatch",
    )(pos.reshape(n // tile, 1, TOP_K * tile), x, grouped_init)


def _expert_ffn_kernel(tile_expert_ref, n_tiles_ref, xs_ref, wg_ref, wu_ref, wd_ref, o_ref,
                       acc_ref, xb_ref, *, n_chunks):
    del tile_expert_ref
    c = pl.program_id(1)

    @pl.when(pl.program_id(0) < n_tiles_ref[0])
    def _():
        @pl.when(c == 0)
        def _():
            xb_ref[...] = xs_ref[...].astype(BF16)

        def finish(ffn):
            o_ref[...] = ffn

        def part():
            return _swiglu_chunk(xb_ref[...], wg_ref[0], wu_ref[0], wd_ref[0])

        _accumulate_chunks(c, n_chunks, part, acc_ref, finish)


def _expert_ffn(xs, tile_expert, n_tiles, w_gu, w_down, layer):
    n_rows, d = xs.shape
    d_ff = w_down.shape[2]
    chunk, n_chunks = _ffn_chunks(d_ff)
    row_tile = lambda t, c, te, nt: (jnp.minimum(t, nt[0] - 1), 0)
    grid_spec = pltpu.PrefetchScalarGridSpec(
        num_scalar_prefetch=2,
        grid=(n_rows // EXPERT_TILE, n_chunks),
        in_specs=[
            pl.BlockSpec((EXPERT_TILE, d), row_tile),
            pl.BlockSpec((None, 1, d, chunk), lambda t, c, te, nt: (layer, te[t], 0, c)),
            pl.BlockSpec((None, 1, d, chunk), lambda t, c, te, nt: (layer, te[t], 0, c + n_chunks)),
            pl.BlockSpec((None, 1, chunk, d), lambda t, c, te, nt: (layer, te[t], c, 0)),
        ],
        out_specs=pl.BlockSpec((EXPERT_TILE, d), row_tile),
        scratch_shapes=[pltpu.VMEM((EXPERT_TILE, d), F32), pltpu.VMEM((EXPERT_TILE, d), BF16)],
    )
    return pl.pallas_call(
        functools.partial(_expert_ffn_kernel, n_chunks=n_chunks),
        grid_spec=grid_spec,
        out_shape=jax.ShapeDtypeStruct((n_rows, d), F32),
        compiler_params=_params(("arbitrary", "arbitrary")),
        name="moe_experts",
    )(tile_expert, n_tiles, xs, w_gu, w_gu, w_down)


def _combine_kernel(pos_ref, meta_ref, x_ref, g_ref, b_ref, ys_ref, o_ref, rows_ref, sem, *, alpha):
    tile = x_ref.shape[0]

    def copies(r):
        return [_row_copy(ys_ref, pos_ref[0, 0, TOP_K * r + k], rows_ref.at[k], r, sem)
                for k in range(TOP_K)]

    _start_then_wait(copies, tile)

    meta = meta_ref[...]
    lane = lax.broadcasted_iota(jnp.int32, meta.shape, 1)
    ffn = jnp.zeros(x_ref.shape, F32)
    for k in range(TOP_K):
        gate = jnp.sum(jnp.where(lane == META_GATE + k, meta, 0.0), axis=-1, keepdims=True)
        ffn = ffn + gate * rows_ref[k]
    o_ref[...] = _layer_norm(alpha * x_ref[...] + ffn, g_ref[...], b_ref[...])


def _combine(x, meta, pos, ys, g, b, alpha):
    n, d = x.shape
    tile = min(COMBINE_TILE, n)
    return pl.pallas_call(
        functools.partial(_combine_kernel, alpha=alpha),
        grid=(n // tile,),
        in_specs=[pl.BlockSpec((1, 1, TOP_K * tile), lambda i: (i, 0, 0), memory_space=pltpu.SMEM),
                  pl.BlockSpec((tile, LANES), lambda i: (i, 0)),
                  pl.BlockSpec((tile, d), lambda i: (i, 0)),
                  pl.BlockSpec((1, d), lambda i: (0, 0)),
                  pl.BlockSpec((1, d), lambda i: (0, 0)),
                  pl.BlockSpec(memory_space=pl.ANY)],
        out_specs=pl.BlockSpec((tile, d), lambda i: (i, 0)),
        out_shape=jax.ShapeDtypeStruct(x.shape, F32),
        scratch_shapes=[pltpu.VMEM((TOP_K, tile, d), F32), pltpu.SemaphoreType.DMA],
        compiler_params=_params(("arbitrary",)),
        name="moe_combine",
    )(pos.reshape(n // tile, 1, TOP_K * tile), meta, x, g, b, ys)


def _moe(x, w_router, w_gu, w_down, layer, g, b, alpha, grouped_init):
    n, d = x.shape
    n_experts = w_router.shape[1]
    meta, counts = _route(x, w_router)

    counts = counts[0, :n_experts].astype(jnp.int32)
    group_tiles = (counts + EXPERT_TILE - 1) // EXPERT_TILE
    tiles_end = jnp.cumsum(group_tiles)
    group_row0 = (tiles_end - group_tiles) * EXPERT_TILE
    n_grid_tiles = (TOP_K * n) // EXPERT_TILE + n_experts
    tile_ids = jnp.arange(n_grid_tiles, dtype=jnp.int32)
    tile_expert = jnp.minimum(jnp.sum((tile_ids[:, None] >= tiles_end[None, :]).astype(jnp.int32), axis=1),
                              n_experts - 1)
    expert = meta[:, META_EXPERT:META_EXPERT + TOP_K].astype(jnp.int32)
    rank = meta[:, META_RANK:META_RANK + TOP_K].astype(jnp.int32)
    pos = (group_row0[expert] + rank).reshape(-1)

    if grouped_init is None:
        grouped_init = jnp.zeros((n_grid_tiles * EXPERT_TILE, d), F32)
    xs = _dispatch(x, pos, grouped_init)
    ys = _expert_ffn(xs, tile_expert, tiles_end[-1:].astype(jnp.int32), w_gu, w_down, layer)
    return _combine(x, meta, pos, ys, g, b, alpha), xs


def _rotate_half_columns(w):
    half = ROPE_DIM // 2
    col = jnp.arange(w.shape[1])
    c = col % HEAD_DIM
    src = jnp.where(c < half, col + half, col - half)
    sign = jnp.where(c < half, -1.0, jnp.where(c < ROPE_DIM, 1.0, 0.0)).astype(w.dtype)
    return w[:, jnp.clip(src, 0, w.shape[1] - 1)] * sign


def _rope_lane_frequencies():
    inv_freq = ROPE_THETA ** (-jnp.arange(0, ROPE_DIM, 2, dtype=F32) / ROPE_DIM)
    c = jnp.arange(LANES) % HEAD_DIM
    return jnp.where(c < ROPE_DIM, inv_freq[c % (ROPE_DIM // 2)], 0.0).astype(F32)[None, :]


def kernel(x, mem, positions, w_in_a, conv_a, w_q_b, w_kv_shared, w_mem_kv, w_o, ln1_g, ln1_b,
           ln2_g, ln2_b, w_gu_dense, w_down_dense, w_router, w_gu_moe, w_down_moe):
    bsz, seq, d = x.shape
    depth = w_o.shape[0]
    n_a = w_in_a.shape[0]
    alpha = float((2 * depth) ** 0.25)
    assert seq % MOBA_BLOCK == 0 and seq // MOBA_BLOCK <= LANES

    pos = positions.astype(F32)[:, :, None]
    freq = _rope_lane_frequencies()
    mem_len = mem.shape[1]
    w_mem_all = jnp.transpose(w_mem_kv, (1, 0, 2)).reshape(d, depth * 2 * MEM_WIDTH).astype(BF16)
    mem_kv = _matmul(mem.reshape(bsz * mem_len, d), w_mem_all, BF16).reshape(bsz, mem_len, -1)

    w_gu_dense_bf16, w_down_dense_bf16 = w_gu_dense.astype(BF16), w_down_dense.astype(BF16)
    w_gu_moe_bf16, w_down_moe_bf16 = w_gu_moe.astype(BF16), w_down_moe.astype(BF16)
    shared = None
    grouped = None
    for layer in range(depth):
        g1, b1 = ln1_g[layer][None, :], ln1_b[layer][None, :]
        g2, b2 = ln2_g[layer][None, :], ln2_b[layer][None, :]
        wo = w_o[layer].astype(BF16)
        if layer < n_a:
            x = _mixer_a(x, w_in_a[layer].astype(BF16), conv_a[layer], mem_kv, layer, wo, g1, b1, alpha)
        else:
            wq = w_q_b[layer - n_a]
            q, mem_out = _query_b(x, pos, freq, wq.astype(BF16),
                                  _rotate_half_columns(wq[:, :MIX_WIDTH]).astype(BF16), mem_kv, layer)
            main = _moba(q, *shared)
            x = _mixer_out(x, main, mem_out, wo, g1, b1, alpha)
        xf = x.reshape(bsz * seq, d)
        if layer % 2 == 0:
            xf = _ffn(xf, w_gu_dense_bf16, w_down_dense_bf16, layer // 2, g2, b2, alpha)
        else:
            xf, grouped = _moe(xf, w_router[layer // 2], w_gu_moe_bf16, w_down_moe_bf16, layer // 2,
                               g2, b2, alpha, grouped)
        x = xf.reshape(bsz, seq, d)
        if layer == n_a - 1:
            wk = w_kv_shared[:, :MIX_WIDTH]
            k, v, km = _shared_kv(x, pos, freq, wk.astype(BF16), _rotate_half_columns(wk).astype(BF16),
                                  w_kv_shared[:, MIX_WIDTH:].astype(BF16))
            n_blocks = seq // MOBA_BLOCK
            km = jnp.pad(km.reshape(bsz, n_blocks, MIX_WIDTH), ((0, 0), (0, LANES - n_blocks), (0, 0)))
            shared = (k, v, km)
    return x
```

```python
import functools

import jax
import jax.numpy as jnp
from jax import lax
from jax.experimental import pallas as pl
from jax.experimental.pallas import tpu as pltpu

F32 = jnp.float32
BF16 = jnp.bfloat16

HEAD_DIM = 64
MIX_HEADS = 12
MIX_WIDTH = MIX_HEADS * HEAD_DIM
MEM_HEADS = 4
MEM_WIDTH = MEM_HEADS * HEAD_DIM
CONV_WIDTH = 3
ROPE_DIM = HEAD_DIM // 4
ROPE_THETA = 500000.0
MOBA_BLOCK = 256
MOBA_TOPK = 3
TOP_K = 2
LN_EPS = 1e-5
ATTN_SCALE = HEAD_DIM ** -0.5
LOG2_E = 1.4426950408889634

LANES = 128
HEADS_PER_LANE_GROUP = LANES // HEAD_DIM
MASK_VALUE = -1e30
VMEM_LIMIT_BYTES = 56 * 1024 * 1024

SEQ_TILE = 512
FFN_TILE = 1024
FF_CHUNK = 1408
MOBA_CHUNK_BLOCKS = 4
ROUTE_TILE = 1024
EXPERT_TILE = 512
COMBINE_TILE = 1024
DMA_LOOP_UNROLL = 8


def _params(semantics):
    return pltpu.CompilerParams(dimension_semantics=semantics, vmem_limit_bytes=VMEM_LIMIT_BYTES)


def _layer_norm(r, g, b):
    mu = jnp.mean(r, axis=-1, keepdims=True)
    c = r - mu
    var = jnp.mean(c * c, axis=-1, keepdims=True)
    return c * lax.rsqrt(var + LN_EPS) * g + b


def _dot(a, b):
    return jnp.dot(a, b, preferred_element_type=F32)


def _dot_nt(a, b):
    return lax.dot_general(a, b, (((1,), (1,)), ((), ())), preferred_element_type=F32)


def _head_lane_mask(width, head):
    lane = lax.broadcasted_iota(jnp.int32, (1, width), 1)
    return (lane >= head * HEAD_DIM) & (lane < (head + 1) * HEAD_DIM)


def _mem_attention(q_mem, mem_k, mem_v):
    out = jnp.zeros(q_mem.shape, F32)
    for head in range(MEM_HEADS):
        hm = _head_lane_mask(MEM_WIDTH, head)
        qh = jnp.where(hm, q_mem * ATTN_SCALE, 0.0).astype(BF16)
        s = _dot_nt(qh, mem_k)
        m = jnp.max(s, axis=-1, keepdims=True)
        p = jnp.exp(s - m)
        l = jnp.sum(p, axis=-1, keepdims=True)
        o = _dot(p.astype(BF16), mem_v) / l
        out = jnp.where(hm, o, out)
    return out


def _rope_tables(pos, freq, reps):
    angle = pos * freq
    cos = jnp.cos(angle)
    sin = jnp.sin(angle)
    return (jnp.concatenate([cos] * reps, axis=1), jnp.concatenate([sin] * reps, axis=1))


def _matmul_kernel(x_ref, w_ref, o_ref):
    o_ref[...] = _dot(x_ref[...].astype(BF16), w_ref[...]).astype(o_ref.dtype)


def _matmul(x, w, out_dtype, tile_m=256):
    m, k = x.shape
    n = w.shape[1]
    return pl.pallas_call(
        _matmul_kernel,
        grid=(m // tile_m,),
        in_specs=[pl.BlockSpec((tile_m, k), lambda i: (i, 0)),
                  pl.BlockSpec((k, n), lambda i: (0, 0))],
        out_specs=pl.BlockSpec((tile_m, n), lambda i: (i, 0)),
        out_shape=jax.ShapeDtypeStruct((m, n), out_dtype),
        compiler_params=_params(("arbitrary",)),
        name="mem_kv_proj",
    )(x, w)


def _mixer_a_kernel(x_ref, win_ref, conv_ref, mk_ref, mv_ref, wo_ref, g_ref, b_ref, o_ref,
                    ztail_ref, *, alpha):
    tile = x_ref.shape[1]

    @pl.when(pl.program_id(1) == 0)
    def _():
        ztail_ref[...] = jnp.zeros_like(ztail_ref)

    x = x_ref[0]
    proj = _dot(x.astype(BF16), win_ref[...])
    u_b = proj[:, :MIX_WIDTH]
    u_c = proj[:, MIX_WIDTH:2 * MIX_WIDTH]
    u_h = proj[:, 2 * MIX_WIDTH:3 * MIX_WIDTH]
    q_mem = proj[:, 3 * MIX_WIDTH:]

    z = u_c * u_h
    tail = ztail_ref[...]
    ztail_ref[...] = z[tile - 8:, :]
    row = lax.broadcasted_iota(jnp.int32, z.shape, 0)
    z1 = jnp.where(row == 0, tail[7:8, :], pltpu.roll(z, 1, 0))
    z2 = jnp.where(row == 0, tail[6:7, :], jnp.where(row == 1, tail[7:8, :], pltpu.roll(z, 2, 0)))
    w = conv_ref[...]
    main = u_b * (w[0:1, :] * z2 + w[1:2, :] * z1 + w[2:3, :] * z)

    mem_out = _mem_attention(q_mem, mk_ref[0], mv_ref[0])
    mix = (_dot(main.astype(BF16), wo_ref[:MIX_WIDTH, :])
           + _dot(mem_out.astype(BF16), wo_ref[MIX_WIDTH:, :]))
    o_ref[0] = _layer_norm(alpha * x + mix, g_ref[...], b_ref[...])


def _mixer_a(x, w_in, conv_w, mem_kv, layer, w_o, g, b, alpha):
    bsz, seq, d = x.shape
    tile = min(SEQ_TILE, seq)
    mem_len = mem_kv.shape[1]
    return pl.pallas_call(
        functools.partial(_mixer_a_kernel, alpha=alpha),
        grid=(bsz, seq // tile),
        in_specs=[
            pl.BlockSpec((1, tile, d), lambda bi, ti: (bi, ti, 0)),
            pl.BlockSpec(w_in.shape, lambda bi, ti: (0, 0)),
            pl.BlockSpec(conv_w.shape, lambda bi, ti: (0, 0)),
            pl.BlockSpec((1, mem_len, MEM_WIDTH), lambda bi, ti: (bi, 0, 2 * layer)),
            pl.BlockSpec((1, mem_len, MEM_WIDTH), lambda bi, ti: (bi, 0, 2 * layer + 1)),
            pl.BlockSpec(w_o.shape, lambda bi, ti: (0, 0)),
            pl.BlockSpec((1, d), lambda bi, ti: (0, 0)),
            pl.BlockSpec((1, d), lambda bi, ti: (0, 0)),
        ],
        out_specs=pl.BlockSpec((1, tile, d), lambda bi, ti: (bi, ti, 0)),
        out_shape=jax.ShapeDtypeStruct(x.shape, F32),
        scratch_shapes=[pltpu.VMEM((8, MIX_WIDTH), F32)],
        compiler_params=_params(("arbitrary", "arbitrary")),
        name="mixer_a",
    )(x, w_in, conv_w, mem_kv, mem_kv, w_o, g, b)


def _partial_rope(t, pos, freq):
    half = ROPE_DIM // 2
    cos, sin = _rope_tables(pos, freq, MIX_WIDTH // LANES)
    dim = lax.broadcasted_iota(jnp.int32, (1, MIX_WIDTH), 1) % HEAD_DIM
    upper = pltpu.roll(t, MIX_WIDTH - half, 1)
    lower = pltpu.roll(t, half, 1)
    return t * cos + jnp.where(dim < half, -upper, lower) * sin


def _shared_kv_kernel(x_ref, pos_ref, freq_ref, wk_ref, wv_ref, k_ref, v_ref, km_ref):
    xb = x_ref[0].astype(BF16)
    k = _partial_rope(_dot(xb, wk_ref[...]), pos_ref[0], freq_ref[...])
    k_ref[0] = k.astype(BF16)
    v_ref[0] = _dot(xb, wv_ref[...]).astype(BF16)
    km_ref[0, 0] = jnp.mean(k, axis=0, keepdims=True)


def _shared_kv(x, pos, freq, w_k, w_v):
    bsz, seq, d = x.shape
    n_blocks = seq // MOBA_BLOCK
    full = lambda bi, ti: (0, 0)
    return pl.pallas_call(
        _shared_kv_kernel,
        grid=(bsz, n_blocks),
        in_specs=[
            pl.BlockSpec((1, MOBA_BLOCK, d), lambda bi, ti: (bi, ti, 0)),
            pl.BlockSpec((1, MOBA_BLOCK, 1), lambda bi, ti: (bi, ti, 0)),
            pl.BlockSpec(freq.shape, full),
            pl.BlockSpec(w_k.shape, full),
            pl.BlockSpec(w_v.shape, full),
        ],
        out_specs=[
            pl.BlockSpec((1, MOBA_BLOCK, MIX_WIDTH), lambda bi, ti: (bi, ti, 0)),
            pl.BlockSpec((1, MOBA_BLOCK, MIX_WIDTH), lambda bi, ti: (bi, ti, 0)),
            pl.BlockSpec((1, 1, 1, MIX_WIDTH), lambda bi, ti: (bi, ti, 0, 0)),
        ],
        out_shape=[
            jax.ShapeDtypeStruct((bsz, seq, MIX_WIDTH), BF16),
            jax.ShapeDtypeStruct((bsz, seq, MIX_WIDTH), BF16),
            jax.ShapeDtypeStruct((bsz, n_blocks, 1, MIX_WIDTH), F32),
        ],
        compiler_params=_params(("arbitrary", "arbitrary")),
        name="shared_kv",
    )(x, pos, freq, w_k, w_v)


def _query_b_kernel(x_ref, pos_ref, freq_ref, wq_ref, mk_ref, mv_ref, q_ref, mo_ref):
    xb = x_ref[0].astype(BF16)
    proj = _dot(xb, wq_ref[...])
    q = _partial_rope(proj[:, :MIX_WIDTH], pos_ref[0], freq_ref[...])
    q_ref[0] = (q * (ATTN_SCALE * LOG2_E)).astype(BF16)
    mo_ref[0] = _mem_attention(proj[:, MIX_WIDTH:], mk_ref[0], mv_ref[0]).astype(BF16)


def _query_b(x, pos, freq, w_q, mem_kv, layer):
    bsz, seq, d = x.shape
    tile = min(SEQ_TILE, seq)
    mem_len = mem_kv.shape[1]
    full = lambda bi, ti: (0, 0)
    return pl.pallas_call(
        _query_b_kernel,
        grid=(bsz, seq // tile),
        in_specs=[
            pl.BlockSpec((1, tile, d), lambda bi, ti: (bi, ti, 0)),
            pl.BlockSpec((1, tile, 1), lambda bi, ti: (bi, ti, 0)),
            pl.BlockSpec(freq.shape, full),
            pl.BlockSpec(w_q.shape, full),
            pl.BlockSpec((1, mem_len, MEM_WIDTH), lambda bi, ti: (bi, 0, 2 * layer)),
            pl.BlockSpec((1, mem_len, MEM_WIDTH), lambda bi, ti: (bi, 0, 2 * layer + 1)),
        ],
        out_specs=[
            pl.BlockSpec((1, tile, MIX_WIDTH), lambda bi, ti: (bi, ti, 0)),
            pl.BlockSpec((1, tile, MEM_WIDTH), lambda bi, ti: (bi, ti, 0)),
        ],
        out_shape=[
            jax.ShapeDtypeStruct((bsz, seq, MIX_WIDTH), BF16),
            jax.ShapeDtypeStruct((bsz, seq, MEM_WIDTH), BF16),
        ],
        compiler_params=_params(("arbitrary", "arbitrary")),
        name="query_b",
    )(x, pos, freq, w_q, mem_kv, mem_kv)


def _moba_kernel(q_ref, k_ref, v_ref, km_ref, o_ref, kaug_ref, vaug_ref, m_ref, acc_ref, s0_ref,
                 s1_ref, *, chunk_blocks):
    blk = MOBA_BLOCK
    span = chunk_blocks * blk
    pair = pl.program_id(2)
    chains = [(u, h) for u in range(chunk_blocks) for h in range(HEADS_PER_LANE_GROUP)]
    rows = len(chains) * blk

    @pl.when(pair == 0)
    def _():
        kaug_ref[:, :LANES] = k_ref[0]
        vaug_ref[:, :LANES] = v_ref[0]
        lane = lax.broadcasted_iota(jnp.int32, (blk, LANES), 1)
        ones_lane = (lane == 0).astype(BF16)

        def fill(j, carry):
            off = pl.multiple_of(j * blk, blk)
            kaug_ref[pl.ds(off, blk), LANES:] = (lane == j).astype(BF16)
            vaug_ref[pl.ds(off, blk), LANES:] = ones_lane
            return carry

        lax.fori_loop(0, k_ref.shape[1] // blk, fill, 0)

    def chunk_rows(c):
        return pl.ds(pl.multiple_of(c * span, span), span)

    def sweep(half, sa_ref, sb_ref):
        own_chunk = 2 * pair + half
        q = q_ref[0, half * span:(half + 1) * span, :]
        q_heads = jnp.concatenate(
            [jnp.where(_head_lane_mask(LANES, h), q[u * blk:(u + 1) * blk], 0).astype(BF16)
             for u, h in chains], axis=0)

        block_id = lax.broadcasted_iota(jnp.int32, (rows, LANES), 1)
        own = (own_chunk * chunk_blocks
               + lax.broadcasted_iota(jnp.int32, (rows, LANES), 0) // (HEADS_PER_LANE_GROUP * blk))
        is_past = block_id < own
        gate = jnp.where(is_past, _dot_nt(q_heads, km_ref[0].astype(BF16)), -jnp.inf)
        kth = gate
        for _ in range(MOBA_TOPK - 1):
            kth = jnp.where(kth == jnp.max(kth, axis=-1, keepdims=True), -jnp.inf, kth)
        threshold = jnp.max(kth, axis=-1, keepdims=True)
        allowed = (is_past & (gate >= threshold)) | (block_id == own)
        block_bias = jnp.where(allowed, 0.0, MASK_VALUE).astype(BF16)
        q_aug = jnp.concatenate([q_heads, block_bias], axis=1)

        state_rows = [slice(half * rows + i * blk, half * rows + (i + 1) * blk)
                      for i in range(len(chains))]
        chain_rows = [slice(i * blk, (i + 1) * blk) for i in range(len(chains))]
        for r in state_rows:
            m_ref[r, :] = jnp.full((blk, 1), MASK_VALUE, F32)
            acc_ref[r, :] = jnp.zeros((blk, acc_ref.shape[1]), F32)

        def score_own(s_ref):
            for r, (u, _) in zip(chain_rows, chains):
                width = (u + 1) * blk
                k_part = kaug_ref[pl.ds(pl.multiple_of(own_chunk * span, span), width), :]
                k_pos = lax.broadcasted_iota(jnp.int32, (blk, width), 1)
                q_pos = u * blk + lax.broadcasted_iota(jnp.int32, (blk, width), 0)
                s_ref[r, :width] = jnp.where(k_pos <= q_pos, _dot_nt(q_aug[r], k_part), MASK_VALUE)
                if width < span:
                    s_ref[r, width:] = jnp.full((blk, span - width), MASK_VALUE, F32)

        def score_past(c, s_ref):
            k_chunk = kaug_ref[chunk_rows(c), :]
            for r in chain_rows:
                s_ref[r, :] = _dot_nt(q_aug[r], k_chunk)

        def softmax_value(c, s_ref):
            v_chunk = vaug_ref[chunk_rows(c), :]
            for r, sr in zip(chain_rows, state_rows):
                s = s_ref[r, :]
                m_old = m_ref[sr, :]
                m_new = jnp.maximum(m_old, jnp.max(s, axis=-1, keepdims=True))
                p = jnp.exp2((s - m_new).astype(BF16))
                acc_ref[sr, :] = jnp.exp2(m_old - m_new) * acc_ref[sr, :] + _dot(p, v_chunk)
                m_ref[sr, :] = m_new

        def step_chunk(j):
            return jnp.where(j == 0, own_chunk, j - 1)

        n_past = own_chunk
        score_own(sa_ref)

        def two_steps(i, carry):
            j = 2 * i + 1
            score_past(j - 1, sb_ref)
            softmax_value(step_chunk(j - 1), sa_ref)
            score_past(j, sa_ref)
            softmax_value(j - 1, sb_ref)
            return carry

        lax.fori_loop(0, pair, two_steps, 0)
        if half == 1:
            score_past(n_past - 1, sb_ref)
            softmax_value(step_chunk(n_past - 1), sa_ref)
            softmax_value(n_past - 1, sb_ref)
        else:
            softmax_value(step_chunk(n_past), sa_ref)

        for u in range(chunk_blocks):
            out = None
            for h in range(HEADS_PER_LANE_GROUP):
                acc = acc_ref[state_rows[u * HEADS_PER_LANE_GROUP + h], :]
                o = acc[:, :LANES] / acc[:, LANES:LANES + 1]
                out = o if out is None else jnp.where(_head_lane_mask(LANES, h), o, out)
            o_ref[0, half * span + u * blk:half * span + (u + 1) * blk, :] = out.astype(o_ref.dtype)

    sweep(0, s0_ref, s1_ref)
    sweep(1, s1_ref, s0_ref)


def _moba(q, k, v, k_means):
    bsz, seq, _ = q.shape
    n_blocks = seq // MOBA_BLOCK
    groups = MIX_WIDTH // LANES
    chunk_blocks = MOBA_CHUNK_BLOCKS
    while n_blocks % (2 * chunk_blocks):
        chunk_blocks //= 2
    assert chunk_blocks >= 1, "the sequence must hold an even number of MoBA blocks"
    span = chunk_blocks * MOBA_BLOCK
    rows = HEADS_PER_LANE_GROUP * span
    return pl.pallas_call(
        functools.partial(_moba_kernel, chunk_blocks=chunk_blocks),
        grid=(bsz, groups, seq // (2 * span)),
        in_specs=[
            pl.BlockSpec((1, 2 * span, LANES), lambda bi, gi, qi: (bi, qi, gi)),
            pl.BlockSpec((1, seq, LANES), lambda bi, gi, qi: (bi, 0, gi)),
            pl.BlockSpec((1, seq, LANES), lambda bi, gi, qi: (bi, 0, gi)),
            pl.BlockSpec((1, LANES, LANES), lambda bi, gi, qi: (bi, 0, gi)),
        ],
        out_specs=pl.BlockSpec((1, 2 * span, LANES), lambda bi, gi, qi: (bi, qi, gi)),
        out_shape=jax.ShapeDtypeStruct(q.shape, BF16),
        scratch_shapes=[pltpu.VMEM((seq, 2 * LANES), BF16), pltpu.VMEM((seq, 2 * LANES), BF16),
                        pltpu.VMEM((2 * rows, 1), F32), pltpu.VMEM((2 * rows, 2 * LANES), F32),
                        pltpu.VMEM((rows, span), F32), pltpu.VMEM((rows, span), F32)],
        compiler_params=_params(("arbitrary", "arbitrary", "arbitrary")),
        name="moba",
    )(q, k, v, k_means)


def _mixer_out_kernel(x_ref, main_ref, mo_ref, wo_ref, g_ref, b_ref, o_ref, *, alpha):
    mix = _dot(main_ref[0], wo_ref[:MIX_WIDTH, :]) + _dot(mo_ref[0], wo_ref[MIX_WIDTH:, :])
    o_ref[0] = _layer_norm(alpha * x_ref[0] + mix, g_ref[...], b_ref[...])


def _mixer_out(x, main, mem_out, w_o, g, b, alpha):
    bsz, seq, d = x.shape
    tile = min(SEQ_TILE, seq)
    full = lambda bi, ti: (0, 0)
    return pl.pallas_call(
        functools.partial(_mixer_out_kernel, alpha=alpha),
        grid=(bsz, seq // tile),
        in_specs=[
            pl.BlockSpec((1, tile, d), lambda bi, ti: (bi, ti, 0)),
            pl.BlockSpec((1, tile, MIX_WIDTH), lambda bi, ti: (bi, ti, 0)),
            pl.BlockSpec((1, tile, MEM_WIDTH), lambda bi, ti: (bi, ti, 0)),
            pl.BlockSpec(w_o.shape, full),
            pl.BlockSpec((1, d), full),
            pl.BlockSpec((1, d), full),
        ],
        out_specs=pl.BlockSpec((1, tile, d), lambda bi, ti: (bi, ti, 0)),
        out_shape=jax.ShapeDtypeStruct(x.shape, F32),
        compiler_params=_params(("arbitrary", "arbitrary")),
        name="mixer_out",
    )(x, main, mem_out, w_o, g, b)


def _swiglu_chunk(xb, wg, wu, wd):
    gp = _dot(xb, wg)
    up = _dot(xb, wu)
    return _dot((gp * jax.nn.sigmoid(gp) * up).astype(BF16), wd)


def _ffn_chunks(d_ff):
    chunk = FF_CHUNK if d_ff % FF_CHUNK == 0 else d_ff
    return chunk, d_ff // chunk


def _accumulate_chunks(c, n_chunks, part, acc_ref, finish):
    if n_chunks == 1:
        finish(part())
        return

    @pl.when(c == 0)
    def _():
        acc_ref[...] = part()

    if n_chunks > 2:
        @pl.when((c > 0) & (c < n_chunks - 1))
        def _():
            acc_ref[...] += part()

    @pl.when(c == n_chunks - 1)
    def _():
        finish(acc_ref[...] + part())


def _ffn_kernel(x_ref, wg_ref, wu_ref, wd_ref, g_ref, b_ref, o_ref, acc_ref, xb_ref, *, alpha,
                n_chunks):
    c = pl.program_id(1)

    @pl.when(c == 0)
    def _():
        xb_ref[...] = x_ref[...].astype(BF16)

    def finish(ffn):
        o_ref[...] = _layer_norm(alpha * x_ref[...] + ffn, g_ref[...], b_ref[...])

    def part():
        return _swiglu_chunk(xb_ref[...], wg_ref[...], wu_ref[...], wd_ref[...])

    _accumulate_chunks(c, n_chunks, part, acc_ref, finish)


def _ffn(x, w_gu, w_down, layer, g, b, alpha):
    n, d = x.shape
    d_ff = w_down.shape[1]
    tile = min(FFN_TILE, n)
    chunk, n_chunks = _ffn_chunks(d_ff)
    return pl.pallas_call(
        functools.partial(_ffn_kernel, alpha=alpha, n_chunks=n_chunks),
        grid=(n // tile, n_chunks),
        in_specs=[
            pl.BlockSpec((tile, d), lambda i, c: (i, 0)),
            pl.BlockSpec((None, d, chunk), lambda i, c: (layer, 0, c)),
            pl.BlockSpec((None, d, chunk), lambda i, c: (layer, 0, c + n_chunks)),
            pl.BlockSpec((None, chunk, d), lambda i, c: (layer, c, 0)),
            pl.BlockSpec((1, d), lambda i, c: (0, 0)),
            pl.BlockSpec((1, d), lambda i, c: (0, 0)),
        ],
        out_specs=pl.BlockSpec((tile, d), lambda i, c: (i, 0)),
        out_shape=jax.ShapeDtypeStruct(x.shape, F32),
        scratch_shapes=[pltpu.VMEM((tile, d), F32), pltpu.VMEM((tile, d), BF16)],
        compiler_params=_params(("arbitrary", "arbitrary")),
        name="ffn_dense",
    )(x, w_gu, w_gu, w_down, g, b)


META_EXPERT, META_RANK, META_GATE = 0, 2, 4


def _split_bf16(a):
    hi = a.astype(BF16)
    return hi, (a - hi.astype(F32)).astype(BF16)


def _route_kernel(x_ref, wr_ref, meta_ref, counts_ref, *, n_experts):
    tile = x_ref.shape[0]

    @pl.when(pl.program_id(0) == 0)
    def _():
        counts_ref[...] = jnp.zeros_like(counts_ref)

    lane = lax.broadcasted_iota(jnp.int32, (tile, LANES), 1)
    x_hi, x_lo = _split_bf16(x_ref[...])
    w_hi, w_lo = _split_bf16(wr_ref[...])
    logits = _dot(x_hi, w_hi) + (_dot(x_hi, w_lo) + _dot(x_lo, w_hi))
    logits = jnp.where(lane < n_experts, logits, -jnp.inf)
    top1 = jnp.max(logits, axis=-1, keepdims=True)
    idx1 = jnp.min(jnp.where(logits == top1, lane, LANES), axis=-1, keepdims=True)
    rest = jnp.where(lane == idx1, -jnp.inf, logits)
    top2 = jnp.max(rest, axis=-1, keepdims=True)
    idx2 = jnp.min(jnp.where(rest == top2, lane, LANES), axis=-1, keepdims=True)
    e2 = jnp.exp(top2 - top1)

    chosen = (lane == idx1) | (lane == idx2)
    earlier = (lax.broadcasted_iota(jnp.int32, (tile, tile), 1)
               < lax.broadcasted_iota(jnp.int32, (tile, tile), 0))
    rank = counts_ref[...] + _dot(earlier.astype(BF16), chosen.astype(BF16))
    rank1 = jnp.sum(jnp.where(lane == idx1, rank, 0.0), axis=-1, keepdims=True)
    rank2 = jnp.sum(jnp.where(lane == idx2, rank, 0.0), axis=-1, keepdims=True)
    counts_ref[...] += jnp.sum(chosen.astype(F32), axis=0, keepdims=True)

    fields = [idx1.astype(F32), idx2.astype(F32), rank1, rank2, 1.0 / (1.0 + e2), e2 / (1.0 + e2)]
    meta = jnp.zeros((tile, LANES), F32)
    for i, field in enumerate(fields):
        meta = jnp.where(lane == i, field, meta)
    meta_ref[...] = meta


def _route(x, w_router):
    n, d = x.shape
    tile = min(ROUTE_TILE, n)
    return pl.pallas_call(
        functools.partial(_route_kernel, n_experts=w_router.shape[1]),
        grid=(n // tile,),
        in_specs=[pl.BlockSpec((tile, d), lambda i: (i, 0)),
                  pl.BlockSpec((d, LANES), lambda i: (0, 0))],
        out_specs=[pl.BlockSpec((tile, LANES), lambda i: (i, 0)),
                   pl.BlockSpec((1, LANES), lambda i: (0, 0))],
        out_shape=[jax.ShapeDtypeStruct((n, LANES), F32), jax.ShapeDtypeStruct((1, LANES), F32)],
        compiler_params=_params(("arbitrary",)),
        name="moe_route",
    )(x, jnp.pad(w_router, ((0, 0), (0, LANES - w_router.shape[1]))))


def _row_copy(src_ref, src_row, dst_ref, dst_row, sem):
    return pltpu.make_async_copy(src_ref.at[pl.ds(src_row, 1), :], dst_ref.at[pl.ds(dst_row, 1), :], sem)


def _start_then_wait(copies, n):
    def start(r, carry):
        for cp in copies(r):
            cp.start()
        return carry

    def wait(r, carry):
        for cp in copies(r):
            cp.wait()
        return carry

    lax.fori_loop(0, n, start, 0, unroll=DMA_LOOP_UNROLL)
    lax.fori_loop(0, n, wait, 0, unroll=DMA_LOOP_UNROLL)


def _dispatch_kernel(pos_ref, x_ref, init_ref, xs_ref, sem):
    del init_ref

    def copies(r):
        return [_row_copy(x_ref, r, xs_ref, pos_ref[0, 0, TOP_K * r + k], sem) for k in range(TOP_K)]

    _start_then_wait(copies, x_ref.shape[0])


def _dispatch(x, pos, grouped_init):
    n, d = x.shape
    tile = min(ROUTE_TILE, n)
    any_spec = pl.BlockSpec(memory_space=pl.ANY)
    return pl.pallas_call(
        _dispatch_kernel,
        grid=(n // tile,),
        in_specs=[pl.BlockSpec((1, 1, TOP_K * tile), lambda i: (i, 0, 0), memory_space=pltpu.SMEM),
                  pl.BlockSpec((tile, d), lambda i: (i, 0)), any_spec],
        out_specs=any_spec,
        out_shape=jax.ShapeDtypeStruct(grouped_init.shape, F32),
        scratch_shapes=[pltpu.SemaphoreType.DMA],
        input_output_aliases={2: 0},
        compiler_params=_params(("arbitrary",)),
        name="moe_dispatch",
    )(pos.reshape(n // tile, 1, TOP_K * tile), x, grouped_init)


def _expert_ffn_kernel(tile_expert_ref, n_tiles_ref, xs_ref, wg_ref, wu_ref, wd_ref, o_ref,
                       acc_ref, xb_ref, *, n_chunks):
    del tile_expert_ref
    c = pl.program_id(1)

    @pl.when(pl.program_id(0) < n_tiles_ref[0])
    def _():
        @pl.when(c == 0)
        def _():
            xb_ref[...] = xs_ref[...].astype(BF16)

        def finish(ffn):
            o_ref[...] = ffn

        def part():
            return _swiglu_chunk(xb_ref[...], wg_ref[0], wu_ref[0], wd_ref[0])

        _accumulate_chunks(c, n_chunks, part, acc_ref, finish)


def _expert_ffn(xs, tile_expert, n_tiles, w_gu, w_down, layer):
    n_rows, d = xs.shape
    d_ff = w_down.shape[2]
    chunk, n_chunks = _ffn_chunks(d_ff)
    row_tile = lambda t, c, te, nt: (jnp.minimum(t, nt[0] - 1), 0)
    grid_spec = pltpu.PrefetchScalarGridSpec(
        num_scalar_prefetch=2,
        grid=(n_rows // EXPERT_TILE, n_chunks),
        in_specs=[
            pl.BlockSpec((EXPERT_TILE, d), row_tile),
            pl.BlockSpec((None, 1, d, chunk), lambda t, c, te, nt: (layer, te[t], 0, c)),
            pl.BlockSpec((None, 1, d, chunk), lambda t, c, te, nt: (layer, te[t], 0, c + n_chunks)),
            pl.BlockSpec((None, 1, chunk, d), lambda t, c, te, nt: (layer, te[t], c, 0)),
        ],
        out_specs=pl.BlockSpec((EXPERT_TILE, d), row_tile),
        scratch_shapes=[pltpu.VMEM((EXPERT_TILE, d), F32), pltpu.VMEM((EXPERT_TILE, d), BF16)],
    )
    return pl.pallas_call(
        functools.partial(_expert_ffn_kernel, n_chunks=n_chunks),
        grid_spec=grid_spec,
        out_shape=jax.ShapeDtypeStruct((n_rows, d), F32),
        compiler_params=_params(("arbitrary", "arbitrary")),
        name="moe_experts",
    )(tile_expert, n_tiles, xs, w_gu, w_gu, w_down)


def _combine_kernel(pos_ref, meta_ref, x_ref, g_ref, b_ref, ys_ref, o_ref, rows_ref, sem, *, alpha):
    tile = x_ref.shape[0]

    def copies(r):
        return [_row_copy(ys_ref, pos_ref[0, 0, TOP_K * r + k], rows_ref.at[k], r, sem)
                for k in range(TOP_K)]

    _start_then_wait(copies, tile)

    meta = meta_ref[...]
    lane = lax.broadcasted_iota(jnp.int32, meta.shape, 1)
    ffn = jnp.zeros(x_ref.shape, F32)
    for k in range(TOP_K):
        gate = jnp.sum(jnp.where(lane == META_GATE + k, meta, 0.0), axis=-1, keepdims=True)
        ffn = ffn + gate * rows_ref[k]
    o_ref[...] = _layer_norm(alpha * x_ref[...] + ffn, g_ref[...], b_ref[...])


def _combine(x, meta, pos, ys, g, b, alpha):
    n, d = x.shape
    tile = min(COMBINE_TILE, n)
    return pl.pallas_call(
        functools.partial(_combine_kernel, alpha=alpha),
        grid=(n // tile,),
        in_specs=[pl.BlockSpec((1, 1, TOP_K * tile), lambda i: (i, 0, 0), memory_space=pltpu.SMEM),
                  pl.BlockSpec((tile, LANES), lambda i: (i, 0)),
                  pl.BlockSpec((tile, d), lambda i: (i, 0)),
                  pl.BlockSpec((1, d), lambda i: (0, 0)),
                  pl.BlockSpec((1, d), lambda i: (0, 0)),
                  pl.BlockSpec(memory_space=pl.ANY)],
        out_specs=pl.BlockSpec((tile, d), lambda i: (i, 0)),
        out_shape=jax.ShapeDtypeStruct(x.shape, F32),
        scratch_shapes=[pltpu.VMEM((TOP_K, tile, d), F32), pltpu.SemaphoreType.DMA],
        compiler_params=_params(("arbitrary",)),
        name="moe_combine",
    )(pos.reshape(n // tile, 1, TOP_K * tile), meta, x, g, b, ys)


def _moe(x, w_router, w_gu, w_down, layer, g, b, alpha, grouped_init):
    n, d = x.shape
    n_experts = w_router.shape[1]
    meta, counts = _route(x, w_router)

    counts = counts[0, :n_experts].astype(jnp.int32)
    group_tiles = (counts + EXPERT_TILE - 1) // EXPERT_TILE
    tiles_end = jnp.cumsum(group_tiles)
    group_row0 = (tiles_end - group_tiles) * EXPERT_TILE
    n_grid_tiles = (TOP_K * n) // EXPERT_TILE + n_experts
    tile_ids = jnp.arange(n_grid_tiles, dtype=jnp.int32)
    tile_expert = jnp.minimum(jnp.sum((tile_ids[:, None] >= tiles_end[None, :]).astype(jnp.int32), axis=1),
                              n_experts - 1)
    expert = meta[:, META_EXPERT:META_EXPERT + TOP_K].astype(jnp.int32)
    rank = meta[:, META_RANK:META_RANK + TOP_K].astype(jnp.int32)
    pos = (group_row0[expert] + rank).reshape(-1)

    if grouped_init is None:
        grouped_init = jnp.zeros((n_grid_tiles * EXPERT_TILE, d), F32)
    xs = _dispatch(x, pos, grouped_init)
    ys = _expert_ffn(xs, tile_expert, tiles_end[-1:].astype(jnp.int32), w_gu, w_down, layer)
    return _combine(x, meta, pos, ys, g, b, alpha), xs


def _rope_lane_frequencies():
    inv_freq = ROPE_THETA ** (-jnp.arange(0, ROPE_DIM, 2, dtype=F32) / ROPE_DIM)
    c = jnp.arange(LANES) % HEAD_DIM
    return jnp.where(c < ROPE_DIM, inv_freq[c % (ROPE_DIM // 2)], 0.0).astype(F32)[None, :]


def kernel(x, mem, positions, w_in_a, conv_a, w_q_b, w_kv_shared, w_mem_kv, w_o, ln1_g, ln1_b,
           ln2_g, ln2_b, w_gu_dense, w_down_dense, w_router, w_gu_moe, w_down_moe):
    bsz, seq, d = x.shape
    depth = w_o.shape[0]
    n_a = w_in_a.shape[0]
    alpha = float((2 * depth) ** 0.25)
    assert seq % MOBA_BLOCK == 0 and seq // MOBA_BLOCK <= LANES

    pos = positions.astype(F32)[:, :, None]
    freq = _rope_lane_frequencies()
    mem_len = mem.shape[1]
    w_mem_all = jnp.transpose(w_mem_kv, (1, 0, 2)).reshape(d, depth * 2 * MEM_WIDTH).astype(BF16)
    mem_kv = _matmul(mem.reshape(bsz * mem_len, d), w_mem_all, BF16).reshape(bsz, mem_len, -1)

    w_gu_dense_bf16, w_down_dense_bf16 = w_gu_dense.astype(BF16), w_down_dense.astype(BF16)
    w_gu_moe_bf16, w_down_moe_bf16 = w_gu_moe.astype(BF16), w_down_moe.astype(BF16)
    shared = None
    grouped = None
    for layer in range(depth):
        g1, b1 = ln1_g[layer][None, :], ln1_b[layer][None, :]
        g2, b2 = ln2_g[layer][None, :], ln2_b[layer][None, :]
        wo = w_o[layer].astype(BF16)
        if layer < n_a:
            x = _mixer_a(x, w_in_a[layer].astype(BF16), conv_a[layer], mem_kv, layer, wo, g1, b1, alpha)
        else:
            wq = w_q_b[layer - n_a]
            q, mem_out = _query_b(x, pos, freq, wq.astype(BF16), mem_kv, layer)
            main = _moba(q, *shared)
            x = _mixer_out(x, main, mem_out, wo, g1, b1, alpha)
        xf = x.reshape(bsz * seq, d)
        if layer % 2 == 0:
            xf = _ffn(xf, w_gu_dense_bf16, w_down_dense_bf16, layer // 2, g2, b2, alpha)
        else:
            xf, grouped = _moe(xf, w_router[layer // 2], w_gu_moe_bf16, w_down_moe_bf16, layer // 2,
                               g2, b2, alpha, grouped)
        x = xf.reshape(bsz, seq, d)
        if layer == n_a - 1:
            wk = w_kv_shared[:, :MIX_WIDTH]
            k, v, km = _shared_kv(x, pos, freq, wk.astype(BF16), w_kv_shared[:, MIX_WIDTH:].astype(BF16))
            n_blocks = seq // MOBA_BLOCK
            km = jnp.pad(km.reshape(bsz, n_blocks, MIX_WIDTH), ((0, 0), (0, LANES - n_blocks), (0, 0)))
            shared = (k, v, km)
    return x
```

```python
import functools

import jax
import jax.numpy as jnp
from jax import lax
from jax.experimental import pallas as pl
from jax.experimental.pallas import tpu as pltpu

F32 = jnp.float32
BF16 = jnp.bfloat16

HEAD_DIM = 64
MIX_HEADS = 12
MIX_WIDTH = MIX_HEADS * HEAD_DIM
MEM_HEADS = 4
MEM_WIDTH = MEM_HEADS * HEAD_DIM
CONV_WIDTH = 3
ROPE_DIM = HEAD_DIM // 4
ROPE_THETA = 500000.0
MOBA_BLOCK = 256
MOBA_TOPK = 3
TOP_K = 2
LN_EPS = 1e-5
ATTN_SCALE = HEAD_DIM ** -0.5
LOG2_E = 1.4426950408889634

LANES = 128
HEADS_PER_LANE_GROUP = LANES // HEAD_DIM
MASK_VALUE = -1e30
VMEM_LIMIT_BYTES = 56 * 1024 * 1024

SEQ_TILE = 512
FFN_TILE = 1024
FF_CHUNK = 1408
MOBA_CHUNK_BLOCKS = 4
ROUTE_TILE = 1024
EXPERT_TILE = 512
COMBINE_TILE = 1024
DMA_LOOP_UNROLL = 8


def _params(semantics):
    return pltpu.CompilerParams(dimension_semantics=semantics, vmem_limit_bytes=VMEM_LIMIT_BYTES)


def _layer_norm(r, g, b):
    mu = jnp.mean(r, axis=-1, keepdims=True)
    c = r - mu
    var = jnp.mean(c * c, axis=-1, keepdims=True)
    return c * lax.rsqrt(var + LN_EPS) * g + b


def _dot(a, b):
    return jnp.dot(a, b, preferred_element_type=F32)


def _dot_nt(a, b):
    return lax.dot_general(a, b, (((1,), (1,)), ((), ())), preferred_element_type=F32)


def _head_lane_mask(width, head):
    lane = lax.broadcasted_iota(jnp.int32, (1, width), 1)
    return (lane >= head * HEAD_DIM) & (lane < (head + 1) * HEAD_DIM)


def _mem_attention(q_mem, mem_k, mem_v):
    out = jnp.zeros(q_mem.shape, F32)
    for head in range(MEM_HEADS):
        hm = _head_lane_mask(MEM_WIDTH, head)
        qh = jnp.where(hm, q_mem * ATTN_SCALE, 0.0).astype(BF16)
        s = _dot_nt(qh, mem_k)
        m = jnp.max(s, axis=-1, keepdims=True)
        p = jnp.exp(s - m)
        l = jnp.sum(p, axis=-1, keepdims=True)
        o = _dot(p.astype(BF16), mem_v) / l
        out = jnp.where(hm, o, out)
    return out


def _rope_tables(pos, freq, reps):
    angle = pos * freq
    cos = jnp.cos(angle)
    sin = jnp.sin(angle)
    return (jnp.concatenate([cos] * reps, axis=1), jnp.concatenate([sin] * reps, axis=1))


def _matmul_kernel(x_ref, w_ref, o_ref):
    o_ref[...] = _dot(x_ref[...].astype(BF16), w_ref[...]).astype(o_ref.dtype)


def _matmul(x, w, out_dtype, tile_m=256):
    m, k = x.shape
    n = w.shape[1]
    return pl.pallas_call(
        _matmul_kernel,
        grid=(m // tile_m,),
        in_specs=[pl.BlockSpec((tile_m, k), lambda i: (i, 0)),
                  pl.BlockSpec((k, n), lambda i: (0, 0))],
        out_specs=pl.BlockSpec((tile_m, n), lambda i: (i, 0)),
        out_shape=jax.ShapeDtypeStruct((m, n), out_dtype),
        compiler_params=_params(("arbitrary",)),
        name="mem_kv_proj",
    )(x, w)


def _mixer_a_kernel(x_ref, win_ref, conv_ref, mk_ref, mv_ref, wo_ref, g_ref, b_ref, o_ref,
                    ztail_ref, *, alpha):
    tile = x_ref.shape[1]

    @pl.when(pl.program_id(1) == 0)
    def _():
        ztail_ref[...] = jnp.zeros_like(ztail_ref)

    x = x_ref[0]
    proj = _dot(x.astype(BF16), win_ref[...])
    u_b = proj[:, :MIX_WIDTH]
    u_c = proj[:, MIX_WIDTH:2 * MIX_WIDTH]
    u_h = proj[:, 2 * MIX_WIDTH:3 * MIX_WIDTH]
    q_mem = proj[:, 3 * MIX_WIDTH:]

    z = u_c * u_h
    tail = ztail_ref[...]
    ztail_ref[...] = z[tile - 8:, :]
    row = lax.broadcasted_iota(jnp.int32, z.shape, 0)
    z1 = jnp.where(row == 0, tail[7:8, :], pltpu.roll(z, 1, 0))
    z2 = jnp.where(row == 0, tail[6:7, :], jnp.where(row == 1, tail[7:8, :], pltpu.roll(z, 2, 0)))
    w = conv_ref[...]
    main = u_b * (w[0:1, :] * z2 + w[1:2, :] * z1 + w[2:3, :] * z)

    mem_out = _mem_attention(q_mem, mk_ref[0], mv_ref[0])
    mix = (_dot(main.astype(BF16), wo_ref[:MIX_WIDTH, :])
           + _dot(mem_out.astype(BF16), wo_ref[MIX_WIDTH:, :]))
    o_ref[0] = _layer_norm(alpha * x + mix, g_ref[...], b_ref[...])


def _mixer_a(x, w_in, conv_w, mem_kv, layer, w_o, g, b, alpha):
    bsz, seq, d = x.shape
    tile = min(SEQ_TILE, seq)
    mem_len = mem_kv.shape[1]
    return pl.pallas_call(
        functools.partial(_mixer_a_kernel, alpha=alpha),
        grid=(bsz, seq // tile),
        in_specs=[
            pl.BlockSpec((1, tile, d), lambda bi, ti: (bi, ti, 0)),
            pl.BlockSpec(w_in.shape, lambda bi, ti: (0, 0)),
            pl.BlockSpec(conv_w.shape, lambda bi, ti: (0, 0)),
            pl.BlockSpec((1, mem_len, MEM_WIDTH), lambda bi, ti: (bi, 0, 2 * layer)),
            pl.BlockSpec((1, mem_len, MEM_WIDTH), lambda bi, ti: (bi, 0, 2 * layer + 1)),
            pl.BlockSpec(w_o.shape, lambda bi, ti: (0, 0)),
            pl.BlockSpec((1, d), lambda bi, ti: (0, 0)),
            pl.BlockSpec((1, d), lambda bi, ti: (0, 0)),
        ],
        out_specs=pl.BlockSpec((1, tile, d), lambda bi, ti: (bi, ti, 0)),
        out_shape=jax.ShapeDtypeStruct(x.shape, F32),
        scratch_shapes=[pltpu.VMEM((8, MIX_WIDTH), F32)],
        compiler_params=_params(("arbitrary", "arbitrary")),
        name="mixer_a",
    )(x, w_in, conv_w, mem_kv, mem_kv, w_o, g, b)


def _partial_rope(t, pos, freq):
    half = ROPE_DIM // 2
    cos, sin = _rope_tables(pos, freq, MIX_WIDTH // LANES)
    dim = lax.broadcasted_iota(jnp.int32, (1, MIX_WIDTH), 1) % HEAD_DIM
    upper = pltpu.roll(t, MIX_WIDTH - half, 1)
    lower = pltpu.roll(t, half, 1)
    return t * cos + jnp.where(dim < half, -upper, lower) * sin


def _shared_kv_kernel(x_ref, pos_ref, freq_ref, wk_ref, wv_ref, k_ref, v_ref, km_ref):
    xb = x_ref[0].astype(BF16)
    k = _partial_rope(_dot(xb, wk_ref[...]), pos_ref[0], freq_ref[...])
    k_ref[0] = k.astype(BF16)
    v_ref[0] = _dot(xb, wv_ref[...]).astype(BF16)
    km_ref[0, 0] = jnp.mean(k, axis=0, keepdims=True)


def _shared_kv(x, pos, freq, w_k, w_v):
    bsz, seq, d = x.shape
    n_blocks = seq // MOBA_BLOCK
    full = lambda bi, ti: (0, 0)
    return pl.pallas_call(
        _shared_kv_kernel,
        grid=(bsz, n_blocks),
        in_specs=[
            pl.BlockSpec((1, MOBA_BLOCK, d), lambda bi, ti: (bi, ti, 0)),
            pl.BlockSpec((1, MOBA_BLOCK, 1), lambda bi, ti: (bi, ti, 0)),
            pl.BlockSpec(freq.shape, full),
            pl.BlockSpec(w_k.shape, full),
            pl.BlockSpec(w_v.shape, full),
        ],
        out_specs=[
            pl.BlockSpec((1, MOBA_BLOCK, MIX_WIDTH), lambda bi, ti: (bi, ti, 0)),
            pl.BlockSpec((1, MOBA_BLOCK, MIX_WIDTH), lambda bi, ti: (bi, ti, 0)),
            pl.BlockSpec((1, 1, 1, MIX_WIDTH), lambda bi, ti: (bi, ti, 0, 0)),
        ],
        out_shape=[
            jax.ShapeDtypeStruct((bsz, seq, MIX_WIDTH), BF16),
            jax.ShapeDtypeStruct((bsz, seq, MIX_WIDTH), BF16),
            jax.ShapeDtypeStruct((bsz, n_blocks, 1, MIX_WIDTH), F32),
        ],
        compiler_params=_params(("arbitrary", "arbitrary")),
        name="shared_kv",
    )(x, pos, freq, w_k, w_v)


def _query_b_kernel(x_ref, pos_ref, freq_ref, wq_ref, mk_ref, mv_ref, q_ref, mo_ref):
    xb = x_ref[0].astype(BF16)
    proj = _dot(xb, wq_ref[...])
    q = _partial_rope(proj[:, :MIX_WIDTH], pos_ref[0], freq_ref[...])
    q_ref[0] = (q * (ATTN_SCALE * LOG2_E)).astype(BF16)
    mo_ref[0] = _mem_attention(proj[:, MIX_WIDTH:], mk_ref[0], mv_ref[0]).astype(BF16)


def _query_b(x, pos, freq, w_q, mem_kv, layer):
    bsz, seq, d = x.shape
    tile = min(SEQ_TILE, seq)
    mem_len = mem_kv.shape[1]
    full = lambda bi, ti: (0, 0)
    return pl.pallas_call(
        _query_b_kernel,
        grid=(bsz, seq // tile),
        in_specs=[
            pl.BlockSpec((1, tile, d), lambda bi, ti: (bi, ti, 0)),
            pl.BlockSpec((1, tile, 1), lambda bi, ti: (bi, ti, 0)),
            pl.BlockSpec(freq.shape, full),
            pl.BlockSpec(w_q.shape, full),
            pl.BlockSpec((1, mem_len, MEM_WIDTH), lambda bi, ti: (bi, 0, 2 * layer)),
            pl.BlockSpec((1, mem_len, MEM_WIDTH), lambda bi, ti: (bi, 0, 2 * layer + 1)),
        ],
        out_specs=[
            pl.BlockSpec((1, tile, MIX_WIDTH), lambda bi, ti: (bi, ti, 0)),
            pl.BlockSpec((1, tile, MEM_WIDTH), lambda bi, ti: (bi, ti, 0)),
        ],
        out_shape=[
            jax.ShapeDtypeStruct((bsz, seq, MIX_WIDTH), BF16),
            jax.ShapeDtypeStruct((bsz, seq, MEM_WIDTH), BF16),
        ],
        compiler_params=_params(("arbitrary", "arbitrary")),
        name="query_b",
    )(x, pos, freq, w_q, mem_kv, mem_kv)


def _moba_kernel(q_ref, k_ref, v_ref, km_ref, o_ref, kaug_ref, vaug_ref, m_ref, acc_ref, s0_ref,
                 s1_ref, *, chunk_blocks):
    blk = MOBA_BLOCK
    span = chunk_blocks * blk
    pair = pl.program_id(2)
    chains = [(u, h) for u in range(chunk_blocks) for h in range(HEADS_PER_LANE_GROUP)]
    rows = len(chains) * blk

    @pl.when(pair == 0)
    def _():
        kaug_ref[:, :LANES] = k_ref[0]
        vaug_ref[:, :LANES] = v_ref[0]
        lane = lax.broadcasted_iota(jnp.int32, (blk, LANES), 1)
        ones_lane = (lane == 0).astype(BF16)

        def fill(j, carry):
            off = pl.multiple_of(j * blk, blk)
            kaug_ref[pl.ds(off, blk), LANES:] = (lane == j).astype(BF16)
            vaug_ref[pl.ds(off, blk), LANES:] = ones_lane
            return carry

        lax.fori_loop(0, k_ref.shape[1] // blk, fill, 0)

    def chunk_rows(c):
        return pl.ds(pl.multiple_of(c * span, span), span)

    def sweep(half, sa_ref, sb_ref):
        own_chunk = 2 * pair + half
        q = q_ref[0, half * span:(half + 1) * span, :]
        q_heads = jnp.concatenate(
            [jnp.where(_head_lane_mask(LANES, h), q[u * blk:(u + 1) * blk], 0).astype(BF16)
             for u, h in chains], axis=0)

        block_id = lax.broadcasted_iota(jnp.int32, (rows, LANES), 1)
        own = (own_chunk * chunk_blocks
               + lax.broadcasted_iota(jnp.int32, (rows, LANES), 0) // (HEADS_PER_LANE_GROUP * blk))
        is_past = block_id < own
        gate = jnp.where(is_past, _dot_nt(q_heads, km_ref[0].astype(BF16)), -jnp.inf)
        kth = gate
        for _ in range(MOBA_TOPK - 1):
            kth = jnp.where(kth == jnp.max(kth, axis=-1, keepdims=True), -jnp.inf, kth)
        threshold = jnp.max(kth, axis=-1, keepdims=True)
        allowed = (is_past & (gate >= threshold)) | (block_id == own)
        block_bias = jnp.where(allowed, 0.0, MASK_VALUE).astype(BF16)
        q_aug = jnp.concatenate([q_heads, block_bias], axis=1)

        state_rows = [slice(half * rows + i * blk, half * rows + (i + 1) * blk)
                      for i in range(len(chains))]
        chain_rows = [slice(i * blk, (i + 1) * blk) for i in range(len(chains))]
        for r in state_rows:
            m_ref[r, :] = jnp.full((blk, 1), MASK_VALUE, F32)
            acc_ref[r, :] = jnp.zeros((blk, acc_ref.shape[1]), F32)

        def score_own(s_ref):
            for r, (u, _) in zip(chain_rows, chains):
                width = (u + 1) * blk
                k_part = kaug_ref[pl.ds(pl.multiple_of(own_chunk * span, span), width), :]
                k_pos = lax.broadcasted_iota(jnp.int32, (blk, width), 1)
                q_pos = u * blk + lax.broadcasted_iota(jnp.int32, (blk, width), 0)
                s_ref[r, :width] = jnp.where(k_pos <= q_pos, _dot_nt(q_aug[r], k_part), MASK_VALUE)
                if width < span:
                    s_ref[r, width:] = jnp.full((blk, span - width), MASK_VALUE, F32)

        def score_past(c, s_ref):
            k_chunk = kaug_ref[chunk_rows(c), :]
            for r in chain_rows:
                s_ref[r, :] = _dot_nt(q_aug[r], k_chunk)

        def softmax_value(c, s_ref):
            v_chunk = vaug_ref[chunk_rows(c), :]
            for r, sr in zip(chain_rows, state_rows):
                s = s_ref[r, :]
                m_old = m_ref[sr, :]
                m_new = jnp.maximum(m_old, jnp.max(s, axis=-1, keepdims=True))
                p = jnp.exp2((s - m_new).astype(BF16))
                acc_ref[sr, :] = jnp.exp2(m_old - m_new) * acc_ref[sr, :] + _dot(p, v_chunk)
                m_ref[sr, :] = m_new

        def step_chunk(j):
            return jnp.where(j == 0, own_chunk, j - 1)

        n_past = own_chunk
        score_own(sa_ref)

        def two_steps(i, carry):
            j = 2 * i + 1
            score_past(j - 1, sb_ref)
            softmax_value(step_chunk(j - 1), sa_ref)
            score_past(j, sa_ref)
            softmax_value(j - 1, sb_ref)
            return carry

        lax.fori_loop(0, pair, two_steps, 0)
        if half == 1:
            score_past(n_past - 1, sb_ref)
            softmax_value(step_chunk(n_past - 1), sa_ref)
            softmax_value(n_past - 1, sb_ref)
        else:
            softmax_value(step_chunk(n_past), sa_ref)

        for u in range(chunk_blocks):
            out = None
            for h in range(HEADS_PER_LANE_GROUP):
                acc = acc_ref[state_rows[u * HEADS_PER_LANE_GROUP + h], :]
                o = acc[:, :LANES] / acc[:, LANES:LANES + 1]
                out = o if out is None else jnp.where(_head_lane_mask(LANES, h), o, out)
            o_ref[0, half * span + u * blk:half * span + (u + 1) * blk, :] = out.astype(o_ref.dtype)

    sweep(0, s0_ref, s1_ref)
    sweep(1, s1_ref, s0_ref)


def _moba(q, k, v, k_means):
    bsz, seq, _ = q.shape
    n_blocks = seq // MOBA_BLOCK
    groups = MIX_WIDTH // LANES
    chunk_blocks = MOBA_CHUNK_BLOCKS
    while n_blocks % (2 * chunk_blocks):
        chunk_blocks //= 2
    assert chunk_blocks >= 1, "the sequence must hold an even number of MoBA blocks"
    span = chunk_blocks * MOBA_BLOCK
    rows = HEADS_PER_LANE_GROUP * span
    return pl.pallas_call(
        functools.partial(_moba_kernel, chunk_blocks=chunk_blocks),
        grid=(bsz, groups, seq // (2 * span)),
        in_specs=[
            pl.BlockSpec((1, 2 * span, LANES), lambda bi, gi, qi: (bi, qi, gi)),
            pl.BlockSpec((1, seq, LANES), lambda bi, gi, qi: (bi, 0, gi)),
            pl.BlockSpec((1, seq, LANES), lambda bi, gi, qi: (bi, 0, gi)),
            pl.BlockSpec((1, LANES, LANES), lambda bi, gi, qi: (bi, 0, gi)),
        ],
        out_specs=pl.BlockSpec((1, 2 * span, LANES), lambda bi, gi, qi: (bi, qi, gi)),
        out_shape=jax.ShapeDtypeStruct(q.shape, BF16),
        scratch_shapes=[pltpu.VMEM((seq, 2 * LANES), BF16), pltpu.VMEM((seq, 2 * LANES), BF16),
                        pltpu.VMEM((2 * rows, 1), F32), pltpu.VMEM((2 * rows, 2 * LANES), F32),
                        pltpu.VMEM((rows, span), F32), pltpu.VMEM((rows, span), F32)],
        compiler_params=_params(("arbitrary", "arbitrary", "arbitrary")),
        name="moba",
    )(q, k, v, k_means)


def _mixer_out_kernel(x_ref, main_ref, mo_ref, wo_ref, g_ref, b_ref, o_ref, *, alpha):
    mix = _dot(main_ref[0], wo_ref[:MIX_WIDTH, :]) + _dot(mo_ref[0], wo_ref[MIX_WIDTH:, :])
    o_ref[0] = _layer_norm(alpha * x_ref[0] + mix, g_ref[...], b_ref[...])


def _mixer_out(x, main, mem_out, w_o, g, b, alpha):
    bsz, seq, d = x.shape
    tile = min(SEQ_TILE, seq)
    full = lambda bi, ti: (0, 0)
    return pl.pallas_call(
        functools.partial(_mixer_out_kernel, alpha=alpha),
        grid=(bsz, seq // tile),
        in_specs=[
            pl.BlockSpec((1, tile, d), lambda bi, ti: (bi, ti, 0)),
            pl.BlockSpec((1, tile, MIX_WIDTH), lambda bi, ti: (bi, ti, 0)),
            pl.BlockSpec((1, tile, MEM_WIDTH), lambda bi, ti: (bi, ti, 0)),
            pl.BlockSpec(w_o.shape, full),
            pl.BlockSpec((1, d), full),
            pl.BlockSpec((1, d), full),
        ],
        out_specs=pl.BlockSpec((1, tile, d), lambda bi, ti: (bi, ti, 0)),
        out_shape=jax.ShapeDtypeStruct(x.shape, F32),
        compiler_params=_params(("arbitrary", "arbitrary")),
        name="mixer_out",
    )(x, main, mem_out, w_o, g, b)


def _swiglu_chunk(xb, wg, wu, wd):
    gp = _dot(xb, wg)
    up = _dot(xb, wu)
    return _dot((gp * jax.nn.sigmoid(gp) * up).astype(BF16), wd)


def _ffn_chunks(d_ff):
    chunk = FF_CHUNK if d_ff % FF_CHUNK == 0 else d_ff
    return chunk, d_ff // chunk


def _accumulate_chunks(c, n_chunks, part, acc_ref, finish):
    if n_chunks == 1:
        finish(part())
        return

    @pl.when(c == 0)
    def _():
        acc_ref[...] = part()

    if n_chunks > 2:
        @pl.when((c > 0) & (c < n_chunks - 1))
        def _():
            acc_ref[...] += part()

    @pl.when(c == n_chunks - 1)
    def _():
        finish(acc_ref[...] + part())


def _ffn_kernel(x_ref, wg_ref, wu_ref, wd_ref, g_ref, b_ref, o_ref, acc_ref, xb_ref, *, alpha,
                n_chunks):
    c = pl.program_id(1)

    @pl.when(c == 0)
    def _():
        xb_ref[...] = x_ref[...].astype(BF16)

    def finish(ffn):
        o_ref[...] = _layer_norm(alpha * x_ref[...] + ffn, g_ref[...], b_ref[...])

    def part():
        return _swiglu_chunk(xb_ref[...], wg_ref[...], wu_ref[...], wd_ref[...])

    _accumulate_chunks(c, n_chunks, part, acc_ref, finish)


def _ffn(x, w_gu, w_down, layer, g, b, alpha):
    n, d = x.shape
    d_ff = w_down.shape[1]
    tile = min(FFN_TILE, n)
    chunk, n_chunks = _ffn_chunks(d_ff)
    return pl.pallas_call(
        functools.partial(_ffn_kernel, alpha=alpha, n_chunks=n_chunks),
        grid=(n // tile, n_chunks),
        in_specs=[
            pl.BlockSpec((tile, d), lambda i, c: (i, 0)),
            pl.BlockSpec((None, d, chunk), lambda i, c: (layer, 0, c)),
            pl.BlockSpec((None, d, chunk), lambda i, c: (layer, 0, c + n_chunks)),
            pl.BlockSpec((None, chunk, d), lambda i, c: (layer, c, 0)),
            pl.BlockSpec((1, d), lambda i, c: (0, 0)),
            pl.BlockSpec((1, d), lambda i, c: (0, 0)),
        ],
        out_specs=pl.BlockSpec((tile, d), lambda i, c: (i, 0)),
        out_shape=jax.ShapeDtypeStruct(x.shape, F32),
        scratch_shapes=[pltpu.VMEM((tile, d), F32), pltpu.VMEM((tile, d), BF16)],
        compiler_params=_params(("arbitrary", "arbitrary")),
        name="ffn_dense",
    )(x, w_gu, w_gu, w_down, g, b)


META_EXPERT, META_RANK, META_GATE, META_ROWS = 0, 2, 4, 8
EXPERT_ROWS = 16


def _split_bf16(a):
    hi = a.astype(BF16)
    return hi, (a - hi.astype(F32)).astype(BF16)


def _route_kernel(x_ref, wr_ref, meta_ref, counts_ref, *, n_experts):
    tile = x_ref.shape[0]

    @pl.when(pl.program_id(0) == 0)
    def _():
        counts_ref[...] = jnp.zeros_like(counts_ref)

    x_hi, x_lo = _split_bf16(x_ref[...])
    w_hi, w_lo = _split_bf16(wr_ref[...])
    hi_part = _dot_nt(jnp.concatenate([w_hi, w_lo], axis=0), x_hi)
    logits = hi_part[:EXPERT_ROWS] + (hi_part[EXPERT_ROWS:] + _dot_nt(w_hi, x_lo))
    expert = lax.broadcasted_iota(jnp.int32, (EXPERT_ROWS, tile), 0)
    logits = jnp.where(expert < n_experts, logits, -jnp.inf)
    top1 = jnp.max(logits, axis=0, keepdims=True)
    idx1 = jnp.min(jnp.where(logits == top1, expert, EXPERT_ROWS), axis=0, keepdims=True)
    rest = jnp.where(expert == idx1, -jnp.inf, logits)
    top2 = jnp.max(rest, axis=0, keepdims=True)
    idx2 = jnp.min(jnp.where(rest == top2, expert, EXPERT_ROWS), axis=0, keepdims=True)
    e2 = jnp.exp(top2 - top1)

    chosen = (expert == idx1) | (expert == idx2)
    earlier = (lax.broadcasted_iota(jnp.int32, (tile, tile), 0)
               < lax.broadcasted_iota(jnp.int32, (tile, tile), 1))
    rank = counts_ref[...] + _dot(chosen.astype(BF16), earlier.astype(BF16))
    rank1 = jnp.sum(jnp.where(expert == idx1, rank, 0.0), axis=0, keepdims=True)
    rank2 = jnp.sum(jnp.where(expert == idx2, rank, 0.0), axis=0, keepdims=True)
    counts_ref[...] += jnp.sum(chosen.astype(F32), axis=1, keepdims=True)

    fields = [idx1.astype(F32), idx2.astype(F32), rank1, rank2, 1.0 / (1.0 + e2), e2 / (1.0 + e2)]
    row = lax.broadcasted_iota(jnp.int32, (META_ROWS, tile), 0)
    meta = jnp.zeros((META_ROWS, tile), F32)
    for i, field in enumerate(fields):
        meta = jnp.where(row == i, field, meta)
    meta_ref[...] = meta


def _route(x, w_router):
    n, d = x.shape
    tile = min(ROUTE_TILE, n)
    n_experts = w_router.shape[1]
    return pl.pallas_call(
        functools.partial(_route_kernel, n_experts=n_experts),
        grid=(n // tile,),
        in_specs=[pl.BlockSpec((tile, d), lambda i: (i, 0)),
                  pl.BlockSpec((EXPERT_ROWS, d), lambda i: (0, 0))],
        out_specs=[pl.BlockSpec((META_ROWS, tile), lambda i: (0, i)),
                   pl.BlockSpec((EXPERT_ROWS, 1), lambda i: (0, 0))],
        out_shape=[jax.ShapeDtypeStruct((META_ROWS, n), F32),
                   jax.ShapeDtypeStruct((EXPERT_ROWS, 1), F32)],
        compiler_params=_params(("arbitrary",)),
        name="moe_route",
    )(x, jnp.pad(w_router.T, ((0, EXPERT_ROWS - n_experts), (0, 0))))


def _row_copy(src_ref, src_row, dst_ref, dst_row, sem):
    return pltpu.make_async_copy(src_ref.at[pl.ds(src_row, 1), :], dst_ref.at[pl.ds(dst_row, 1), :], sem)


def _start_then_wait(copies, n):
    def start(r, carry):
        for cp in copies(r):
            cp.start()
        return carry

    def wait(r, carry):
        for cp in copies(r):
            cp.wait()
        return carry

    lax.fori_loop(0, n, start, 0, unroll=DMA_LOOP_UNROLL)
    lax.fori_loop(0, n, wait, 0, unroll=DMA_LOOP_UNROLL)


def _dispatch_kernel(pos_ref, x_ref, init_ref, xs_ref, sem):
    del init_ref

    def copies(r):
        return [_row_copy(x_ref, r, xs_ref, pos_ref[0, 0, TOP_K * r + k], sem) for k in range(TOP_K)]

    _start_then_wait(copies, x_ref.shape[0])


def _dispatch(x, pos, grouped_init):
    n, d = x.shape
    tile = min(ROUTE_TILE, n)
    any_spec = pl.BlockSpec(memory_space=pl.ANY)
    return pl.pallas_call(
        _dispatch_kernel,
        grid=(n // tile,),
        in_specs=[pl.BlockSpec((1, 1, TOP_K * tile), lambda i: (i, 0, 0), memory_space=pltpu.SMEM),
                  pl.BlockSpec((tile, d), lambda i: (i, 0)), any_spec],
        out_specs=any_spec,
        out_shape=jax.ShapeDtypeStruct(grouped_init.shape, F32),
        scratch_shapes=[pltpu.SemaphoreType.DMA],
        input_output_aliases={2: 0},
        compiler_params=_params(("arbitrary",)),
        name="moe_dispatch",
    )(pos.reshape(n // tile, 1, TOP_K * tile), x, grouped_init)


def _expert_ffn_kernel(tile_expert_ref, n_tiles_ref, xs_ref, wg_ref, wu_ref, wd_ref, o_ref,
                       acc_ref, xb_ref, *, n_chunks):
    del tile_expert_ref
    c = pl.program_id(1)

    @pl.when(pl.program_id(0) < n_tiles_ref[0])
    def _():
        @pl.when(c == 0)
        def _():
            xb_ref[...] = xs_ref[...].astype(BF16)

        def finish(ffn):
            o_ref[...] = ffn

        def part():
            return _swiglu_chunk(xb_ref[...], wg_ref[0], wu_ref[0], wd_ref[0])

        _accumulate_chunks(c, n_chunks, part, acc_ref, finish)


def _expert_ffn(xs, tile_expert, n_tiles, w_gu, w_down, layer):
    n_rows, d = xs.shape
    d_ff = w_down.shape[2]
    chunk, n_chunks = _ffn_chunks(d_ff)
    row_tile = lambda t, c, te, nt: (jnp.minimum(t, nt[0] - 1), 0)
    grid_spec = pltpu.PrefetchScalarGridSpec(
        num_scalar_prefetch=2,
        grid=(n_rows // EXPERT_TILE, n_chunks),
        in_specs=[
            pl.BlockSpec((EXPERT_TILE, d), row_tile),
            pl.BlockSpec((None, 1, d, chunk), lambda t, c, te, nt: (layer, te[t], 0, c)),
            pl.BlockSpec((None, 1, d, chunk), lambda t, c, te, nt: (layer, te[t], 0, c + n_chunks)),
            pl.BlockSpec((None, 1, chunk, d), lambda t, c, te, nt: (layer, te[t], c, 0)),
        ],
        out_specs=pl.BlockSpec((EXPERT_TILE, d), row_tile),
        scratch_shapes=[pltpu.VMEM((EXPERT_TILE, d), F32), pltpu.VMEM((EXPERT_TILE, d), BF16)],
    )
    return pl.pallas_call(
        functools.partial(_expert_ffn_kernel, n_chunks=n_chunks),
        grid_spec=grid_spec,
        out_shape=jax.ShapeDtypeStruct((n_rows, d), F32),
        compiler_params=_params(("arbitrary", "arbitrary")),
        name="moe_experts",
    )(tile_expert, n_tiles, xs, w_gu, w_gu, w_down)


def _combine_kernel(pos_ref, gates_ref, x_ref, g_ref, b_ref, ys_ref, o_ref, rows_ref, sem, *, alpha):
    tile = x_ref.shape[0]

    def copies(r):
        return [_row_copy(ys_ref, pos_ref[0, 0, TOP_K * r + k], rows_ref.at[k], r, sem)
                for k in range(TOP_K)]

    _start_then_wait(copies, tile)

    gates = gates_ref[...]
    ffn = jnp.zeros(x_ref.shape, F32)
    for k in range(TOP_K):
        ffn = ffn + gates[:, k:k + 1] * rows_ref[k]
    o_ref[...] = _layer_norm(alpha * x_ref[...] + ffn, g_ref[...], b_ref[...])


def _combine(x, gates, pos, ys, g, b, alpha):
    n, d = x.shape
    tile = min(COMBINE_TILE, n)
    return pl.pallas_call(
        functools.partial(_combine_kernel, alpha=alpha),
        grid=(n // tile,),
        in_specs=[pl.BlockSpec((1, 1, TOP_K * tile), lambda i: (i, 0, 0), memory_space=pltpu.SMEM),
                  pl.BlockSpec((tile, TOP_K), lambda i: (i, 0)),
                  pl.BlockSpec((tile, d), lambda i: (i, 0)),
                  pl.BlockSpec((1, d), lambda i: (0, 0)),
                  pl.BlockSpec((1, d), lambda i: (0, 0)),
                  pl.BlockSpec(memory_space=pl.ANY)],
        out_specs=pl.BlockSpec((tile, d), lambda i: (i, 0)),
        out_shape=jax.ShapeDtypeStruct(x.shape, F32),
        scratch_shapes=[pltpu.VMEM((TOP_K, tile, d), F32), pltpu.SemaphoreType.DMA],
        compiler_params=_params(("arbitrary",)),
        name="moe_combine",
    )(pos.reshape(n // tile, 1, TOP_K * tile), gates, x, g, b, ys)


def _moe(x, w_router, w_gu, w_down, layer, g, b, alpha, grouped_init):
    n, d = x.shape
    n_experts = w_router.shape[1]
    meta, counts = _route(x, w_router)

    counts = counts[:n_experts, 0].astype(jnp.int32)
    group_tiles = (counts + EXPERT_TILE - 1) // EXPERT_TILE
    tiles_end = jnp.cumsum(group_tiles)
    group_row0 = (tiles_end - group_tiles) * EXPERT_TILE
    n_grid_tiles = (TOP_K * n) // EXPERT_TILE + n_experts
    tile_ids = jnp.arange(n_grid_tiles, dtype=jnp.int32)
    tile_expert = jnp.minimum(jnp.sum((tile_ids[:, None] >= tiles_end[None, :]).astype(jnp.int32), axis=1),
                              n_experts - 1)
    records = meta.T
    expert = records[:, META_EXPERT:META_EXPERT + TOP_K].astype(jnp.int32)
    rank = records[:, META_RANK:META_RANK + TOP_K].astype(jnp.int32)
    gates = records[:, META_GATE:META_GATE + TOP_K]
    pos = (group_row0[expert] + rank).reshape(-1)

    if grouped_init is None:
        grouped_init = jnp.zeros((n_grid_tiles * EXPERT_TILE, d), F32)
    xs = _dispatch(x, pos, grouped_init)
    ys = _expert_ffn(xs, tile_expert, tiles_end[-1:].astype(jnp.int32), w_gu, w_down, layer)
    return _combine(x, gates, pos, ys, g, b, alpha), xs


def _rope_lane_frequencies():
    inv_freq = ROPE_THETA ** (-jnp.arange(0, ROPE_DIM, 2, dtype=F32) / ROPE_DIM)
    c = jnp.arange(LANES) % HEAD_DIM
    return jnp.where(c < ROPE_DIM, inv_freq[c % (ROPE_DIM // 2)], 0.0).astype(F32)[None, :]


def kernel(x, mem, positions, w_in_a, conv_a, w_q_b, w_kv_shared, w_mem_kv, w_o, ln1_g, ln1_b,
           ln2_g, ln2_b, w_gu_dense, w_down_dense, w_router, w_gu_moe, w_down_moe):
    bsz, seq, d = x.shape
    depth = w_o.shape[0]
    n_a = w_in_a.shape[0]
    alpha = float((2 * depth) ** 0.25)
    assert seq % MOBA_BLOCK == 0 and seq // MOBA_BLOCK <= LANES

    pos = positions.astype(F32)[:, :, None]
    freq = _rope_lane_frequencies()
    mem_len = mem.shape[1]
    w_mem_all = jnp.transpose(w_mem_kv, (1, 0, 2)).reshape(d, depth * 2 * MEM_WIDTH).astype(BF16)
    mem_kv = _matmul(mem.reshape(bsz * mem_len, d), w_mem_all, BF16).reshape(bsz, mem_len, -1)

    w_gu_dense_bf16, w_down_dense_bf16 = w_gu_dense.astype(BF16), w_down_dense.astype(BF16)
    w_gu_moe_bf16, w_down_moe_bf16 = w_gu_moe.astype(BF16), w_down_moe.astype(BF16)
    shared = None
    grouped = None
    for layer in range(depth):
        g1, b1 = ln1_g[layer][None, :], ln1_b[layer][None, :]
        g2, b2 = ln2_g[layer][None, :], ln2_b[layer][None, :]
        wo = w_o[layer].astype(BF16)
        if layer < n_a:
            x = _mixer_a(x, w_in_a[layer].astype(BF16), conv_a[layer], mem_kv, layer, wo, g1, b1, alpha)
        else:
            wq = w_q_b[layer - n_a]
            q, mem_out = _query_b(x, pos, freq, wq.astype(BF16), mem_kv, layer)
            main = _moba(q, *shared)
            x = _mixer_out(x, main, mem_out, wo, g1, b1, alpha)
        xf = x.reshape(bsz * seq, d)
        if layer % 2 == 0:
            xf = _ffn(xf, w_gu_dense_bf16, w_down_dense_bf16, layer // 2, g2, b2, alpha)
        else:
            xf, grouped = _moe(xf, w_router[layer // 2], w_gu_moe_bf16, w_down_moe_bf16, layer // 2,
                               g2, b2, alpha, grouped)
        x = xf.reshape(bsz, seq, d)
        if layer == n_a - 1:
            wk = w_kv_shared[:, :MIX_WIDTH]
            k, v, km = _shared_kv(x, pos, freq, wk.astype(BF16), w_kv_shared[:, MIX_WIDTH:].astype(BF16))
            n_blocks = seq // MOBA_BLOCK
            km = jnp.pad(km.reshape(bsz, n_blocks, MIX_WIDTH), ((0, 0), (0, LANES - n_blocks), (0, 0)))
            shared = (k, v, km)
    return x
```

```python
import functools

import jax
import jax.numpy as jnp
from jax import lax
from jax.experimental import pallas as pl
from jax.experimental.pallas import tpu as pltpu

F32 = jnp.float32
BF16 = jnp.bfloat16

HEAD_DIM = 64
MIX_HEADS = 12
MIX_WIDTH = MIX_HEADS * HEAD_DIM
MEM_HEADS = 4
MEM_WIDTH = MEM_HEADS * HEAD_DIM
CONV_WIDTH = 3
ROPE_DIM = HEAD_DIM // 4
ROPE_THETA = 500000.0
MOBA_BLOCK = 256
MOBA_TOPK = 3
TOP_K = 2
LN_EPS = 1e-5
ATTN_SCALE = HEAD_DIM ** -0.5
LOG2_E = 1.4426950408889634

LANES = 128
HEADS_PER_LANE_GROUP = LANES // HEAD_DIM
MASK_VALUE = -1e30
VMEM_LIMIT_BYTES = 56 * 1024 * 1024

SEQ_TILE = 512
FFN_TILE = 1024
FF_CHUNK = 1408
MOBA_CHUNK_BLOCKS = 4
ROUTE_TILE = 1024
EXPERT_TILE = 512
COMBINE_TILE = 1024
DMA_LOOP_UNROLL = 8
DMA_PRIORITIES = 2


def _params(semantics):
    return pltpu.CompilerParams(dimension_semantics=semantics, vmem_limit_bytes=VMEM_LIMIT_BYTES)


def _layer_norm(r, g, b):
    mu = jnp.mean(r, axis=-1, keepdims=True)
    c = r - mu
    var = jnp.mean(c * c, axis=-1, keepdims=True)
    return c * lax.rsqrt(var + LN_EPS) * g + b


def _dot(a, b):
    return jnp.dot(a, b, preferred_element_type=F32)


def _dot_nt(a, b):
    return lax.dot_general(a, b, (((1,), (1,)), ((), ())), preferred_element_type=F32)


def _head_lane_mask(width, head):
    lane = lax.broadcasted_iota(jnp.int32, (1, width), 1)
    return (lane >= head * HEAD_DIM) & (lane < (head + 1) * HEAD_DIM)


def _mem_attention(q_mem, mem_k, mem_v):
    out = jnp.zeros(q_mem.shape, F32)
    for head in range(MEM_HEADS):
        hm = _head_lane_mask(MEM_WIDTH, head)
        qh = jnp.where(hm, q_mem * ATTN_SCALE, 0.0).astype(BF16)
        s = _dot_nt(qh, mem_k)
        m = jnp.max(s, axis=-1, keepdims=True)
        p = jnp.exp(s - m)
        l = jnp.sum(p, axis=-1, keepdims=True)
        o = _dot(p.astype(BF16), mem_v) / l
        out = jnp.where(hm, o, out)
    return out


def _rope_tables(pos, freq, reps):
    angle = pos * freq
    cos = jnp.cos(angle)
    sin = jnp.sin(angle)
    return (jnp.concatenate([cos] * reps, axis=1), jnp.concatenate([sin] * reps, axis=1))


def _matmul_kernel(x_ref, w_ref, o_ref):
    o_ref[...] = _dot(x_ref[...].astype(BF16), w_ref[...]).astype(o_ref.dtype)


def _matmul(x, w, out_dtype, tile_m=256):
    m, k = x.shape
    n = w.shape[1]
    return pl.pallas_call(
        _matmul_kernel,
        grid=(m // tile_m,),
        in_specs=[pl.BlockSpec((tile_m, k), lambda i: (i, 0)),
                  pl.BlockSpec((k, n), lambda i: (0, 0))],
        out_specs=pl.BlockSpec((tile_m, n), lambda i: (i, 0)),
        out_shape=jax.ShapeDtypeStruct((m, n), out_dtype),
        compiler_params=_params(("arbitrary",)),
        name="mem_kv_proj",
    )(x, w)


def _mixer_a_kernel(x_ref, win_ref, conv_ref, mk_ref, mv_ref, wo_ref, g_ref, b_ref, o_ref,
                    ztail_ref, *, alpha):
    tile = x_ref.shape[1]

    @pl.when(pl.program_id(1) == 0)
    def _():
        ztail_ref[...] = jnp.zeros_like(ztail_ref)

    x = x_ref[0]
    proj = _dot(x.astype(BF16), win_ref[...])
    u_b = proj[:, :MIX_WIDTH]
    u_c = proj[:, MIX_WIDTH:2 * MIX_WIDTH]
    u_h = proj[:, 2 * MIX_WIDTH:3 * MIX_WIDTH]
    q_mem = proj[:, 3 * MIX_WIDTH:]

    z = u_c * u_h
    tail = ztail_ref[...]
    ztail_ref[...] = z[tile - 8:, :]
    row = lax.broadcasted_iota(jnp.int32, z.shape, 0)
    z1 = jnp.where(row == 0, tail[7:8, :], pltpu.roll(z, 1, 0))
    z2 = jnp.where(row == 0, tail[6:7, :], jnp.where(row == 1, tail[7:8, :], pltpu.roll(z, 2, 0)))
    w = conv_ref[...]
    main = u_b * (w[0:1, :] * z2 + w[1:2, :] * z1 + w[2:3, :] * z)

    mem_out = _mem_attention(q_mem, mk_ref[0], mv_ref[0])
    mix = (_dot(main.astype(BF16), wo_ref[:MIX_WIDTH, :])
           + _dot(mem_out.astype(BF16), wo_ref[MIX_WIDTH:, :]))
    o_ref[0] = _layer_norm(alpha * x + mix, g_ref[...], b_ref[...])


def _mixer_a(x, w_in, conv_w, mem_kv, layer, w_o, g, b, alpha):
    bsz, seq, d = x.shape
    tile = min(SEQ_TILE, seq)
    mem_len = mem_kv.shape[1]
    return pl.pallas_call(
        functools.partial(_mixer_a_kernel, alpha=alpha),
        grid=(bsz, seq // tile),
        in_specs=[
            pl.BlockSpec((1, tile, d), lambda bi, ti: (bi, ti, 0)),
            pl.BlockSpec(w_in.shape, lambda bi, ti: (0, 0)),
            pl.BlockSpec(conv_w.shape, lambda bi, ti: (0, 0)),
            pl.BlockSpec((1, mem_len, MEM_WIDTH), lambda bi, ti: (bi, 0, 2 * layer)),
            pl.BlockSpec((1, mem_len, MEM_WIDTH), lambda bi, ti: (bi, 0, 2 * layer + 1)),
            pl.BlockSpec(w_o.shape, lambda bi, ti: (0, 0)),
            pl.BlockSpec((1, d), lambda bi, ti: (0, 0)),
            pl.BlockSpec((1, d), lambda bi, ti: (0, 0)),
        ],
        out_specs=pl.BlockSpec((1, tile, d), lambda bi, ti: (bi, ti, 0)),
        out_shape=jax.ShapeDtypeStruct(x.shape, F32),
        scratch_shapes=[pltpu.VMEM((8, MIX_WIDTH), F32)],
        compiler_params=_params(("arbitrary", "arbitrary")),
        name="mixer_a",
    )(x, w_in, conv_w, mem_kv, mem_kv, w_o, g, b)


def _partial_rope(t, pos, freq):
    half = ROPE_DIM // 2
    cos, sin = _rope_tables(pos, freq, MIX_WIDTH // LANES)
    dim = lax.broadcasted_iota(jnp.int32, (1, MIX_WIDTH), 1) % HEAD_DIM
    upper = pltpu.roll(t, MIX_WIDTH - half, 1)
    lower = pltpu.roll(t, half, 1)
    return t * cos + jnp.where(dim < half, -upper, lower) * sin


def _shared_kv_kernel(x_ref, pos_ref, freq_ref, wk_ref, wv_ref, k_ref, v_ref, km_ref):
    xb = x_ref[0].astype(BF16)
    k = _partial_rope(_dot(xb, wk_ref[...]), pos_ref[0], freq_ref[...])
    k_ref[0] = k.astype(BF16)
    v_ref[0] = _dot(xb, wv_ref[...]).astype(BF16)
    km_ref[0, 0] = jnp.mean(k, axis=0, keepdims=True)


def _shared_kv(x, pos, freq, w_k, w_v):
    bsz, seq, d = x.shape
    n_blocks = seq // MOBA_BLOCK
    full = lambda bi, ti: (0, 0)
    return pl.pallas_call(
        _shared_kv_kernel,
        grid=(bsz, n_blocks),
        in_specs=[
            pl.BlockSpec((1, MOBA_BLOCK, d), lambda bi, ti: (bi, ti, 0)),
            pl.BlockSpec((1, MOBA_BLOCK, 1), lambda bi, ti: (bi, ti, 0)),
            pl.BlockSpec(freq.shape, full),
            pl.BlockSpec(w_k.shape, full),
            pl.BlockSpec(w_v.shape, full),
        ],
        out_specs=[
            pl.BlockSpec((1, MOBA_BLOCK, MIX_WIDTH), lambda bi, ti: (bi, ti, 0)),
            pl.BlockSpec((1, MOBA_BLOCK, MIX_WIDTH), lambda bi, ti: (bi, ti, 0)),
            pl.BlockSpec((1, 1, 1, MIX_WIDTH), lambda bi, ti: (bi, ti, 0, 0)),
        ],
        out_shape=[
            jax.ShapeDtypeStruct((bsz, seq, MIX_WIDTH), BF16),
            jax.ShapeDtypeStruct((bsz, seq, MIX_WIDTH), BF16),
            jax.ShapeDtypeStruct((bsz, n_blocks, 1, MIX_WIDTH), F32),
        ],
        compiler_params=_params(("arbitrary", "arbitrary")),
        name="shared_kv",
    )(x, pos, freq, w_k, w_v)


def _query_b_kernel(x_ref, pos_ref, freq_ref, wq_ref, mk_ref, mv_ref, q_ref, mo_ref):
    xb = x_ref[0].astype(BF16)
    proj = _dot(xb, wq_ref[...])
    q = _partial_rope(proj[:, :MIX_WIDTH], pos_ref[0], freq_ref[...])
    q_ref[0] = (q * (ATTN_SCALE * LOG2_E)).astype(BF16)
    mo_ref[0] = _mem_attention(proj[:, MIX_WIDTH:], mk_ref[0], mv_ref[0]).astype(BF16)


def _query_b(x, pos, freq, w_q, mem_kv, layer):
    bsz, seq, d = x.shape
    tile = min(SEQ_TILE, seq)
    mem_len = mem_kv.shape[1]
    full = lambda bi, ti: (0, 0)
    return pl.pallas_call(
        _query_b_kernel,
        grid=(bsz, seq // tile),
        in_specs=[
            pl.BlockSpec((1, tile, d), lambda bi, ti: (bi, ti, 0)),
            pl.BlockSpec((1, tile, 1), lambda bi, ti: (bi, ti, 0)),
            pl.BlockSpec(freq.shape, full),
            pl.BlockSpec(w_q.shape, full),
            pl.BlockSpec((1, mem_len, MEM_WIDTH), lambda bi, ti: (bi, 0, 2 * layer)),
            pl.BlockSpec((1, mem_len, MEM_WIDTH), lambda bi, ti: (bi, 0, 2 * layer + 1)),
        ],
        out_specs=[
            pl.BlockSpec((1, tile, MIX_WIDTH), lambda bi, ti: (bi, ti, 0)),
            pl.BlockSpec((1, tile, MEM_WIDTH), lambda bi, ti: (bi, ti, 0)),
        ],
        out_shape=[
            jax.ShapeDtypeStruct((bsz, seq, MIX_WIDTH), BF16),
            jax.ShapeDtypeStruct((bsz, seq, MEM_WIDTH), BF16),
        ],
        compiler_params=_params(("arbitrary", "arbitrary")),
        name="query_b",
    )(x, pos, freq, w_q, mem_kv, mem_kv)


def _moba_kernel(q_ref, k_ref, v_ref, km_ref, o_ref, kaug_ref, vaug_ref, m_ref, acc_ref, s0_ref,
                 s1_ref, *, chunk_blocks):
    blk = MOBA_BLOCK
    span = chunk_blocks * blk
    pair = pl.program_id(2)
    chains = [(u, h) for u in range(chunk_blocks) for h in range(HEADS_PER_LANE_GROUP)]
    rows = len(chains) * blk

    @pl.when(pair == 0)
    def _():
        kaug_ref[:, :LANES] = k_ref[0]
        vaug_ref[:, :LANES] = v_ref[0]
        lane = lax.broadcasted_iota(jnp.int32, (blk, LANES), 1)
        ones_lane = (lane == 0).astype(BF16)

        def fill(j, carry):
            off = pl.multiple_of(j * blk, blk)
            kaug_ref[pl.ds(off, blk), LANES:] = (lane == j).astype(BF16)
            vaug_ref[pl.ds(off, blk), LANES:] = ones_lane
            return carry

        lax.fori_loop(0, k_ref.shape[1] // blk, fill, 0)

    def chunk_rows(c):
        return pl.ds(pl.multiple_of(c * span, span), span)

    def sweep(half, sa_ref, sb_ref):
        own_chunk = 2 * pair + half
        q = q_ref[0, half * span:(half + 1) * span, :]
        q_heads = jnp.concatenate(
            [jnp.where(_head_lane_mask(LANES, h), q[u * blk:(u + 1) * blk], 0).astype(BF16)
             for u, h in chains], axis=0)

        block_id = lax.broadcasted_iota(jnp.int32, (rows, LANES), 1)
        own = (own_chunk * chunk_blocks
               + lax.broadcasted_iota(jnp.int32, (rows, LANES), 0) // (HEADS_PER_LANE_GROUP * blk))
        is_past = block_id < own
        gate = jnp.where(is_past, _dot_nt(q_heads, km_ref[0].astype(BF16)), -jnp.inf)
        kth = gate
        for _ in range(MOBA_TOPK - 1):
            kth = jnp.where(kth == jnp.max(kth, axis=-1, keepdims=True), -jnp.inf, kth)
        threshold = jnp.max(kth, axis=-1, keepdims=True)
        allowed = (is_past & (gate >= threshold)) | (block_id == own)
        block_bias = jnp.where(allowed, 0.0, MASK_VALUE).astype(BF16)
        q_aug = jnp.concatenate([q_heads, block_bias], axis=1)

        state_rows = [slice(half * rows + i * blk, half * rows + (i + 1) * blk)
                      for i in range(len(chains))]
        chain_rows = [slice(i * blk, (i + 1) * blk) for i in range(len(chains))]
        for r in state_rows:
            m_ref[r, :] = jnp.full((blk, 1), MASK_VALUE, F32)
            acc_ref[r, :] = jnp.zeros((blk, acc_ref.shape[1]), F32)

        def score_own(s_ref):
            for r, (u, _) in zip(chain_rows, chains):
                width = (u + 1) * blk
                k_part = kaug_ref[pl.ds(pl.multiple_of(own_chunk * span, span), width), :]
                k_pos = lax.broadcasted_iota(jnp.int32, (blk, width), 1)
                q_pos = u * blk + lax.broadcasted_iota(jnp.int32, (blk, width), 0)
                s_ref[r, :width] = jnp.where(k_pos <= q_pos, _dot_nt(q_aug[r], k_part), MASK_VALUE)
                if width < span:
                    s_ref[r, width:] = jnp.full((blk, span - width), MASK_VALUE, F32)

        def score_past(c, s_ref):
            k_chunk = kaug_ref[chunk_rows(c), :]
            for r in chain_rows:
                s_ref[r, :] = _dot_nt(q_aug[r], k_chunk)

        def softmax_value(c, s_ref):
            v_chunk = vaug_ref[chunk_rows(c), :]
            for r, sr in zip(chain_rows, state_rows):
                s = s_ref[r, :]
                m_old = m_ref[sr, :]
                m_new = jnp.maximum(m_old, jnp.max(s, axis=-1, keepdims=True))
                p = jnp.exp2((s - m_new).astype(BF16))
                acc_ref[sr, :] = jnp.exp2(m_old - m_new) * acc_ref[sr, :] + _dot(p, v_chunk)
                m_ref[sr, :] = m_new

        def step_chunk(j):
            return jnp.where(j == 0, own_chunk, j - 1)

        n_past = own_chunk
        score_own(sa_ref)

        def two_steps(i, carry):
            j = 2 * i + 1
            score_past(j - 1, sb_ref)
            softmax_value(step_chunk(j - 1), sa_ref)
            score_past(j, sa_ref)
            softmax_value(j - 1, sb_ref)
            return carry

        lax.fori_loop(0, pair, two_steps, 0)
        if half == 1:
            score_past(n_past - 1, sb_ref)
            softmax_value(step_chunk(n_past - 1), sa_ref)
            softmax_value(n_past - 1, sb_ref)
        else:
            softmax_value(step_chunk(n_past), sa_ref)

        for u in range(chunk_blocks):
            out = None
            for h in range(HEADS_PER_LANE_GROUP):
                acc = acc_ref[state_rows[u * HEADS_PER_LANE_GROUP + h], :]
                o = acc[:, :LANES] / acc[:, LANES:LANES + 1]
                out = o if out is None else jnp.where(_head_lane_mask(LANES, h), o, out)
            o_ref[0, half * span + u * blk:half * span + (u + 1) * blk, :] = out.astype(o_ref.dtype)

    sweep(0, s0_ref, s1_ref)
    sweep(1, s1_ref, s0_ref)


def _moba(q, k, v, k_means):
    bsz, seq, _ = q.shape
    n_blocks = seq // MOBA_BLOCK
    groups = MIX_WIDTH // LANES
    chunk_blocks = MOBA_CHUNK_BLOCKS
    while n_blocks % (2 * chunk_blocks):
        chunk_blocks //= 2
    assert chunk_blocks >= 1, "the sequence must hold an even number of MoBA blocks"
    span = chunk_blocks * MOBA_BLOCK
    rows = HEADS_PER_LANE_GROUP * span
    return pl.pallas_call(
        functools.partial(_moba_kernel, chunk_blocks=chunk_blocks),
        grid=(bsz, groups, seq // (2 * span)),
        in_specs=[
            pl.BlockSpec((1, 2 * span, LANES), lambda bi, gi, qi: (bi, qi, gi)),
            pl.BlockSpec((1, seq, LANES), lambda bi, gi, qi: (bi, 0, gi)),
            pl.BlockSpec((1, seq, LANES), lambda bi, gi, qi: (bi, 0, gi)),
            pl.BlockSpec((1, LANES, LANES), lambda bi, gi, qi: (bi, 0, gi)),
        ],
        out_specs=pl.BlockSpec((1, 2 * span, LANES), lambda bi, gi, qi: (bi, qi, gi)),
        out_shape=jax.ShapeDtypeStruct(q.shape, BF16),
        scratch_shapes=[pltpu.VMEM((seq, 2 * LANES), BF16), pltpu.VMEM((seq, 2 * LANES), BF16),
                        pltpu.VMEM((2 * rows, 1), F32), pltpu.VMEM((2 * rows, 2 * LANES), F32),
                        pltpu.VMEM((rows, span), F32), pltpu.VMEM((rows, span), F32)],
        compiler_params=_params(("arbitrary", "arbitrary", "arbitrary")),
        name="moba",
    )(q, k, v, k_means)


def _mixer_out_kernel(x_ref, main_ref, mo_ref, wo_ref, g_ref, b_ref, o_ref, *, alpha):
    mix = _dot(main_ref[0], wo_ref[:MIX_WIDTH, :]) + _dot(mo_ref[0], wo_ref[MIX_WIDTH:, :])
    o_ref[0] = _layer_norm(alpha * x_ref[0] + mix, g_ref[...], b_ref[...])


def _mixer_out(x, main, mem_out, w_o, g, b, alpha):
    bsz, seq, d = x.shape
    tile = min(SEQ_TILE, seq)
    full = lambda bi, ti: (0, 0)
    return pl.pallas_call(
        functools.partial(_mixer_out_kernel, alpha=alpha),
        grid=(bsz, seq // tile),
        in_specs=[
            pl.BlockSpec((1, tile, d), lambda bi, ti: (bi, ti, 0)),
            pl.BlockSpec((1, tile, MIX_WIDTH), lambda bi, ti: (bi, ti, 0)),
            pl.BlockSpec((1, tile, MEM_WIDTH), lambda bi, ti: (bi, ti, 0)),
            pl.BlockSpec(w_o.shape, full),
            pl.BlockSpec((1, d), full),
            pl.BlockSpec((1, d), full),
        ],
        out_specs=pl.BlockSpec((1, tile, d), lambda bi, ti: (bi, ti, 0)),
        out_shape=jax.ShapeDtypeStruct(x.shape, F32),
        compiler_params=_params(("arbitrary", "arbitrary")),
        name="mixer_out",
    )(x, main, mem_out, w_o, g, b)


def _swiglu_chunk(xb, wg, wu, wd):
    gp = _dot(xb, wg)
    up = _dot(xb, wu)
    return _dot((gp * jax.nn.sigmoid(gp) * up).astype(BF16), wd)


def _ffn_chunks(d_ff):
    chunk = FF_CHUNK if d_ff % FF_CHUNK == 0 else d_ff
    return chunk, d_ff // chunk


def _accumulate_chunks(c, n_chunks, part, acc_ref, finish):
    if n_chunks == 1:
        finish(part())
        return

    @pl.when(c == 0)
    def _():
        acc_ref[...] = part()

    if n_chunks > 2:
        @pl.when((c > 0) & (c < n_chunks - 1))
        def _():
            acc_ref[...] += part()

    @pl.when(c == n_chunks - 1)
    def _():
        finish(acc_ref[...] + part())


def _ffn_kernel(x_ref, wg_ref, wu_ref, wd_ref, g_ref, b_ref, o_ref, acc_ref, xb_ref, *, alpha,
                n_chunks):
    c = pl.program_id(1)

    @pl.when(c == 0)
    def _():
        xb_ref[...] = x_ref[...].astype(BF16)

    def finish(ffn):
        o_ref[...] = _layer_norm(alpha * x_ref[...] + ffn, g_ref[...], b_ref[...])

    def part():
        return _swiglu_chunk(xb_ref[...], wg_ref[...], wu_ref[...], wd_ref[...])

    _accumulate_chunks(c, n_chunks, part, acc_ref, finish)


def _ffn(x, w_gu, w_down, layer, g, b, alpha):
    n, d = x.shape
    d_ff = w_down.shape[1]
    tile = min(FFN_TILE, n)
    chunk, n_chunks = _ffn_chunks(d_ff)
    return pl.pallas_call(
        functools.partial(_ffn_kernel, alpha=alpha, n_chunks=n_chunks),
        grid=(n // tile, n_chunks),
        in_specs=[
            pl.BlockSpec((tile, d), lambda i, c: (i, 0)),
            pl.BlockSpec((None, d, chunk), lambda i, c: (layer, 0, c)),
            pl.BlockSpec((None, d, chunk), lambda i, c: (layer, 0, c + n_chunks)),
            pl.BlockSpec((None, chunk, d), lambda i, c: (layer, c, 0)),
            pl.BlockSpec((1, d), lambda i, c: (0, 0)),
            pl.BlockSpec((1, d), lambda i, c: (0, 0)),
        ],
        out_specs=pl.BlockSpec((tile, d), lambda i, c: (i, 0)),
        out_shape=jax.ShapeDtypeStruct(x.shape, F32),
        scratch_shapes=[pltpu.VMEM((tile, d), F32), pltpu.VMEM((tile, d), BF16)],
        compiler_params=_params(("arbitrary", "arbitrary")),
        name="ffn_dense",
    )(x, w_gu, w_gu, w_down, g, b)


META_EXPERT, META_RANK, META_GATE, META_ROWS = 0, 2, 4, 8
EXPERT_ROWS = 16


def _split_bf16(a):
    hi = a.astype(BF16)
    return hi, (a - hi.astype(F32)).astype(BF16)


def _route_kernel(x_ref, wr_ref, meta_ref, counts_ref, *, n_experts):
    tile = x_ref.shape[0]

    @pl.when(pl.program_id(0) == 0)
    def _():
        counts_ref[...] = jnp.zeros_like(counts_ref)

    x_hi, x_lo = _split_bf16(x_ref[...])
    w_hi, w_lo = _split_bf16(wr_ref[...])
    hi_part = _dot_nt(jnp.concatenate([w_hi, w_lo], axis=0), x_hi)
    logits = hi_part[:EXPERT_ROWS] + (hi_part[EXPERT_ROWS:] + _dot_nt(w_hi, x_lo))
    expert = lax.broadcasted_iota(jnp.int32, (EXPERT_ROWS, tile), 0)
    logits = jnp.where(expert < n_experts, logits, -jnp.inf)
    top1 = jnp.max(logits, axis=0, keepdims=True)
    idx1 = jnp.min(jnp.where(logits == top1, expert, EXPERT_ROWS), axis=0, keepdims=True)
    rest = jnp.where(expert == idx1, -jnp.inf, logits)
    top2 = jnp.max(rest, axis=0, keepdims=True)
    idx2 = jnp.min(jnp.where(rest == top2, expert, EXPERT_ROWS), axis=0, keepdims=True)
    e2 = jnp.exp(top2 - top1)

    chosen = (expert == idx1) | (expert == idx2)
    earlier = (lax.broadcasted_iota(jnp.int32, (tile, tile), 0)
               < lax.broadcasted_iota(jnp.int32, (tile, tile), 1))
    rank = counts_ref[...] + _dot(chosen.astype(BF16), earlier.astype(BF16))
    rank1 = jnp.sum(jnp.where(expert == idx1, rank, 0.0), axis=0, keepdims=True)
    rank2 = jnp.sum(jnp.where(expert == idx2, rank, 0.0), axis=0, keepdims=True)
    counts_ref[...] += jnp.sum(chosen.astype(F32), axis=1, keepdims=True)

    fields = [idx1.astype(F32), idx2.astype(F32), rank1, rank2, 1.0 / (1.0 + e2), e2 / (1.0 + e2)]
    row = lax.broadcasted_iota(jnp.int32, (META_ROWS, tile), 0)
    meta = jnp.zeros((META_ROWS, tile), F32)
    for i, field in enumerate(fields):
        meta = jnp.where(row == i, field, meta)
    meta_ref[...] = meta


def _route(x, w_router):
    n, d = x.shape
    tile = min(ROUTE_TILE, n)
    n_experts = w_router.shape[1]
    return pl.pallas_call(
        functools.partial(_route_kernel, n_experts=n_experts),
        grid=(n // tile,),
        in_specs=[pl.BlockSpec((tile, d), lambda i: (i, 0)),
                  pl.BlockSpec((EXPERT_ROWS, d), lambda i: (0, 0))],
        out_specs=[pl.BlockSpec((META_ROWS, tile), lambda i: (0, i)),
                   pl.BlockSpec((EXPERT_ROWS, 1), lambda i: (0, 0))],
        out_shape=[jax.ShapeDtypeStruct((META_ROWS, n), F32),
                   jax.ShapeDtypeStruct((EXPERT_ROWS, 1), F32)],
        compiler_params=_params(("arbitrary",)),
        name="moe_route",
    )(x, jnp.pad(w_router.T, ((0, EXPERT_ROWS - n_experts), (0, 0))))


def _row_copy(src_ref, src_row, dst_ref, dst_row, sem):
    return pltpu.make_async_copy(src_ref.at[pl.ds(src_row, 1), :], dst_ref.at[pl.ds(dst_row, 1), :], sem)


def _start_then_wait(copies, n):
    def start(r, carry):
        for k, cp in enumerate(copies(r)):
            cp.start(priority=k % DMA_PRIORITIES)
        return carry

    def wait(r, carry):
        for cp in copies(r):
            cp.wait()
        return carry

    lax.fori_loop(0, n, start, 0, unroll=DMA_LOOP_UNROLL)
    lax.fori_loop(0, n, wait, 0, unroll=DMA_LOOP_UNROLL)


def _dispatch_kernel(pos_ref, x_ref, init_ref, xs_ref, sem):
    del init_ref

    def copies(r):
        return [_row_copy(x_ref, r, xs_ref, pos_ref[0, 0, TOP_K * r + k], sem) for k in range(TOP_K)]

    _start_then_wait(copies, x_ref.shape[0])


def _dispatch(x, pos, grouped_init):
    n, d = x.shape
    tile = min(ROUTE_TILE, n)
    any_spec = pl.BlockSpec(memory_space=pl.ANY)
    return pl.pallas_call(
        _dispatch_kernel,
        grid=(n // tile,),
        in_specs=[pl.BlockSpec((1, 1, TOP_K * tile), lambda i: (i, 0, 0), memory_space=pltpu.SMEM),
                  pl.BlockSpec((tile, d), lambda i: (i, 0)), any_spec],
        out_specs=any_spec,
        out_shape=jax.ShapeDtypeStruct(grouped_init.shape, F32),
        scratch_shapes=[pltpu.SemaphoreType.DMA],
        input_output_aliases={2: 0},
        compiler_params=_params(("arbitrary",)),
        name="moe_dispatch",
    )(pos.reshape(n // tile, 1, TOP_K * tile), x, grouped_init)


def _expert_ffn_kernel(tile_expert_ref, n_tiles_ref, xs_ref, wg_ref, wu_ref, wd_ref, o_ref,
                       acc_ref, xb_ref, *, n_chunks):
    del tile_expert_ref
    c = pl.program_id(1)

    @pl.when(pl.program_id(0) < n_tiles_ref[0])
    def _():
        @pl.when(c == 0)
        def _():
            xb_ref[...] = xs_ref[...].astype(BF16)

        def finish(ffn):
            o_ref[...] = ffn

        def part():
            return _swiglu_chunk(xb_ref[...], wg_ref[0], wu_ref[0], wd_ref[0])

        _accumulate_chunks(c, n_chunks, part, acc_ref, finish)


def _expert_ffn(xs, tile_expert, n_tiles, w_gu, w_down, layer):
    n_rows, d = xs.shape
    d_ff = w_down.shape[2]
    chunk, n_chunks = _ffn_chunks(d_ff)
    row_tile = lambda t, c, te, nt: (jnp.minimum(t, nt[0] - 1), 0)
    grid_spec = pltpu.PrefetchScalarGridSpec(
        num_scalar_prefetch=2,
        grid=(n_rows // EXPERT_TILE, n_chunks),
        in_specs=[
            pl.BlockSpec((EXPERT_TILE, d), row_tile),
            pl.BlockSpec((None, 1, d, chunk), lambda t, c, te, nt: (layer, te[t], 0, c)),
            pl.BlockSpec((None, 1, d, chunk), lambda t, c, te, nt: (layer, te[t], 0, c + n_chunks)),
            pl.BlockSpec((None, 1, chunk, d), lambda t, c, te, nt: (layer, te[t], c, 0)),
        ],
        out_specs=pl.BlockSpec((EXPERT_TILE, d), row_tile),
        scratch_shapes=[pltpu.VMEM((EXPERT_TILE, d), F32), pltpu.VMEM((EXPERT_TILE, d), BF16)],
    )
    return pl.pallas_call(
        functools.partial(_expert_ffn_kernel, n_chunks=n_chunks),
        grid_spec=grid_spec,
        out_shape=jax.ShapeDtypeStruct((n_rows, d), F32),
        compiler_params=_params(("arbitrary", "arbitrary")),
        name="moe_experts",
    )(tile_expert, n_tiles, xs, w_gu, w_gu, w_down)


def _combine_kernel(pos_ref, gates_ref, x_ref, g_ref, b_ref, ys_ref, o_ref, rows_ref, sem, *, alpha):
    tile = x_ref.shape[0]

    def copies(r):
        return [_row_copy(ys_ref, pos_ref[0, 0, TOP_K * r + k], rows_ref.at[k], r, sem)
                for k in range(TOP_K)]

    _start_then_wait(copies, tile)

    gates = gates_ref[...]
    ffn = jnp.zeros(x_ref.shape, F32)
    for k in range(TOP_K):
        ffn = ffn + gates[:, k:k + 1] * rows_ref[k]
    o_ref[...] = _layer_norm(alpha * x_ref[...] + ffn, g_ref[...], b_ref[...])


def _combine(x, gates, pos, ys, g, b, alpha):
    n, d = x.shape
    tile = min(COMBINE_TILE, n)
    return pl.pallas_call(
        functools.partial(_combine_kernel, alpha=alpha),
        grid=(n // tile,),
        in_specs=[pl.BlockSpec((1, 1, TOP_K * tile), lambda i: (i, 0, 0), memory_space=pltpu.SMEM),
                  pl.BlockSpec((tile, TOP_K), lambda i: (i, 0)),
                  pl.BlockSpec((tile, d), lambda i: (i, 0)),
                  pl.BlockSpec((1, d), lambda i: (0, 0)),
                  pl.BlockSpec((1, d), lambda i: (0, 0)),
                  pl.BlockSpec(memory_space=pl.ANY)],
        out_specs=pl.BlockSpec((tile, d), lambda i: (i, 0)),
        out_shape=jax.ShapeDtypeStruct(x.shape, F32),
        scratch_shapes=[pltpu.VMEM((TOP_K, tile, d), F32), pltpu.SemaphoreType.DMA],
        compiler_params=_params(("arbitrary",)),
        name="moe_combine",
    )(pos.reshape(n // tile, 1, TOP_K * tile), gates, x, g, b, ys)


def _moe(x, w_router, w_gu, w_down, layer, g, b, alpha, grouped_init):
    n, d = x.shape
    n_experts = w_router.shape[1]
    meta, counts = _route(x, w_router)

    counts = counts[:n_experts, 0].astype(jnp.int32)
    group_tiles = (counts + EXPERT_TILE - 1) // EXPERT_TILE
    tiles_end = jnp.cumsum(group_tiles)
    group_row0 = (tiles_end - group_tiles) * EXPERT_TILE
    n_grid_tiles = (TOP_K * n) // EXPERT_TILE + n_experts
    tile_ids = jnp.arange(n_grid_tiles, dtype=jnp.int32)
    tile_expert = jnp.minimum(jnp.sum((tile_ids[:, None] >= tiles_end[None, :]).astype(jnp.int32), axis=1),
                              n_experts - 1)
    records = meta.T
    expert = records[:, META_EXPERT:META_EXPERT + TOP_K].astype(jnp.int32)
    rank = records[:, META_RANK:META_RANK + TOP_K].astype(jnp.int32)
    gates = records[:, META_GATE:META_GATE + TOP_K]
    pos = (group_row0[expert] + rank).reshape(-1)

    if grouped_init is None:
        grouped_init = jnp.zeros((n_grid_tiles * EXPERT_TILE, d), F32)
    xs = _dispatch(x, pos, grouped_init)
    ys = _expert_ffn(xs, tile_expert, tiles_end[-1:].astype(jnp.int32), w_gu, w_down, layer)
    return _combine(x, gates, pos, ys, g, b, alpha), xs


def _rope_lane_frequencies():
    inv_freq = ROPE_THETA ** (-jnp.arange(0, ROPE_DIM, 2, dtype=F32) / ROPE_DIM)
    c = jnp.arange(LANES) % HEAD_DIM
    return jnp.where(c < ROPE_DIM, inv_freq[c % (ROPE_DIM // 2)], 0.0).astype(F32)[None, :]


def kernel(x, mem, positions, w_in_a, conv_a, w_q_b, w_kv_shared, w_mem_kv, w_o, ln1_g, ln1_b,
           ln2_g, ln2_b, w_gu_dense, w_down_dense, w_router, w_gu_moe, w_down_moe):
    bsz, seq, d = x.shape
    depth = w_o.shape[0]
    n_a = w_in_a.shape[0]
    alpha = float((2 * depth) ** 0.25)
    assert seq % MOBA_BLOCK == 0 and seq // MOBA_BLOCK <= LANES

    pos = positions.astype(F32)[:, :, None]
    freq = _rope_lane_frequencies()
    mem_len = mem.shape[1]
    w_mem_all = jnp.transpose(w_mem_kv, (1, 0, 2)).reshape(d, depth * 2 * MEM_WIDTH).astype(BF16)
    mem_kv = _matmul(mem.reshape(bsz * mem_len, d), w_mem_all, BF16).reshape(bsz, mem_len, -1)

    w_gu_dense_bf16, w_down_dense_bf16 = w_gu_dense.astype(BF16), w_down_dense.astype(BF16)
    w_gu_moe_bf16, w_down_moe_bf16 = w_gu_moe.astype(BF16), w_down_moe.astype(BF16)
    shared = None
    grouped = None
    for layer in range(depth):
        g1, b1 = ln1_g[layer][None, :], ln1_b[layer][None, :]
        g2, b2 = ln2_g[layer][None, :], ln2_b[layer][None, :]
        wo = w_o[layer].astype(BF16)
        if layer < n_a:
            x = _mixer_a(x, w_in_a[layer].astype(BF16), conv_a[layer], mem_kv, layer, wo, g1, b1, alpha)
        else:
            wq = w_q_b[layer - n_a]
            q, mem_out = _query_b(x, pos, freq, wq.astype(BF16), mem_kv, layer)
            main = _moba(q, *shared)
            x = _mixer_out(x, main, mem_out, wo, g1, b1, alpha)
        xf = x.reshape(bsz * seq, d)
        if layer % 2 == 0:
            xf = _ffn(xf, w_gu_dense_bf16, w_down_dense_bf16, layer // 2, g2, b2, alpha)
        else:
            xf, grouped = _moe(xf, w_router[layer // 2], w_gu_moe_bf16, w_down_moe_bf16, layer // 2,
                               g2, b2, alpha, grouped)
        x = xf.reshape(bsz, seq, d)
        if layer == n_a - 1:
            wk = w_kv_shared[:, :MIX_WIDTH]
            k, v, km = _shared_kv(x, pos, freq, wk.astype(BF16), w_kv_shared[:, MIX_WIDTH:].astype(BF16))
            n_blocks = seq // MOBA_BLOCK
            km = jnp.pad(km.reshape(bsz, n_blocks, MIX_WIDTH), ((0, 0), (0, LANES - n_blocks), (0, 0)))
            shared = (k, v, km)
    return x
```

```python
import functools

import jax
import jax.numpy as jnp
from jax import lax
from jax.experimental import pallas as pl
from jax.experimental.pallas import tpu as pltpu

F32 = jnp.float32
BF16 = jnp.bfloat16

HEAD_DIM = 64
MIX_HEADS = 12
MIX_WIDTH = MIX_HEADS * HEAD_DIM
MEM_HEADS = 4
MEM_WIDTH = MEM_HEADS * HEAD_DIM
CONV_WIDTH = 3
ROPE_DIM = HEAD_DIM // 4
ROPE_THETA = 500000.0
MOBA_BLOCK = 256
MOBA_TOPK = 3
TOP_K = 2
LN_EPS = 1e-5
ATTN_SCALE = HEAD_DIM ** -0.5
LOG2_E = 1.4426950408889634

LANES = 128
HEADS_PER_LANE_GROUP = LANES // HEAD_DIM
MASK_VALUE = -1e30
VMEM_LIMIT_BYTES = 56 * 1024 * 1024

SEQ_TILE = 512
FFN_TILE = 1024
FF_CHUNK = 1408
MOBA_CHUNK_BLOCKS = 4
ROUTE_TILE = 1024
EXPERT_TILE = 512
COMBINE_TILE = 1024
DMA_LOOP_UNROLL = 8


def _params(semantics):
    return pltpu.CompilerParams(dimension_semantics=semantics, vmem_limit_bytes=VMEM_LIMIT_BYTES)


def _layer_norm(r, g, b):
    mu = jnp.mean(r, axis=-1, keepdims=True)
    c = r - mu
    var = jnp.mean(c * c, axis=-1, keepdims=True)
    return c * lax.rsqrt(var + LN_EPS) * g + b


def _dot(a, b):
    return jnp.dot(a, b, preferred_element_type=F32)


def _dot_nt(a, b):
    return lax.dot_general(a, b, (((1,), (1,)), ((), ())), preferred_element_type=F32)


def _head_lane_mask(width, head):
    lane = lax.broadcasted_iota(jnp.int32, (1, width), 1)
    return (lane >= head * HEAD_DIM) & (lane < (head + 1) * HEAD_DIM)


def _mem_attention(q_mem, mem_k, mem_v):
    out = jnp.zeros(q_mem.shape, F32)
    for head in range(MEM_HEADS):
        hm = _head_lane_mask(MEM_WIDTH, head)
        qh = jnp.where(hm, q_mem * ATTN_SCALE, 0.0).astype(BF16)
        s = _dot_nt(qh, mem_k)
        m = jnp.max(s, axis=-1, keepdims=True)
        p = jnp.exp(s - m)
        l = jnp.sum(p, axis=-1, keepdims=True)
        o = _dot(p.astype(BF16), mem_v) / l
        out = jnp.where(hm, o, out)
    return out


def _rope_tables(pos, freq, reps):
    angle = pos * freq
    cos = jnp.cos(angle)
    sin = jnp.sin(angle)
    return (jnp.concatenate([cos] * reps, axis=1), jnp.concatenate([sin] * reps, axis=1))


def _matmul_kernel(x_ref, w_ref, o_ref):
    o_ref[...] = _dot(x_ref[...].astype(BF16), w_ref[...]).astype(o_ref.dtype)


def _matmul(x, w, out_dtype, tile_m=256):
    m, k = x.shape
    n = w.shape[1]
    return pl.pallas_call(
        _matmul_kernel,
        grid=(m // tile_m,),
        in_specs=[pl.BlockSpec((tile_m, k), lambda i: (i, 0)),
                  pl.BlockSpec((k, n), lambda i: (0, 0))],
        out_specs=pl.BlockSpec((tile_m, n), lambda i: (i, 0)),
        out_shape=jax.ShapeDtypeStruct((m, n), out_dtype),
        compiler_params=_params(("arbitrary",)),
        name="mem_kv_proj",
    )(x, w)


def _mixer_a_kernel(x_ref, win_ref, conv_ref, mk_ref, mv_ref, wo_ref, g_ref, b_ref, o_ref,
                    ztail_ref, *, alpha):
    tile = x_ref.shape[1]

    @pl.when(pl.program_id(1) == 0)
    def _():
        ztail_ref[...] = jnp.zeros_like(ztail_ref)

    x = x_ref[0]
    proj = _dot(x.astype(BF16), win_ref[...])
    u_b = proj[:, :MIX_WIDTH]
    u_c = proj[:, MIX_WIDTH:2 * MIX_WIDTH]
    u_h = proj[:, 2 * MIX_WIDTH:3 * MIX_WIDTH]
    q_mem = proj[:, 3 * MIX_WIDTH:]

    z = u_c * u_h
    tail = ztail_ref[...]
    ztail_ref[...] = z[tile - 8:, :]
    row = lax.broadcasted_iota(jnp.int32, z.shape, 0)
    z1 = jnp.where(row == 0, tail[7:8, :], pltpu.roll(z, 1, 0))
    z2 = jnp.where(row == 0, tail[6:7, :], jnp.where(row == 1, tail[7:8, :], pltpu.roll(z, 2, 0)))
    w = conv_ref[...]
    main = u_b * (w[0:1, :] * z2 + w[1:2, :] * z1 + w[2:3, :] * z)

    mem_out = _mem_attention(q_mem, mk_ref[0], mv_ref[0])
    mix = (_dot(main.astype(BF16), wo_ref[:MIX_WIDTH, :])
           + _dot(mem_out.astype(BF16), wo_ref[MIX_WIDTH:, :]))
    o_ref[0] = _layer_norm(alpha * x + mix, g_ref[...], b_ref[...])


def _mixer_a(x, w_in, conv_w, mem_kv, layer, w_o, g, b, alpha):
    bsz, seq, d = x.shape
    tile = min(SEQ_TILE, seq)
    mem_len = mem_kv.shape[1]
    return pl.pallas_call(
        functools.partial(_mixer_a_kernel, alpha=alpha),
        grid=(bsz, seq // tile),
        in_specs=[
            pl.BlockSpec((1, tile, d), lambda bi, ti: (bi, ti, 0)),
            pl.BlockSpec(w_in.shape, lambda bi, ti: (0, 0)),
            pl.BlockSpec(conv_w.shape, lambda bi, ti: (0, 0)),
            pl.BlockSpec((1, mem_len, MEM_WIDTH), lambda bi, ti: (bi, 0, 2 * layer)),
            pl.BlockSpec((1, mem_len, MEM_WIDTH), lambda bi, ti: (bi, 0, 2 * layer + 1)),
            pl.BlockSpec(w_o.shape, lambda bi, ti: (0, 0)),
            pl.BlockSpec((1, d), lambda bi, ti: (0, 0)),
            pl.BlockSpec((1, d), lambda bi, ti: (0, 0)),
        ],
        out_specs=pl.BlockSpec((1, tile, d), lambda bi, ti: (bi, ti, 0)),
        out_shape=jax.ShapeDtypeStruct(x.shape, F32),
        scratch_shapes=[pltpu.VMEM((8, MIX_WIDTH), F32)],
        compiler_params=_params(("arbitrary", "arbitrary")),
        name="mixer_a",
    )(x, w_in, conv_w, mem_kv, mem_kv, w_o, g, b)


def _partial_rope(t, pos, freq):
    half = ROPE_DIM // 2
    cos, sin = _rope_tables(pos, freq, MIX_WIDTH // LANES)
    dim = lax.broadcasted_iota(jnp.int32, (1, MIX_WIDTH), 1) % HEAD_DIM
    upper = pltpu.roll(t, MIX_WIDTH - half, 1)
    lower = pltpu.roll(t, half, 1)
    return t * cos + jnp.where(dim < half, -upper, lower) * sin


def _shared_kv_kernel(x_ref, pos_ref, freq_ref, wk_ref, wv_ref, k_ref, v_ref, km_ref):
    xb = x_ref[0].astype(BF16)
    k = _partial_rope(_dot(xb, wk_ref[...]), pos_ref[0], freq_ref[...])
    k_ref[0] = k.astype(BF16)
    v_ref[0] = _dot(xb, wv_ref[...]).astype(BF16)
    km_ref[0, 0] = jnp.mean(k, axis=0, keepdims=True)


def _shared_kv(x, pos, freq, w_k, w_v):
    bsz, seq, d = x.shape
    n_blocks = seq // MOBA_BLOCK
    full = lambda bi, ti: (0, 0)
    return pl.pallas_call(
        _shared_kv_kernel,
        grid=(bsz, n_blocks),
        in_specs=[
            pl.BlockSpec((1, MOBA_BLOCK, d), lambda bi, ti: (bi, ti, 0)),
            pl.BlockSpec((1, MOBA_BLOCK, 1), lambda bi, ti: (bi, ti, 0)),
            pl.BlockSpec(freq.shape, full),
            pl.BlockSpec(w_k.shape, full),
            pl.BlockSpec(w_v.shape, full),
        ],
        out_specs=[
            pl.BlockSpec((1, MOBA_BLOCK, MIX_WIDTH), lambda bi, ti: (bi, ti, 0)),
            pl.BlockSpec((1, MOBA_BLOCK, MIX_WIDTH), lambda bi, ti: (bi, ti, 0)),
            pl.BlockSpec((1, 1, 1, MIX_WIDTH), lambda bi, ti: (bi, ti, 0, 0)),
        ],
        out_shape=[
            jax.ShapeDtypeStruct((bsz, seq, MIX_WIDTH), BF16),
            jax.ShapeDtypeStruct((bsz, seq, MIX_WIDTH), BF16),
            jax.ShapeDtypeStruct((bsz, n_blocks, 1, MIX_WIDTH), F32),
        ],
        compiler_params=_params(("arbitrary", "arbitrary")),
        name="shared_kv",
    )(x, pos, freq, w_k, w_v)


def _query_b_kernel(x_ref, pos_ref, freq_ref, wq_ref, mk_ref, mv_ref, q_ref, mo_ref):
    xb = x_ref[0].astype(BF16)
    proj = _dot(xb, wq_ref[...])
    q = _partial_rope(proj[:, :MIX_WIDTH], pos_ref[0], freq_ref[...])
    q_ref[0] = (q * (ATTN_SCALE * LOG2_E)).astype(BF16)
    mo_ref[0] = _mem_attention(proj[:, MIX_WIDTH:], mk_ref[0], mv_ref[0]).astype(BF16)


def _query_b(x, pos, freq, w_q, mem_kv, layer):
    bsz, seq, d = x.shape
    tile = min(SEQ_TILE, seq)
    mem_len = mem_kv.shape[1]
    full = lambda bi, ti: (0, 0)
    return pl.pallas_call(
        _query_b_kernel,
        grid=(bsz, seq // tile),
        in_specs=[
            pl.BlockSpec((1, tile, d), lambda bi, ti: (bi, ti, 0)),
            pl.BlockSpec((1, tile, 1), lambda bi, ti: (bi, ti, 0)),
            pl.BlockSpec(freq.shape, full),
            pl.BlockSpec(w_q.shape, full),
            pl.BlockSpec((1, mem_len, MEM_WIDTH), lambda bi, ti: (bi, 0, 2 * layer)),
            pl.BlockSpec((1, mem_len, MEM_WIDTH), lambda bi, ti: (bi, 0, 2 * layer + 1)),
        ],
        out_specs=[
            pl.BlockSpec((1, tile, MIX_WIDTH), lambda bi, ti: (bi, ti, 0)),
            pl.BlockSpec((1, tile, MEM_WIDTH), lambda bi, ti: (bi, ti, 0)),
        ],
        out_shape=[
            jax.ShapeDtypeStruct((bsz, seq, MIX_WIDTH), BF16),
            jax.ShapeDtypeStruct((bsz, seq, MEM_WIDTH), BF16),
        ],
        compiler_params=_params(("arbitrary", "arbitrary")),
        name="query_b",
    )(x, pos, freq, w_q, mem_kv, mem_kv)


def _moba_kernel(q_ref, k_ref, v_ref, km_ref, o_ref, kaug_ref, vaug_ref, m_ref, acc_ref, s0_ref,
                 s1_ref, *, chunk_blocks):
    blk = MOBA_BLOCK
    span = chunk_blocks * blk
    pair = pl.program_id(2)
    chains = [(u, h) for u in range(chunk_blocks) for h in range(HEADS_PER_LANE_GROUP)]
    rows = len(chains) * blk

    @pl.when(pair == 0)
    def _():
        kaug_ref[:, :LANES] = k_ref[0]
        vaug_ref[:, :LANES] = v_ref[0]
        lane = lax.broadcasted_iota(jnp.int32, (blk, LANES), 1)
        ones_lane = (lane == 0).astype(BF16)

        def fill(j, carry):
            off = pl.multiple_of(j * blk, blk)
            kaug_ref[pl.ds(off, blk), LANES:] = (lane == j).astype(BF16)
            vaug_ref[pl.ds(off, blk), LANES:] = ones_lane
            return carry

        lax.fori_loop(0, k_ref.shape[1] // blk, fill, 0)

    def chunk_rows(c):
        return pl.ds(pl.multiple_of(c * span, span), span)

    def sweep(half, sa_ref, sb_ref):
        own_chunk = 2 * pair + half
        q = q_ref[0, half * span:(half + 1) * span, :]
        q_heads = jnp.concatenate(
            [jnp.where(_head_lane_mask(LANES, h), q[u * blk:(u + 1) * blk], 0).astype(BF16)
             for u, h in chains], axis=0)

        block_id = lax.broadcasted_iota(jnp.int32, (rows, LANES), 1)
        own = (own_chunk * chunk_blocks
               + lax.broadcasted_iota(jnp.int32, (rows, LANES), 0) // (HEADS_PER_LANE_GROUP * blk))
        is_past = block_id < own
        gate = jnp.where(is_past, _dot_nt(q_heads, km_ref[0].astype(BF16)), -jnp.inf)
        kth = gate
        for _ in range(MOBA_TOPK - 1):
            kth = jnp.where(kth == jnp.max(kth, axis=-1, keepdims=True), -jnp.inf, kth)
        threshold = jnp.max(kth, axis=-1, keepdims=True)
        allowed = (is_past & (gate >= threshold)) | (block_id == own)
        block_bias = jnp.where(allowed, 0.0, MASK_VALUE).astype(BF16)
        q_aug = jnp.concatenate([q_heads, block_bias], axis=1)

        state_rows = [slice(half * rows + i * blk, half * rows + (i + 1) * blk)
                      for i in range(len(chains))]
        chain_rows = [slice(i * blk, (i + 1) * blk) for i in range(len(chains))]
        for r in state_rows:
            m_ref[r, :] = jnp.full((blk, 1), MASK_VALUE, F32)
            acc_ref[r, :] = jnp.zeros((blk, acc_ref.shape[1]), F32)

        def score_own(s_ref):
            for r, (u, _) in zip(chain_rows, chains):
                width = (u + 1) * blk
                k_part = kaug_ref[pl.ds(pl.multiple_of(own_chunk * span, span), width), :]
                k_pos = lax.broadcasted_iota(jnp.int32, (blk, width), 1)
                q_pos = u * blk + lax.broadcasted_iota(jnp.int32, (blk, width), 0)
                s_ref[r, :width] = jnp.where(k_pos <= q_pos, _dot_nt(q_aug[r], k_part), MASK_VALUE)
                if width < span:
                    s_ref[r, width:] = jnp.full((blk, span - width), MASK_VALUE, F32)

        def score_past(c, s_ref):
            k_chunk = kaug_ref[chunk_rows(c), :]
            for r in chain_rows:
                s_ref[r, :] = _dot_nt(q_aug[r], k_chunk)

        def softmax_value(c, s_ref):
            v_chunk = vaug_ref[chunk_rows(c), :]
            for r, sr in zip(chain_rows, state_rows):
                s = s_ref[r, :]
                m_old = m_ref[sr, :]
                m_new = jnp.maximum(m_old, jnp.max(s, axis=-1, keepdims=True))
                p = jnp.exp2((s - m_new).astype(BF16))
                acc_ref[sr, :] = jnp.exp2(m_old - m_new) * acc_ref[sr, :] + _dot(p, v_chunk)
                m_ref[sr, :] = m_new

        def step_chunk(j):
            return jnp.where(j == 0, own_chunk, j - 1)

        n_past = own_chunk
        score_own(sa_ref)

        def two_steps(i, carry):
            j = 2 * i + 1
            score_past(j - 1, sb_ref)
            softmax_value(step_chunk(j - 1), sa_ref)
            score_past(j, sa_ref)
            softmax_value(j - 1, sb_ref)
            return carry

        lax.fori_loop(0, pair, two_steps, 0)
        if half == 1:
            score_past(n_past - 1, sb_ref)
            softmax_value(step_chunk(n_past - 1), sa_ref)
            softmax_value(n_past - 1, sb_ref)
        else:
            softmax_value(step_chunk(n_past), sa_ref)

        for u in range(chunk_blocks):
            out = None
            for h in range(HEADS_PER_LANE_GROUP):
                acc = acc_ref[state_rows[u * HEADS_PER_LANE_GROUP + h], :]
                o = acc[:, :LANES] / acc[:, LANES:LANES + 1]
                out = o if out is None else jnp.where(_head_lane_mask(LANES, h), o, out)
            o_ref[0, half * span + u * blk:half * span + (u + 1) * blk, :] = out.astype(o_ref.dtype)

    sweep(0, s0_ref, s1_ref)
    sweep(1, s1_ref, s0_ref)


def _moba(q, k, v, k_means):
    bsz, seq, _ = q.shape
    n_blocks = seq // MOBA_BLOCK
    groups = MIX_WIDTH // LANES
    chunk_blocks = MOBA_CHUNK_BLOCKS
    while n_blocks % (2 * chunk_blocks):
        chunk_blocks //= 2
    assert chunk_blocks >= 1, "the sequence must hold an even number of MoBA blocks"
    span = chunk_blocks * MOBA_BLOCK
    rows = HEADS_PER_LANE_GROUP * span
    return pl.pallas_call(
        functools.partial(_moba_kernel, chunk_blocks=chunk_blocks),
        grid=(bsz, groups, seq // (2 * span)),
        in_specs=[
            pl.BlockSpec((1, 2 * span, LANES), lambda bi, gi, qi: (bi, qi, gi)),
            pl.BlockSpec((1, seq, LANES), lambda bi, gi, qi: (bi, 0, gi)),
            pl.BlockSpec((1, seq, LANES), lambda bi, gi, qi: (bi, 0, gi)),
            pl.BlockSpec((1, LANES, LANES), lambda bi, gi, qi: (bi, 0, gi)),
        ],
        out_specs=pl.BlockSpec((1, 2 * span, LANES), lambda bi, gi, qi: (bi, qi, gi)),
        out_shape=jax.ShapeDtypeStruct(q.shape, BF16),
        scratch_shapes=[pltpu.VMEM((seq, 2 * LANES), BF16), pltpu.VMEM((seq, 2 * LANES), BF16),
                        pltpu.VMEM((2 * rows, 1), F32), pltpu.VMEM((2 * rows, 2 * LANES), F32),
                        pltpu.VMEM((rows, span), F32), pltpu.VMEM((rows, span), F32)],
        compiler_params=_params(("arbitrary", "arbitrary", "arbitrary")),
        name="moba",
    )(q, k, v, k_means)


def _mixer_out_kernel(x_ref, main_ref, mo_ref, wo_ref, g_ref, b_ref, o_ref, *, alpha):
    mix = _dot(main_ref[0], wo_ref[:MIX_WIDTH, :]) + _dot(mo_ref[0], wo_ref[MIX_WIDTH:, :])
    o_ref[0] = _layer_norm(alpha * x_ref[0] + mix, g_ref[...], b_ref[...])


def _mixer_out(x, main, mem_out, w_o, g, b, alpha):
    bsz, seq, d = x.shape
    tile = min(SEQ_TILE, seq)
    full = lambda bi, ti: (0, 0)
    return pl.pallas_call(
        functools.partial(_mixer_out_kernel, alpha=alpha),
        grid=(bsz, seq // tile),
        in_specs=[
            pl.BlockSpec((1, tile, d), lambda bi, ti: (bi, ti, 0)),
            pl.BlockSpec((1, tile, MIX_WIDTH), lambda bi, ti: (bi, ti, 0)),
            pl.BlockSpec((1, tile, MEM_WIDTH), lambda bi, ti: (bi, ti, 0)),
            pl.BlockSpec(w_o.shape, full),
            pl.BlockSpec((1, d), full),
            pl.BlockSpec((1, d), full),
        ],
        out_specs=pl.BlockSpec((1, tile, d), lambda bi, ti: (bi, ti, 0)),
        out_shape=jax.ShapeDtypeStruct(x.shape, F32),
        compiler_params=_params(("arbitrary", "arbitrary")),
        name="mixer_out",
    )(x, main, mem_out, w_o, g, b)


def _swiglu_chunk(xb, wg, wu, wd):
    gp = _dot(xb, wg)
    up = _dot(xb, wu)
    return _dot((gp * jax.nn.sigmoid(gp) * up).astype(BF16), wd)


def _ffn_chunks(d_ff):
    chunk = FF_CHUNK if d_ff % FF_CHUNK == 0 else d_ff
    return chunk, d_ff // chunk


def _accumulate_chunks(c, n_chunks, part, acc_ref, finish):
    if n_chunks == 1:
        finish(part())
        return

    @pl.when(c == 0)
    def _():
        acc_ref[...] = part()

    if n_chunks > 2:
        @pl.when((c > 0) & (c < n_chunks - 1))
        def _():
            acc_ref[...] += part()

    @pl.when(c == n_chunks - 1)
    def _():
        finish(acc_ref[...] + part())


def _ffn_kernel(x_ref, wg_ref, wu_ref, wd_ref, g_ref, b_ref, o_ref, acc_ref, xb_ref, *, alpha,
                n_chunks):
    c = pl.program_id(1)

    @pl.when(c == 0)
    def _():
        xb_ref[...] = x_ref[...].astype(BF16)

    def finish(ffn):
        o_ref[...] = _layer_norm(alpha * x_ref[...] + ffn, g_ref[...], b_ref[...])

    def part():
        return _swiglu_chunk(xb_ref[...], wg_ref[...], wu_ref[...], wd_ref[...])

    _accumulate_chunks(c, n_chunks, part, acc_ref, finish)


def _ffn(x, w_gu, w_down, layer, g, b, alpha):
    n, d = x.shape
    d_ff = w_down.shape[1]
    tile = min(FFN_TILE, n)
    chunk, n_chunks = _ffn_chunks(d_ff)
    return pl.pallas_call(
        functools.partial(_ffn_kernel, alpha=alpha, n_chunks=n_chunks),
        grid=(n // tile, n_chunks),
        in_specs=[
            pl.BlockSpec((tile, d), lambda i, c: (i, 0)),
            pl.BlockSpec((None, d, chunk), lambda i, c: (layer, 0, c)),
            pl.BlockSpec((None, d, chunk), lambda i, c: (layer, 0, c + n_chunks)),
            pl.BlockSpec((None, chunk, d), lambda i, c: (layer, c, 0)),
            pl.BlockSpec((1, d), lambda i, c: (0, 0)),
            pl.BlockSpec((1, d), lambda i, c: (0, 0)),
        ],
        out_specs=pl.BlockSpec((tile, d), lambda i, c: (i, 0)),
        out_shape=jax.ShapeDtypeStruct(x.shape, F32),
        scratch_shapes=[pltpu.VMEM((tile, d), F32), pltpu.VMEM((tile, d), BF16)],
        compiler_params=_params(("arbitrary", "arbitrary")),
        name="ffn_dense",
    )(x, w_gu, w_gu, w_down, g, b)


META_EXPERT, META_RANK, META_GATE, META_ROWS = 0, 2, 4, 8
EXPERT_ROWS = 16


def _split_bf16(a):
    hi = a.astype(BF16)
    return hi, (a - hi.astype(F32)).astype(BF16)


def _route_kernel(x_ref, wr_ref, meta_ref, counts_ref, *, n_experts):
    tile = x_ref.shape[0]

    @pl.when(pl.program_id(0) == 0)
    def _():
        counts_ref[...] = jnp.zeros_like(counts_ref)

    x_hi, x_lo = _split_bf16(x_ref[...])
    w_hi, w_lo = _split_bf16(wr_ref[...])
    hi_part = _dot_nt(jnp.concatenate([w_hi, w_lo], axis=0), x_hi)
    logits = hi_part[:EXPERT_ROWS] + (hi_part[EXPERT_ROWS:] + _dot_nt(w_hi, x_lo))
    expert = lax.broadcasted_iota(jnp.int32, (EXPERT_ROWS, tile), 0)
    logits = jnp.where(expert < n_experts, logits, -jnp.inf)
    top1 = jnp.max(logits, axis=0, keepdims=True)
    idx1 = jnp.min(jnp.where(logits == top1, expert, EXPERT_ROWS), axis=0, keepdims=True)
    rest = jnp.where(expert == idx1, -jnp.inf, logits)
    top2 = jnp.max(rest, axis=0, keepdims=True)
    idx2 = jnp.min(jnp.where(rest == top2, expert, EXPERT_ROWS), axis=0, keepdims=True)
    e2 = jnp.exp(top2 - top1)

    chosen = (expert == idx1) | (expert == idx2)
    earlier = (lax.broadcasted_iota(jnp.int32, (tile, tile), 0)
               < lax.broadcasted_iota(jnp.int32, (tile, tile), 1))
    rank = counts_ref[...] + _dot(chosen.astype(BF16), earlier.astype(BF16))
    rank1 = jnp.sum(jnp.where(expert == idx1, rank, 0.0), axis=0, keepdims=True)
    rank2 = jnp.sum(jnp.where(expert == idx2, rank, 0.0), axis=0, keepdims=True)
    counts_ref[...] += jnp.sum(chosen.astype(F32), axis=1, keepdims=True)

    fields = [idx1.astype(F32), idx2.astype(F32), rank1, rank2, 1.0 / (1.0 + e2), e2 / (1.0 + e2)]
    row = lax.broadcasted_iota(jnp.int32, (META_ROWS, tile), 0)
    meta = jnp.zeros((META_ROWS, tile), F32)
    for i, field in enumerate(fields):
        meta = jnp.where(row == i, field, meta)
    meta_ref[...] = meta


def _route(x, w_router):
    n, d = x.shape
    tile = min(ROUTE_TILE, n)
    n_experts = w_router.shape[1]
    return pl.pallas_call(
        functools.partial(_route_kernel, n_experts=n_experts),
        grid=(n // tile,),
        in_specs=[pl.BlockSpec((tile, d), lambda i: (i, 0)),
                  pl.BlockSpec((EXPERT_ROWS, d), lambda i: (0, 0))],
        out_specs=[pl.BlockSpec((META_ROWS, tile), lambda i: (0, i)),
                   pl.BlockSpec((EXPERT_ROWS, 1), lambda i: (0, 0))],
        out_shape=[jax.ShapeDtypeStruct((META_ROWS, n), F32),
                   jax.ShapeDtypeStruct((EXPERT_ROWS, 1), F32)],
        compiler_params=_params(("arbitrary",)),
        name="moe_route",
    )(x, jnp.pad(w_router.T, ((0, EXPERT_ROWS - n_experts), (0, 0))))


def _row_copy(src_ref, src_row, dst_ref, dst_row, sem):
    return pltpu.make_async_copy(src_ref.at[pl.ds(src_row, 1), :], dst_ref.at[pl.ds(dst_row, 1), :], sem)


def _start_then_wait(copies, n):
    def start(r, carry):
        for cp in copies(r):
            cp.start()
        return carry

    def wait(r, carry):
        for cp in copies(r):
            cp.wait()
        return carry

    lax.fori_loop(0, n, start, 0, unroll=DMA_LOOP_UNROLL)
    lax.fori_loop(0, n, wait, 0, unroll=DMA_LOOP_UNROLL)


def _dispatch_kernel(pos_ref, x_ref, init_ref, xs_ref, sem):
    del init_ref

    def copies(r):
        return [_row_copy(x_ref, r, xs_ref, pos_ref[0, 0, TOP_K * r + k], sem) for k in range(TOP_K)]

    _start_then_wait(copies, x_ref.shape[0])


def _dispatch(x, pos, grouped_init):
    n, d = x.shape
    tile = min(ROUTE_TILE, n)
    any_spec = pl.BlockSpec(memory_space=pl.ANY)
    return pl.pallas_call(
        _dispatch_kernel,
        grid=(n // tile,),
        in_specs=[pl.BlockSpec((1, 1, TOP_K * tile), lambda i: (i, 0, 0), memory_space=pltpu.SMEM),
                  pl.BlockSpec((tile, d), lambda i: (i, 0)), any_spec],
        out_specs=any_spec,
        out_shape=jax.ShapeDtypeStruct(grouped_init.shape, F32),
        scratch_shapes=[pltpu.SemaphoreType.DMA],
        input_output_aliases={2: 0},
        compiler_params=_params(("arbitrary",)),
        name="moe_dispatch",
    )(pos.reshape(n // tile, 1, TOP_K * tile), x, grouped_init)


def _expert_ffn_kernel(tile_expert_ref, n_tiles_ref, xs_ref, wg_ref, wu_ref, wd_ref, o_ref,
                       acc_ref, xb_ref, *, n_chunks):
    del tile_expert_ref
    c = pl.program_id(1)
    in_use = pl.program_id(0) < n_tiles_ref[0]

    @pl.when(jnp.logical_not(in_use) & (c == n_chunks - 1))
    def _():
        o_ref[...] = jnp.zeros_like(o_ref)

    @pl.when(in_use)
    def _():
        @pl.when(c == 0)
        def _():
            xb_ref[...] = xs_ref[...].astype(BF16)

        def finish(ffn):
            o_ref[...] = ffn

        def part():
            return _swiglu_chunk(xb_ref[...], wg_ref[0], wu_ref[0], wd_ref[0])

        _accumulate_chunks(c, n_chunks, part, acc_ref, finish)


def _expert_ffn(xs, tile_expert, n_tiles, w_gu, w_down, layer):
    n_rows, d = xs.shape
    d_ff = w_down.shape[2]
    chunk, n_chunks = _ffn_chunks(d_ff)
    row_tile = lambda t, c, te, nt: (jnp.minimum(t, nt[0] - 1), 0)
    grid_spec = pltpu.PrefetchScalarGridSpec(
        num_scalar_prefetch=2,
        grid=(n_rows // EXPERT_TILE, n_chunks),
        in_specs=[
            pl.BlockSpec((EXPERT_TILE, d), row_tile),
            pl.BlockSpec((None, 1, d, chunk), lambda t, c, te, nt: (layer, te[t], 0, c)),
            pl.BlockSpec((None, 1, d, chunk), lambda t, c, te, nt: (layer, te[t], 0, c + n_chunks)),
            pl.BlockSpec((None, 1, chunk, d), lambda t, c, te, nt: (layer, te[t], c, 0)),
        ],
        out_specs=pl.BlockSpec((EXPERT_TILE, d), lambda t, c, te, nt: (t, 0)),
        scratch_shapes=[pltpu.VMEM((EXPERT_TILE, d), F32), pltpu.VMEM((EXPERT_TILE, d), BF16)],
    )
    return pl.pallas_call(
        functools.partial(_expert_ffn_kernel, n_chunks=n_chunks),
        grid_spec=grid_spec,
        out_shape=jax.ShapeDtypeStruct((n_rows, d), F32),
        compiler_params=_params(("arbitrary", "arbitrary")),
        name="moe_experts",
    )(tile_expert, n_tiles, xs, w_gu, w_gu, w_down)


def _combine_kernel(pos_ref, gates_ref, x_ref, g_ref, b_ref, ys_ref, o_ref, rows_ref, sem, *, alpha):
    tile = x_ref.shape[0]

    def copies(r):
        return [_row_copy(ys_ref, pos_ref[0, 0, TOP_K * r + k], rows_ref.at[k], r, sem)
                for k in range(TOP_K)]

    _start_then_wait(copies, tile)

    gates = gates_ref[...]
    ffn = jnp.zeros(x_ref.shape, F32)
    for k in range(TOP_K):
        ffn = ffn + gates[:, k:k + 1] * rows_ref[k]
    o_ref[...] = _layer_norm(alpha * x_ref[...] + ffn, g_ref[...], b_ref[...])


def _combine(x, gates, pos, ys, g, b, alpha):
    n, d = x.shape
    tile = min(COMBINE_TILE, n)
    return pl.pallas_call(
        functools.partial(_combine_kernel, alpha=alpha),
        grid=(n // tile,),
        in_specs=[pl.BlockSpec((1, 1, TOP_K * tile), lambda i: (i, 0, 0), memory_space=pltpu.SMEM),
                  pl.BlockSpec((tile, TOP_K), lambda i: (i, 0)),
                  pl.BlockSpec((tile, d), lambda i: (i, 0)),
                  pl.BlockSpec((1, d), lambda i: (0, 0)),
                  pl.BlockSpec((1, d), lambda i: (0, 0)),
                  pl.BlockSpec(memory_space=pl.ANY)],
        out_specs=pl.BlockSpec((tile, d), lambda i: (i, 0)),
        out_shape=jax.ShapeDtypeStruct(x.shape, F32),
        scratch_shapes=[pltpu.VMEM((TOP_K, tile, d), F32), pltpu.SemaphoreType.DMA],
        compiler_params=_params(("arbitrary",)),
        name="moe_combine",
    )(pos.reshape(n // tile, 1, TOP_K * tile), gates, x, g, b, ys)


def _moe(x, w_router, w_gu, w_down, layer, g, b, alpha, grouped_init):
    n, d = x.shape
    n_experts = w_router.shape[1]
    meta, counts = _route(x, w_router)

    counts = counts[:n_experts, 0].astype(jnp.int32)
    group_tiles = (counts + EXPERT_TILE - 1) // EXPERT_TILE
    tiles_end = jnp.cumsum(group_tiles)
    group_row0 = (tiles_end - group_tiles) * EXPERT_TILE
    n_grid_tiles = (TOP_K * n) // EXPERT_TILE + n_experts
    tile_ids = jnp.arange(n_grid_tiles, dtype=jnp.int32)
    tile_expert = jnp.minimum(jnp.sum((tile_ids[:, None] >= tiles_end[None, :]).astype(jnp.int32), axis=1),
                              n_experts - 1)
    records = meta.T
    expert = records[:, META_EXPERT:META_EXPERT + TOP_K].astype(jnp.int32)
    rank = records[:, META_RANK:META_RANK + TOP_K].astype(jnp.int32)
    gates = records[:, META_GATE:META_GATE + TOP_K]
    pos = (group_row0[expert] + rank).reshape(-1)

    if grouped_init is None:
        grouped_init = jnp.zeros((n_grid_tiles * EXPERT_TILE, d), F32)
    xs = _dispatch(x, pos, grouped_init)
    ys = _expert_ffn(xs, tile_expert, tiles_end[-1:].astype(jnp.int32), w_gu, w_down, layer)
    return _combine(x, gates, pos, ys, g, b, alpha), xs


def _rope_lane_frequencies():
    inv_freq = ROPE_THETA ** (-jnp.arange(0, ROPE_DIM, 2, dtype=F32) / ROPE_DIM)
    c = jnp.arange(LANES) % HEAD_DIM
    return jnp.where(c < ROPE_DIM, inv_freq[c % (ROPE_DIM // 2)], 0.0).astype(F32)[None, :]


def kernel(x, mem, positions, w_in_a, conv_a, w_q_b, w_kv_shared, w_mem_kv, w_o, ln1_g, ln1_b,
           ln2_g, ln2_b, w_gu_dense, w_down_dense, w_router, w_gu_moe, w_down_moe):
    bsz, seq, d = x.shape
    depth = w_o.shape[0]
    n_a = w_in_a.shape[0]
    alpha = float((2 * depth) ** 0.25)
    assert seq % MOBA_BLOCK == 0 and seq // MOBA_BLOCK <= LANES

    pos = positions.astype(F32)[:, :, None]
    freq = _rope_lane_frequencies()
    mem_len = mem.shape[1]
    w_mem_all = jnp.transpose(w_mem_kv, (1, 0, 2)).reshape(d, depth * 2 * MEM_WIDTH).astype(BF16)
    mem_kv = _matmul(mem.reshape(bsz * mem_len, d), w_mem_all, BF16).reshape(bsz, mem_len, -1)

    w_gu_dense_bf16, w_down_dense_bf16 = w_gu_dense.astype(BF16), w_down_dense.astype(BF16)
    w_gu_moe_bf16, w_down_moe_bf16 = w_gu_moe.astype(BF16), w_down_moe.astype(BF16)
    shared = None
    grouped = None
    for layer in range(depth):
        g1, b1 = ln1_g[layer][None, :], ln1_b[layer][None, :]
        g2, b2 = ln2_g[layer][None, :], ln2_b[layer][None, :]
        wo = w_o[layer].astype(BF16)
        if layer < n_a:
            x = _mixer_a(x, w_in_a[layer].astype(BF16), conv_a[layer], mem_kv, layer, wo, g1, b1, alpha)
        else:
            wq = w_q_b[layer - n_a]
            q, mem_out = _query_b(x, pos, freq, wq.astype(BF16), mem_kv, layer)
            main = _moba(q, *shared)
            x = _mixer_out(x, main, mem_out, wo, g1, b1, alpha)
        xf = x.reshape(bsz * seq, d)
        if layer % 2 == 0:
            xf = _ffn(xf, w_gu_dense_bf16, w_down_dense_bf16, layer // 2, g2, b2, alpha)
        else:
            xf, grouped = _moe(xf, w_router[layer // 2], w_gu_moe_bf16, w_down_moe_bf16, layer // 2,
                               g2, b2, alpha, grouped)
        x = xf.reshape(bsz, seq, d)
        if layer == n_a - 1:
            wk = w_kv_shared[:, :MIX_WIDTH]
            k, v, km = _shared_kv(x, pos, freq, wk.astype(BF16), w_kv_shared[:, MIX_WIDTH:].astype(BF16))
            n_blocks = seq // MOBA_BLOCK
            km = jnp.pad(km.reshape(bsz, n_blocks, MIX_WIDTH), ((0, 0), (0, LANES - n_blocks), (0, 0)))
            shared = (k, v, km)
    return x
```

```python
import functools

import jax
import jax.numpy as jnp
from jax import lax
from jax.experimental import pallas as pl
from jax.experimental.pallas import tpu as pltpu

F32 = jnp.float32
BF16 = jnp.bfloat16

HEAD_DIM = 64
MIX_HEADS = 12
MIX_WIDTH = MIX_HEADS * HEAD_DIM
MEM_HEADS = 4
MEM_WIDTH = MEM_HEADS * HEAD_DIM
CONV_WIDTH = 3
ROPE_DIM = HEAD_DIM // 4
ROPE_THETA = 500000.0
MOBA_BLOCK = 256
MOBA_TOPK = 3
TOP_K = 2
LN_EPS = 1e-5
ATTN_SCALE = HEAD_DIM ** -0.5
LOG2_E = 1.4426950408889634

LANES = 128
HEADS_PER_LANE_GROUP = LANES // HEAD_DIM
MASK_VALUE = -1e30
VMEM_LIMIT_BYTES = 56 * 1024 * 1024

SEQ_TILE = 512
LIGHT_SEQ_TILE = 1024
FFN_TILE = 1024
FF_CHUNK = 1408
MOBA_CHUNK_BLOCKS = 4
ROUTE_TILE = 1024
EXPERT_TILE = 512
COMBINE_TILE = 1024
DMA_LOOP_UNROLL = 8


def _params(semantics):
    return pltpu.CompilerParams(dimension_semantics=semantics, vmem_limit_bytes=VMEM_LIMIT_BYTES)


def _layer_norm(r, g, b):
    mu = jnp.mean(r, axis=-1, keepdims=True)
    c = r - mu
    var = jnp.mean(c * c, axis=-1, keepdims=True)
    return c * lax.rsqrt(var + LN_EPS) * g + b


def _dot(a, b):
    return jnp.dot(a, b, preferred_element_type=F32)


def _dot_nt(a, b):
    return lax.dot_general(a, b, (((1,), (1,)), ((), ())), preferred_element_type=F32)


def _head_lane_mask(width, head):
    lane = lax.broadcasted_iota(jnp.int32, (1, width), 1)
    return (lane >= head * HEAD_DIM) & (lane < (head + 1) * HEAD_DIM)


def _mem_attention(q_mem, mem_k, mem_v):
    out = jnp.zeros(q_mem.shape, F32)
    for head in range(MEM_HEADS):
        hm = _head_lane_mask(MEM_WIDTH, head)
        qh = jnp.where(hm, q_mem * ATTN_SCALE, 0.0).astype(BF16)
        s = _dot_nt(qh, mem_k)
        m = jnp.max(s, axis=-1, keepdims=True)
        p = jnp.exp(s - m)
        l = jnp.sum(p, axis=-1, keepdims=True)
        o = _dot(p.astype(BF16), mem_v) / l
        out = jnp.where(hm, o, out)
    return out


def _rope_tables(pos, freq, reps):
    angle = pos * freq
    cos = jnp.cos(angle)
    sin = jnp.sin(angle)
    return (jnp.concatenate([cos] * reps, axis=1), jnp.concatenate([sin] * reps, axis=1))


def _matmul_kernel(x_ref, w_ref, o_ref):
    o_ref[...] = _dot(x_ref[...].astype(BF16), w_ref[...]).astype(o_ref.dtype)


def _matmul(x, w, out_dtype, tile_m=256):
    m, k = x.shape
    n = w.shape[1]
    return pl.pallas_call(
        _matmul_kernel,
        grid=(m // tile_m,),
        in_specs=[pl.BlockSpec((tile_m, k), lambda i: (i, 0)),
                  pl.BlockSpec((k, n), lambda i: (0, 0))],
        out_specs=pl.BlockSpec((tile_m, n), lambda i: (i, 0)),
        out_shape=jax.ShapeDtypeStruct((m, n), out_dtype),
        compiler_params=_params(("arbitrary",)),
        name="mem_kv_proj",
    )(x, w)


def _mixer_a_kernel(x_ref, win_ref, conv_ref, mk_ref, mv_ref, wo_ref, g_ref, b_ref, o_ref,
                    ztail_ref, *, alpha):
    tile = x_ref.shape[1]

    @pl.when(pl.program_id(1) == 0)
    def _():
        ztail_ref[...] = jnp.zeros_like(ztail_ref)

    x = x_ref[0]
    proj = _dot(x.astype(BF16), win_ref[...])
    u_b = proj[:, :MIX_WIDTH]
    u_c = proj[:, MIX_WIDTH:2 * MIX_WIDTH]
    u_h = proj[:, 2 * MIX_WIDTH:3 * MIX_WIDTH]
    q_mem = proj[:, 3 * MIX_WIDTH:]

    z = u_c * u_h
    tail = ztail_ref[...]
    ztail_ref[...] = z[tile - 8:, :]
    row = lax.broadcasted_iota(jnp.int32, z.shape, 0)
    z1 = jnp.where(row == 0, tail[7:8, :], pltpu.roll(z, 1, 0))
    z2 = jnp.where(row == 0, tail[6:7, :], jnp.where(row == 1, tail[7:8, :], pltpu.roll(z, 2, 0)))
    w = conv_ref[...]
    main = u_b * (w[0:1, :] * z2 + w[1:2, :] * z1 + w[2:3, :] * z)

    mem_out = _mem_attention(q_mem, mk_ref[0], mv_ref[0])
    mix = (_dot(main.astype(BF16), wo_ref[:MIX_WIDTH, :])
           + _dot(mem_out.astype(BF16), wo_ref[MIX_WIDTH:, :]))
    o_ref[0] = _layer_norm(alpha * x + mix, g_ref[...], b_ref[...])


def _mixer_a(x, w_in, conv_w, mem_kv, layer, w_o, g, b, alpha):
    bsz, seq, d = x.shape
    tile = min(SEQ_TILE, seq)
    mem_len = mem_kv.shape[1]
    return pl.pallas_call(
        functools.partial(_mixer_a_kernel, alpha=alpha),
        grid=(bsz, seq // tile),
        in_specs=[
            pl.BlockSpec((1, tile, d), lambda bi, ti: (bi, ti, 0)),
            pl.BlockSpec(w_in.shape, lambda bi, ti: (0, 0)),
            pl.BlockSpec(conv_w.shape, lambda bi, ti: (0, 0)),
            pl.BlockSpec((1, mem_len, MEM_WIDTH), lambda bi, ti: (bi, 0, 2 * layer)),
            pl.BlockSpec((1, mem_len, MEM_WIDTH), lambda bi, ti: (bi, 0, 2 * layer + 1)),
            pl.BlockSpec(w_o.shape, lambda bi, ti: (0, 0)),
            pl.BlockSpec((1, d), lambda bi, ti: (0, 0)),
            pl.BlockSpec((1, d), lambda bi, ti: (0, 0)),
        ],
        out_specs=pl.BlockSpec((1, tile, d), lambda bi, ti: (bi, ti, 0)),
        out_shape=jax.ShapeDtypeStruct(x.shape, F32),
        scratch_shapes=[pltpu.VMEM((8, MIX_WIDTH), F32)],
        compiler_params=_params(("arbitrary", "arbitrary")),
        name="mixer_a",
    )(x, w_in, conv_w, mem_kv, mem_kv, w_o, g, b)


def _partial_rope(t, pos, freq):
    half = ROPE_DIM // 2
    cos, sin = _rope_tables(pos, freq, MIX_WIDTH // LANES)
    dim = lax.broadcasted_iota(jnp.int32, (1, MIX_WIDTH), 1) % HEAD_DIM
    upper = pltpu.roll(t, MIX_WIDTH - half, 1)
    lower = pltpu.roll(t, half, 1)
    return t * cos + jnp.where(dim < half, -upper, lower) * sin


def _shared_kv_kernel(x_ref, pos_ref, freq_ref, wk_ref, wv_ref, k_ref, v_ref, km_ref):
    xb = x_ref[0].astype(BF16)
    k = _partial_rope(_dot(xb, wk_ref[...]), pos_ref[0], freq_ref[...])
    k_ref[0] = k.astype(BF16)
    v_ref[0] = _dot(xb, wv_ref[...]).astype(BF16)
    km_ref[0, 0] = jnp.mean(k, axis=0, keepdims=True)


def _shared_kv(x, pos, freq, w_k, w_v):
    bsz, seq, d = x.shape
    n_blocks = seq // MOBA_BLOCK
    full = lambda bi, ti: (0, 0)
    return pl.pallas_call(
        _shared_kv_kernel,
        grid=(bsz, n_blocks),
        in_specs=[
            pl.BlockSpec((1, MOBA_BLOCK, d), lambda bi, ti: (bi, ti, 0)),
            pl.BlockSpec((1, MOBA_BLOCK, 1), lambda bi, ti: (bi, ti, 0)),
            pl.BlockSpec(freq.shape, full),
            pl.BlockSpec(w_k.shape, full),
            pl.BlockSpec(w_v.shape, full),
        ],
        out_specs=[
            pl.BlockSpec((1, MOBA_BLOCK, MIX_WIDTH), lambda bi, ti: (bi, ti, 0)),
            pl.BlockSpec((1, MOBA_BLOCK, MIX_WIDTH), lambda bi, ti: (bi, ti, 0)),
            pl.BlockSpec((1, 1, 1, MIX_WIDTH), lambda bi, ti: (bi, ti, 0, 0)),
        ],
        out_shape=[
            jax.ShapeDtypeStruct((bsz, seq, MIX_WIDTH), BF16),
            jax.ShapeDtypeStruct((bsz, seq, MIX_WIDTH), BF16),
            jax.ShapeDtypeStruct((bsz, n_blocks, 1, MIX_WIDTH), F32),
        ],
        compiler_params=_params(("arbitrary", "arbitrary")),
        name="shared_kv",
    )(x, pos, freq, w_k, w_v)


def _query_b_kernel(x_ref, pos_ref, freq_ref, wq_ref, mk_ref, mv_ref, q_ref, mo_ref):
    xb = x_ref[0].astype(BF16)
    proj = _dot(xb, wq_ref[...])
    q = _partial_rope(proj[:, :MIX_WIDTH], pos_ref[0], freq_ref[...])
    q_ref[0] = (q * (ATTN_SCALE * LOG2_E)).astype(BF16)
    mo_ref[0] = _mem_attention(proj[:, MIX_WIDTH:], mk_ref[0], mv_ref[0]).astype(BF16)


def _query_b(x, pos, freq, w_q, mem_kv, layer):
    bsz, seq, d = x.shape
    tile = min(LIGHT_SEQ_TILE, seq)
    mem_len = mem_kv.shape[1]
    full = lambda bi, ti: (0, 0)
    return pl.pallas_call(
        _query_b_kernel,
        grid=(bsz, seq // tile),
        in_specs=[
            pl.BlockSpec((1, tile, d), lambda bi, ti: (bi, ti, 0)),
            pl.BlockSpec((1, tile, 1), lambda bi, ti: (bi, ti, 0)),
            pl.BlockSpec(freq.shape, full),
            pl.BlockSpec(w_q.shape, full),
            pl.BlockSpec((1, mem_len, MEM_WIDTH), lambda bi, ti: (bi, 0, 2 * layer)),
            pl.BlockSpec((1, mem_len, MEM_WIDTH), lambda bi, ti: (bi, 0, 2 * layer + 1)),
        ],
        out_specs=[
            pl.BlockSpec((1, tile, MIX_WIDTH), lambda bi, ti: (bi, ti, 0)),
            pl.BlockSpec((1, tile, MEM_WIDTH), lambda bi, ti: (bi, ti, 0)),
        ],
        out_shape=[
            jax.ShapeDtypeStruct((bsz, seq, MIX_WIDTH), BF16),
            jax.ShapeDtypeStruct((bsz, seq, MEM_WIDTH), BF16),
        ],
        compiler_params=_params(("arbitrary", "arbitrary")),
        name="query_b",
    )(x, pos, freq, w_q, mem_kv, mem_kv)


def _moba_kernel(q_ref, k_ref, v_ref, km_ref, o_ref, kaug_ref, vaug_ref, m_ref, acc_ref, s0_ref,
                 s1_ref, *, chunk_blocks):
    blk = MOBA_BLOCK
    span = chunk_blocks * blk
    pair = pl.program_id(2)
    chains = [(u, h) for u in range(chunk_blocks) for h in range(HEADS_PER_LANE_GROUP)]
    rows = len(chains) * blk

    @pl.when(pair == 0)
    def _():
        kaug_ref[:, :LANES] = k_ref[0]
        vaug_ref[:, :LANES] = v_ref[0]
        lane = lax.broadcasted_iota(jnp.int32, (blk, LANES), 1)
        ones_lane = (lane == 0).astype(BF16)

        def fill(j, carry):
            off = pl.multiple_of(j * blk, blk)
            kaug_ref[pl.ds(off, blk), LANES:] = (lane == j).astype(BF16)
            vaug_ref[pl.ds(off, blk), LANES:] = ones_lane
            return carry

        lax.fori_loop(0, k_ref.shape[1] // blk, fill, 0)

    def chunk_rows(c):
        return pl.ds(pl.multiple_of(c * span, span), span)

    def sweep(half, sa_ref, sb_ref):
        own_chunk = 2 * pair + half
        q = q_ref[0, half * span:(half + 1) * span, :]
        q_heads = jnp.concatenate(
            [jnp.where(_head_lane_mask(LANES, h), q[u * blk:(u + 1) * blk], 0).astype(BF16)
             for u, h in chains], axis=0)

        block_id = lax.broadcasted_iota(jnp.int32, (rows, LANES), 1)
        own = (own_chunk * chunk_blocks
               + lax.broadcasted_iota(jnp.int32, (rows, LANES), 0) // (HEADS_PER_LANE_GROUP * blk))
        is_past = block_id < own
        gate = jnp.where(is_past, _dot_nt(q_heads, km_ref[0].astype(BF16)), -jnp.inf)
        kth = gate
        for _ in range(MOBA_TOPK - 1):
            kth = jnp.where(kth == jnp.max(kth, axis=-1, keepdims=True), -jnp.inf, kth)
        threshold = jnp.max(kth, axis=-1, keepdims=True)
        allowed = (is_past & (gate >= threshold)) | (block_id == own)
        block_bias = jnp.where(allowed, 0.0, MASK_VALUE).astype(BF16)
        q_aug = jnp.concatenate([q_heads, block_bias], axis=1)

        state_rows = [slice(half * rows + i * blk, half * rows + (i + 1) * blk)
                      for i in range(len(chains))]
        chain_rows = [slice(i * blk, (i + 1) * blk) for i in range(len(chains))]
        for r in state_rows:
            m_ref[r, :] = jnp.full((blk, 1), MASK_VALUE, F32)
            acc_ref[r, :] = jnp.zeros((blk, acc_ref.shape[1]), F32)

        def score_own(s_ref):
            for r, (u, _) in zip(chain_rows, chains):
                width = (u + 1) * blk
                k_part = kaug_ref[pl.ds(pl.multiple_of(own_chunk * span, span), width), :]
                k_pos = lax.broadcasted_iota(jnp.int32, (blk, width), 1)
                q_pos = u * blk + lax.broadcasted_iota(jnp.int32, (blk, width), 0)
                s_ref[r, :width] = jnp.where(k_pos <= q_pos, _dot_nt(q_aug[r], k_part), MASK_VALUE)
                if width < span:
                    s_ref[r, width:] = jnp.full((blk, span - width), MASK_VALUE, F32)

        def score_past(c, s_ref):
            k_chunk = kaug_ref[chunk_rows(c), :]
            for r in chain_rows:
                s_ref[r, :] = _dot_nt(q_aug[r], k_chunk)

        def softmax_value(c, s_ref):
            v_chunk = vaug_ref[chunk_rows(c), :]
            for r, sr in zip(chain_rows, state_rows):
                s = s_ref[r, :]
                m_old = m_ref[sr, :]
                m_new = jnp.maximum(m_old, jnp.max(s, axis=-1, keepdims=True))
                p = jnp.exp2((s - m_new).astype(BF16))
                acc_ref[sr, :] = jnp.exp2(m_old - m_new) * acc_ref[sr, :] + _dot(p, v_chunk)
                m_ref[sr, :] = m_new

        def step_chunk(j):
            return jnp.where(j == 0, own_chunk, j - 1)

        n_past = own_chunk
        score_own(sa_ref)

        def two_steps(i, carry):
            j = 2 * i + 1
            score_past(j - 1, sb_ref)
            softmax_value(step_chunk(j - 1), sa_ref)
            score_past(j, sa_ref)
            softmax_value(j - 1, sb_ref)
            return carry

        lax.fori_loop(0, pair, two_steps, 0)
        if half == 1:
            score_past(n_past - 1, sb_ref)
            softmax_value(step_chunk(n_past - 1), sa_ref)
            softmax_value(n_past - 1, sb_ref)
        else:
            softmax_value(step_chunk(n_past), sa_ref)

        for u in range(chunk_blocks):
            out = None
            for h in range(HEADS_PER_LANE_GROUP):
                acc = acc_ref[state_rows[u * HEADS_PER_LANE_GROUP + h], :]
                o = acc[:, :LANES] / acc[:, LANES:LANES + 1]
                out = o if out is None else jnp.where(_head_lane_mask(LANES, h), o, out)
            o_ref[0, half * span + u * blk:half * span + (u + 1) * blk, :] = out.astype(o_ref.dtype)

    sweep(0, s0_ref, s1_ref)
    sweep(1, s1_ref, s0_ref)


def _moba(q, k, v, k_means):
    bsz, seq, _ = q.shape
    n_blocks = seq // MOBA_BLOCK
    groups = MIX_WIDTH // LANES
    chunk_blocks = MOBA_CHUNK_BLOCKS
    while n_blocks % (2 * chunk_blocks):
        chunk_blocks //= 2
    assert chunk_blocks >= 1, "the sequence must hold an even number of MoBA blocks"
    span = chunk_blocks * MOBA_BLOCK
    rows = HEADS_PER_LANE_GROUP * span
    return pl.pallas_call(
        functools.partial(_moba_kernel, chunk_blocks=chunk_blocks),
        grid=(bsz, groups, seq // (2 * span)),
        in_specs=[
            pl.BlockSpec((1, 2 * span, LANES), lambda bi, gi, qi: (bi, qi, gi)),
            pl.BlockSpec((1, seq, LANES), lambda bi, gi, qi: (bi, 0, gi)),
            pl.BlockSpec((1, seq, LANES), lambda bi, gi, qi: (bi, 0, gi)),
            pl.BlockSpec((1, LANES, LANES), lambda bi, gi, qi: (bi, 0, gi)),
        ],
        out_specs=pl.BlockSpec((1, 2 * span, LANES), lambda bi, gi, qi: (bi, qi, gi)),
        out_shape=jax.ShapeDtypeStruct(q.shape, BF16),
        scratch_shapes=[pltpu.VMEM((seq, 2 * LANES), BF16), pltpu.VMEM((seq, 2 * LANES), BF16),
                        pltpu.VMEM((2 * rows, 1), F32), pltpu.VMEM((2 * rows, 2 * LANES), F32),
                        pltpu.VMEM((rows, span), F32), pltpu.VMEM((rows, span), F32)],
        compiler_params=_params(("arbitrary", "arbitrary", "arbitrary")),
        name="moba",
    )(q, k, v, k_means)


def _mixer_out_kernel(x_ref, main_ref, mo_ref, wo_ref, g_ref, b_ref, o_ref, *, alpha):
    mix = _dot(main_ref[0], wo_ref[:MIX_WIDTH, :]) + _dot(mo_ref[0], wo_ref[MIX_WIDTH:, :])
    o_ref[0] = _layer_norm(alpha * x_ref[0] + mix, g_ref[...], b_ref[...])


def _mixer_out(x, main, mem_out, w_o, g, b, alpha):
    bsz, seq, d = x.shape
    tile = min(LIGHT_SEQ_TILE, seq)
    full = lambda bi, ti: (0, 0)
    return pl.pallas_call(
        functools.partial(_mixer_out_kernel, alpha=alpha),
        grid=(bsz, seq // tile),
        in_specs=[
            pl.BlockSpec((1, tile, d), lambda bi, ti: (bi, ti, 0)),
            pl.BlockSpec((1, tile, MIX_WIDTH), lambda bi, ti: (bi, ti, 0)),
            pl.BlockSpec((1, tile, MEM_WIDTH), lambda bi, ti: (bi, ti, 0)),
            pl.BlockSpec(w_o.shape, full),
            pl.BlockSpec((1, d), full),
            pl.BlockSpec((1, d), full),
        ],
        out_specs=pl.BlockSpec((1, tile, d), lambda bi, ti: (bi, ti, 0)),
        out_shape=jax.ShapeDtypeStruct(x.shape, F32),
        compiler_params=_params(("arbitrary", "arbitrary")),
        name="mixer_out",
    )(x, main, mem_out, w_o, g, b)


def _swiglu_chunk(xb, wg, wu, wd):
    gp = _dot(xb, wg)
    up = _dot(xb, wu)
    return _dot((gp * jax.nn.sigmoid(gp) * up).astype(BF16), wd)


def _ffn_chunks(d_ff):
    chunk = FF_CHUNK if d_ff % FF_CHUNK == 0 else d_ff
    return chunk, d_ff // chunk


def _accumulate_chunks(c, n_chunks, part, acc_ref, finish):
    if n_chunks == 1:
        finish(part())
        return

    @pl.when(c == 0)
    def _():
        acc_ref[...] = part()

    if n_chunks > 2:
        @pl.when((c > 0) & (c < n_chunks - 1))
        def _():
            acc_ref[...] += part()

    @pl.when(c == n_chunks - 1)
    def _():
        finish(acc_ref[...] + part())


def _ffn_kernel(x_ref, wg_ref, wu_ref, wd_ref, g_ref, b_ref, o_ref, acc_ref, xb_ref, *, alpha,
                n_chunks):
    c = pl.program_id(1)

    @pl.when(c == 0)
    def _():
        xb_ref[...] = x_ref[...].astype(BF16)

    def finish(ffn):
        o_ref[...] = _layer_norm(alpha * x_ref[...] + ffn, g_ref[...], b_ref[...])

    def part():
        return _swiglu_chunk(xb_ref[...], wg_ref[...], wu_ref[...], wd_ref[...])

    _accumulate_chunks(c, n_chunks, part, acc_ref, finish)


def _ffn(x, w_gu, w_down, layer, g, b, alpha):
    n, d = x.shape
    d_ff = w_down.shape[1]
    tile = min(FFN_TILE, n)
    chunk, n_chunks = _ffn_chunks(d_ff)
    return pl.pallas_call(
        functools.partial(_ffn_kernel, alpha=alpha, n_chunks=n_chunks),
        grid=(n // tile, n_chunks),
        in_specs=[
            pl.BlockSpec((tile, d), lambda i, c: (i, 0)),
            pl.BlockSpec((None, d, chunk), lambda i, c: (layer, 0, c)),
            pl.BlockSpec((None, d, chunk), lambda i, c: (layer, 0, c + n_chunks)),
            pl.BlockSpec((None, chunk, d), lambda i, c: (layer, c, 0)),
            pl.BlockSpec((1, d), lambda i, c: (0, 0)),
            pl.BlockSpec((1, d), lambda i, c: (0, 0)),
        ],
        out_specs=pl.BlockSpec((tile, d), lambda i, c: (i, 0)),
        out_shape=jax.ShapeDtypeStruct(x.shape, F32),
        scratch_shapes=[pltpu.VMEM((tile, d), F32), pltpu.VMEM((tile, d), BF16)],
        compiler_params=_params(("arbitrary", "arbitrary")),
        name="ffn_dense",
    )(x, w_gu, w_gu, w_down, g, b)


META_EXPERT, META_RANK, META_GATE, META_ROWS = 0, 2, 4, 8
EXPERT_ROWS = 16


def _split_bf16(a):
    hi = a.astype(BF16)
    return hi, (a - hi.astype(F32)).astype(BF16)


def _route_kernel(x_ref, wr_ref, meta_ref, counts_ref, *, n_experts):
    tile = x_ref.shape[0]

    @pl.when(pl.program_id(0) == 0)
    def _():
        counts_ref[...] = jnp.zeros_like(counts_ref)

    x_hi, x_lo = _split_bf16(x_ref[...])
    w_hi, w_lo = _split_bf16(wr_ref[...])
    hi_part = _dot_nt(jnp.concatenate([w_hi, w_lo], axis=0), x_hi)
    logits = hi_part[:EXPERT_ROWS] + (hi_part[EXPERT_ROWS:] + _dot_nt(w_hi, x_lo))
    expert = lax.broadcasted_iota(jnp.int32, (EXPERT_ROWS, tile), 0)
    logits = jnp.where(expert < n_experts, logits, -jnp.inf)
    top1 = jnp.max(logits, axis=0, keepdims=True)
    idx1 = jnp.min(jnp.where(logits == top1, expert, EXPERT_ROWS), axis=0, keepdims=True)
    rest = jnp.where(expert == idx1, -jnp.inf, logits)
    top2 = jnp.max(rest, axis=0, keepdims=True)
    idx2 = jnp.min(jnp.where(rest == top2, expert, EXPERT_ROWS), axis=0, keepdims=True)
    e2 = jnp.exp(top2 - top1)

    chosen = (expert == idx1) | (expert == idx2)
    earlier = (lax.broadcasted_iota(jnp.int32, (tile, tile), 0)
               < lax.broadcasted_iota(jnp.int32, (tile, tile), 1))
    rank = counts_ref[...] + _dot(chosen.astype(BF16), earlier.astype(BF16))
    rank1 = jnp.sum(jnp.where(expert == idx1, rank, 0.0), axis=0, keepdims=True)
    rank2 = jnp.sum(jnp.where(expert == idx2, rank, 0.0), axis=0, keepdims=True)
    counts_ref[...] += jnp.sum(chosen.astype(F32), axis=1, keepdims=True)

    fields = [idx1.astype(F32), idx2.astype(F32), rank1, rank2, 1.0 / (1.0 + e2), e2 / (1.0 + e2)]
    row = lax.broadcasted_iota(jnp.int32, (META_ROWS, tile), 0)
    meta = jnp.zeros((META_ROWS, tile), F32)
    for i, field in enumerate(fields):
        meta = jnp.where(row == i, field, meta)
    meta_ref[...] = meta


def _route(x, w_router):
    n, d = x.shape
    tile = min(ROUTE_TILE, n)
    n_experts = w_router.shape[1]
    return pl.pallas_call(
        functools.partial(_route_kernel, n_experts=n_experts),
        grid=(n // tile,),
        in_specs=[pl.BlockSpec((tile, d), lambda i: (i, 0)),
                  pl.BlockSpec((EXPERT_ROWS, d), lambda i: (0, 0))],
        out_specs=[pl.BlockSpec((META_ROWS, tile), lambda i: (0, i)),
                   pl.BlockSpec((EXPERT_ROWS, 1), lambda i: (0, 0))],
        out_shape=[jax.ShapeDtypeStruct((META_ROWS, n), F32),
                   jax.ShapeDtypeStruct((EXPERT_ROWS, 1), F32)],
        compiler_params=_params(("arbitrary",)),
        name="moe_route",
    )(x, jnp.pad(w_router.T, ((0, EXPERT_ROWS - n_experts), (0, 0))))


def _row_copy(src_ref, src_row, dst_ref, dst_row, sem):
    return pltpu.make_async_copy(src_ref.at[pl.ds(src_row, 1), :], dst_ref.at[pl.ds(dst_row, 1), :], sem)


def _start_then_wait(copies, n):
    def start(r, carry):
        for cp in copies(r):
            cp.start()
        return carry

    def wait(r, carry):
        for cp in copies(r):
            cp.wait()
        return carry

    lax.fori_loop(0, n, start, 0, unroll=DMA_LOOP_UNROLL)
    lax.fori_loop(0, n, wait, 0, unroll=DMA_LOOP_UNROLL)


def _dispatch_kernel(pos_ref, x_ref, init_ref, xs_ref, sem):
    del init_ref

    def copies(r):
        return [_row_copy(x_ref, r, xs_ref, pos_ref[0, 0, TOP_K * r + k], sem) for k in range(TOP_K)]

    _start_then_wait(copies, x_ref.shape[0])


def _dispatch(x, pos, grouped_init):
    n, d = x.shape
    tile = min(ROUTE_TILE, n)
    any_spec = pl.BlockSpec(memory_space=pl.ANY)
    return pl.pallas_call(
        _dispatch_kernel,
        grid=(n // tile,),
        in_specs=[pl.BlockSpec((1, 1, TOP_K * tile), lambda i: (i, 0, 0), memory_space=pltpu.SMEM),
                  pl.BlockSpec((tile, d), lambda i: (i, 0)), any_spec],
        out_specs=any_spec,
        out_shape=jax.ShapeDtypeStruct(grouped_init.shape, F32),
        scratch_shapes=[pltpu.SemaphoreType.DMA],
        input_output_aliases={2: 0},
        compiler_params=_params(("arbitrary",)),
        name="moe_dispatch",
    )(pos.reshape(n // tile, 1, TOP_K * tile), x, grouped_init)


def _expert_ffn_kernel(tile_expert_ref, n_tiles_ref, xs_ref, wg_ref, wu_ref, wd_ref, o_ref,
                       acc_ref, xb_ref, *, n_chunks):
    del tile_expert_ref
    c = pl.program_id(1)
    in_use = pl.program_id(0) < n_tiles_ref[0]

    @pl.when(jnp.logical_not(in_use) & (c == n_chunks - 1))
    def _():
        o_ref[...] = jnp.zeros_like(o_ref)

    @pl.when(in_use)
    def _():
        @pl.when(c == 0)
        def _():
            xb_ref[...] = xs_ref[...].astype(BF16)

        def finish(ffn):
            o_ref[...] = ffn

        def part():
            return _swiglu_chunk(xb_ref[...], wg_ref[0], wu_ref[0], wd_ref[0])

        _accumulate_chunks(c, n_chunks, part, acc_ref, finish)


def _expert_ffn(xs, tile_expert, n_tiles, w_gu, w_down, layer):
    n_rows, d = xs.shape
    d_ff = w_down.shape[2]
    chunk, n_chunks = _ffn_chunks(d_ff)
    row_tile = lambda t, c, te, nt: (jnp.minimum(t, nt[0] - 1), 0)
    grid_spec = pltpu.PrefetchScalarGridSpec(
        num_scalar_prefetch=2,
        grid=(n_rows // EXPERT_TILE, n_chunks),
        in_specs=[
            pl.BlockSpec((EXPERT_TILE, d), row_tile),
            pl.BlockSpec((None, 1, d, chunk), lambda t, c, te, nt: (layer, te[t], 0, c)),
            pl.BlockSpec((None, 1, d, chunk), lambda t, c, te, nt: (layer, te[t], 0, c + n_chunks)),
            pl.BlockSpec((None, 1, chunk, d), lambda t, c, te, nt: (layer, te[t], c, 0)),
        ],
        out_specs=pl.BlockSpec((EXPERT_TILE, d), lambda t, c, te, nt: (t, 0)),
        scratch_shapes=[pltpu.VMEM((EXPERT_TILE, d), F32), pltpu.VMEM((EXPERT_TILE, d), BF16)],
    )
    return pl.pallas_call(
        functools.partial(_expert_ffn_kernel, n_chunks=n_chunks),
        grid_spec=grid_spec,
        out_shape=jax.ShapeDtypeStruct((n_rows, d), F32),
        compiler_params=_params(("arbitrary", "arbitrary")),
        name="moe_experts",
    )(tile_expert, n_tiles, xs, w_gu, w_gu, w_down)


def _combine_kernel(pos_ref, gates_ref, x_ref, g_ref, b_ref, ys_ref, o_ref, rows_ref, sem, *, alpha):
    tile = x_ref.shape[0]

    def copies(r):
        return [_row_copy(ys_ref, pos_ref[0, 0, TOP_K * r + k], rows_ref.at[k], r, sem)
                for k in range(TOP_K)]

    _start_then_wait(copies, tile)

    gates = gates_ref[...]
    ffn = jnp.zeros(x_ref.shape, F32)
    for k in range(TOP_K):
        ffn = ffn + gates[:, k:k + 1] * rows_ref[k]
    o_ref[...] = _layer_norm(alpha * x_ref[...] + ffn, g_ref[...], b_ref[...])


def _combine(x, gates, pos, ys, g, b, alpha):
    n, d = x.shape
    tile = min(COMBINE_TILE, n)
    return pl.pallas_call(
        functools.partial(_combine_kernel, alpha=alpha),
        grid=(n // tile,),
        in_specs=[pl.BlockSpec((1, 1, TOP_K * tile), lambda i: (i, 0, 0), memory_space=pltpu.SMEM),
                  pl.BlockSpec((tile, TOP_K), lambda i: (i, 0)),
                  pl.BlockSpec((tile, d), lambda i: (i, 0)),
                  pl.BlockSpec((1, d), lambda i: (0, 0)),
                  pl.BlockSpec((1, d), lambda i: (0, 0)),
                  pl.BlockSpec(memory_space=pl.ANY)],
        out_specs=pl.BlockSpec((tile, d), lambda i: (i, 0)),
        out_shape=jax.ShapeDtypeStruct(x.shape, F32),
        scratch_shapes=[pltpu.VMEM((TOP_K, tile, d), F32), pltpu.SemaphoreType.DMA],
        compiler_params=_params(("arbitrary",)),
        name="moe_combine",
    )(pos.reshape(n // tile, 1, TOP_K * tile), gates, x, g, b, ys)


def _moe(x, w_router, w_gu, w_down, layer, g, b, alpha, grouped_init):
    n, d = x.shape
    n_experts = w_router.shape[1]
    meta, counts = _route(x, w_router)

    counts = counts[:n_experts, 0].astype(jnp.int32)
    group_tiles = (counts + EXPERT_TILE - 1) // EXPERT_TILE
    tiles_end = jnp.cumsum(group_tiles)
    group_row0 = (tiles_end - group_tiles) * EXPERT_TILE
    n_grid_tiles = (TOP_K * n) // EXPERT_TILE + n_experts
    tile_ids = jnp.arange(n_grid_tiles, dtype=jnp.int32)
    tile_expert = jnp.minimum(jnp.sum((tile_ids[:, None] >= tiles_end[None, :]).astype(jnp.int32), axis=1),
                              n_experts - 1)
    records = meta.T
    expert = records[:, META_EXPERT:META_EXPERT + TOP_K].astype(jnp.int32)
    rank = records[:, META_RANK:META_RANK + TOP_K].astype(jnp.int32)
    gates = records[:, META_GATE:META_GATE + TOP_K]
    pos = (group_row0[expert] + rank).reshape(-1)

    if grouped_init is None:
        grouped_init = jnp.zeros((n_grid_tiles * EXPERT_TILE, d), F32)
    xs = _dispatch(x, pos, grouped_init)
    ys = _expert_ffn(xs, tile_expert, tiles_end[-1:].astype(jnp.int32), w_gu, w_down, layer)
    return _combine(x, gates, pos, ys, g, b, alpha), xs


def _rope_lane_frequencies():
    inv_freq = ROPE_THETA ** (-jnp.arange(0, ROPE_DIM, 2, dtype=F32) / ROPE_DIM)
    c = jnp.arange(LANES) % HEAD_DIM
    return jnp.where(c < ROPE_DIM, inv_freq[c % (ROPE_DIM // 2)], 0.0).astype(F32)[None, :]


def kernel(x, mem, positions, w_in_a, conv_a, w_q_b, w_kv_shared, w_mem_kv, w_o, ln1_g, ln1_b,
           ln2_g, ln2_b, w_gu_dense, w_down_dense, w_router, w_gu_moe, w_down_moe):
    bsz, seq, d = x.shape
    depth = w_o.shape[0]
    n_a = w_in_a.shape[0]
    alpha = float((2 * depth) ** 0.25)
    assert seq % MOBA_BLOCK == 0 and seq // MOBA_BLOCK <= LANES

    pos = positions.astype(F32)[:, :, None]
    freq = _rope_lane_frequencies()
    mem_len = mem.shape[1]
    w_mem_all = jnp.transpose(w_mem_kv, (1, 0, 2)).reshape(d, depth * 2 * MEM_WIDTH).astype(BF16)
    mem_kv = _matmul(mem.reshape(bsz * mem_len, d), w_mem_all, BF16).reshape(bsz, mem_len, -1)

    w_gu_dense_bf16, w_down_dense_bf16 = w_gu_dense.astype(BF16), w_down_dense.astype(BF16)
    w_gu_moe_bf16, w_down_moe_bf16 = w_gu_moe.astype(BF16), w_down_moe.astype(BF16)
    shared = None
    grouped = None
    for layer in range(depth):
        g1, b1 = ln1_g[layer][None, :], ln1_b[layer][None, :]
        g2, b2 = ln2_g[layer][None, :], ln2_b[layer][None, :]
        wo = w_o[layer].astype(BF16)
        if layer < n_a:
            x = _mixer_a(x, w_in_a[layer].astype(BF16), conv_a[layer], mem_kv, layer, wo, g1, b1, alpha)
        else:
            wq = w_q_b[layer - n_a]
            q, mem_out = _query_b(x, pos, freq, wq.astype(BF16), mem_kv, layer)
            main = _moba(q, *shared)
            x = _mixer_out(x, main, mem_out, wo, g1, b1, alpha)
        xf = x.reshape(bsz * seq, d)
        if layer % 2 == 0:
            xf = _ffn(xf, w_gu_dense_bf16, w_down_dense_bf16, layer // 2, g2, b2, alpha)
        else:
            xf, grouped = _moe(xf, w_router[layer // 2], w_gu_moe_bf16, w_down_moe_bf16, layer // 2,
                               g2, b2, alpha, grouped)
        x = xf.reshape(bsz, seq, d)
        if layer == n_a - 1:
            wk = w_kv_shared[:, :MIX_WIDTH]
            k, v, km = _shared_kv(x, pos, freq, wk.astype(BF16), w_kv_shared[:, MIX_WIDTH:].astype(BF16))
            n_blocks = seq // MOBA_BLOCK
            km = jnp.pad(km.reshape(bsz, n_blocks, MIX_WIDTH), ((0, 0), (0, LANES - n_blocks), (0, 0)))
            shared = (k, v, km)
    return x
```

```python
import functools

import jax
import jax.numpy as jnp
from jax import lax
from jax.experimental import pallas as pl
from jax.experimental.pallas import tpu as pltpu

F32 = jnp.float32
BF16 = jnp.bfloat16

HEAD_DIM = 64
MIX_HEADS = 12
MIX_WIDTH = MIX_HEADS * HEAD_DIM
MEM_HEADS = 4
MEM_WIDTH = MEM_HEADS * HEAD_DIM
CONV_WIDTH = 3
ROPE_DIM = HEAD_DIM // 4
ROPE_THETA = 500000.0
MOBA_BLOCK = 256
MOBA_TOPK = 3
TOP_K = 2
LN_EPS = 1e-5
ATTN_SCALE = HEAD_DIM ** -0.5
LOG2_E = 1.4426950408889634

LANES = 128
HEADS_PER_LANE_GROUP = LANES // HEAD_DIM
MASK_VALUE = -1e30
VMEM_LIMIT_BYTES = 56 * 1024 * 1024

SEQ_TILE = 512
LIGHT_SEQ_TILE = 1024
FFN_TILE = 1024
FF_CHUNK = 1408
MOBA_CHUNK_BLOCKS = 4
ROUTE_TILE = 1024
DISPATCH_TILE = 2048
EXPERT_TILE = 512
COMBINE_TILE = 1024
DMA_LOOP_UNROLL = 8


def _params(semantics):
    return pltpu.CompilerParams(dimension_semantics=semantics, vmem_limit_bytes=VMEM_LIMIT_BYTES)


def _layer_norm(r, g, b):
    mu = jnp.mean(r, axis=-1, keepdims=True)
    c = r - mu
    var = jnp.mean(c * c, axis=-1, keepdims=True)
    return c * lax.rsqrt(var + LN_EPS) * g + b


def _dot(a, b):
    return jnp.dot(a, b, preferred_element_type=F32)


def _dot_nt(a, b):
    return lax.dot_general(a, b, (((1,), (1,)), ((), ())), preferred_element_type=F32)


def _head_lane_mask(width, head):
    lane = lax.broadcasted_iota(jnp.int32, (1, width), 1)
    return (lane >= head * HEAD_DIM) & (lane < (head + 1) * HEAD_DIM)


def _mem_attention(q_mem, mem_k, mem_v):
    out = jnp.zeros(q_mem.shape, F32)
    for head in range(MEM_HEADS):
        hm = _head_lane_mask(MEM_WIDTH, head)
        qh = jnp.where(hm, q_mem * ATTN_SCALE, 0.0).astype(BF16)
        s = _dot_nt(qh, mem_k)
        m = jnp.max(s, axis=-1, keepdims=True)
        p = jnp.exp(s - m)
        l = jnp.sum(p, axis=-1, keepdims=True)
        o = _dot(p.astype(BF16), mem_v) / l
        out = jnp.where(hm, o, out)
    return out


def _rope_tables(pos, freq, reps):
    angle = pos * freq
    cos = jnp.cos(angle)
    sin = jnp.sin(angle)
    return (jnp.concatenate([cos] * reps, axis=1), jnp.concatenate([sin] * reps, axis=1))


def _matmul_kernel(x_ref, w_ref, o_ref):
    o_ref[...] = _dot(x_ref[...].astype(BF16), w_ref[...]).astype(o_ref.dtype)


def _matmul(x, w, out_dtype, tile_m=256):
    m, k = x.shape
    n = w.shape[1]
    return pl.pallas_call(
        _matmul_kernel,
        grid=(m // tile_m,),
        in_specs=[pl.BlockSpec((tile_m, k), lambda i: (i, 0)),
                  pl.BlockSpec((k, n), lambda i: (0, 0))],
        out_specs=pl.BlockSpec((tile_m, n), lambda i: (i, 0)),
        out_shape=jax.ShapeDtypeStruct((m, n), out_dtype),
        compiler_params=_params(("arbitrary",)),
        name="mem_kv_proj",
    )(x, w)


def _mixer_a_kernel(x_ref, win_ref, conv_ref, mk_ref, mv_ref, wo_ref, g_ref, b_ref, o_ref,
                    ztail_ref, *, alpha):
    tile = x_ref.shape[1]

    @pl.when(pl.program_id(1) == 0)
    def _():
        ztail_ref[...] = jnp.zeros_like(ztail_ref)

    x = x_ref[0]
    proj = _dot(x.astype(BF16), win_ref[...])
    u_b = proj[:, :MIX_WIDTH]
    u_c = proj[:, MIX_WIDTH:2 * MIX_WIDTH]
    u_h = proj[:, 2 * MIX_WIDTH:3 * MIX_WIDTH]
    q_mem = proj[:, 3 * MIX_WIDTH:]

    z = u_c * u_h
    tail = ztail_ref[...]
    ztail_ref[...] = z[tile - 8:, :]
    row = lax.broadcasted_iota(jnp.int32, z.shape, 0)
    z1 = jnp.where(row == 0, tail[7:8, :], pltpu.roll(z, 1, 0))
    z2 = jnp.where(row == 0, tail[6:7, :], jnp.where(row == 1, tail[7:8, :], pltpu.roll(z, 2, 0)))
    w = conv_ref[...]
    main = u_b * (w[0:1, :] * z2 + w[1:2, :] * z1 + w[2:3, :] * z)

    mem_out = _mem_attention(q_mem, mk_ref[0], mv_ref[0])
    mix = (_dot(main.astype(BF16), wo_ref[:MIX_WIDTH, :])
           + _dot(mem_out.astype(BF16), wo_ref[MIX_WIDTH:, :]))
    o_ref[0] = _layer_norm(alpha * x + mix, g_ref[...], b_ref[...])


def _mixer_a(x, w_in, conv_w, mem_kv, layer, w_o, g, b, alpha):
    bsz, seq, d = x.shape
    tile = min(SEQ_TILE, seq)
    mem_len = mem_kv.shape[1]
    return pl.pallas_call(
        functools.partial(_mixer_a_kernel, alpha=alpha),
        grid=(bsz, seq // tile),
        in_specs=[
            pl.BlockSpec((1, tile, d), lambda bi, ti: (bi, ti, 0)),
            pl.BlockSpec(w_in.shape, lambda bi, ti: (0, 0)),
            pl.BlockSpec(conv_w.shape, lambda bi, ti: (0, 0)),
            pl.BlockSpec((1, mem_len, MEM_WIDTH), lambda bi, ti: (bi, 0, 2 * layer)),
            pl.BlockSpec((1, mem_len, MEM_WIDTH), lambda bi, ti: (bi, 0, 2 * layer + 1)),
            pl.BlockSpec(w_o.shape, lambda bi, ti: (0, 0)),
            pl.BlockSpec((1, d), lambda bi, ti: (0, 0)),
            pl.BlockSpec((1, d), lambda bi, ti: (0, 0)),
        ],
        out_specs=pl.BlockSpec((1, tile, d), lambda bi, ti: (bi, ti, 0)),
        out_shape=jax.ShapeDtypeStruct(x.shape, F32),
        scratch_shapes=[pltpu.VMEM((8, MIX_WIDTH), F32)],
        compiler_params=_params(("arbitrary", "arbitrary")),
        name="mixer_a",
    )(x, w_in, conv_w, mem_kv, mem_kv, w_o, g, b)


def _partial_rope(t, pos, freq):
    half = ROPE_DIM // 2
    cos, sin = _rope_tables(pos, freq, MIX_WIDTH // LANES)
    dim = lax.broadcasted_iota(jnp.int32, (1, MIX_WIDTH), 1) % HEAD_DIM
    upper = pltpu.roll(t, MIX_WIDTH - half, 1)
    lower = pltpu.roll(t, half, 1)
    return t * cos + jnp.where(dim < half, -upper, lower) * sin


def _shared_kv_kernel(x_ref, pos_ref, freq_ref, wk_ref, wv_ref, k_ref, v_ref, km_ref):
    xb = x_ref[0].astype(BF16)
    k = _partial_rope(_dot(xb, wk_ref[...]), pos_ref[0], freq_ref[...])
    k_ref[0] = k.astype(BF16)
    v_ref[0] = _dot(xb, wv_ref[...]).astype(BF16)
    km_ref[0, 0] = jnp.mean(k, axis=0, keepdims=True)


def _shared_kv(x, pos, freq, w_k, w_v):
    bsz, seq, d = x.shape
    n_blocks = seq // MOBA_BLOCK
    full = lambda bi, ti: (0, 0)
    return pl.pallas_call(
        _shared_kv_kernel,
        grid=(bsz, n_blocks),
        in_specs=[
            pl.BlockSpec((1, MOBA_BLOCK, d), lambda bi, ti: (bi, ti, 0)),
            pl.BlockSpec((1, MOBA_BLOCK, 1), lambda bi, ti: (bi, ti, 0)),
            pl.BlockSpec(freq.shape, full),
            pl.BlockSpec(w_k.shape, full),
            pl.BlockSpec(w_v.shape, full),
        ],
        out_specs=[
            pl.BlockSpec((1, MOBA_BLOCK, MIX_WIDTH), lambda bi, ti: (bi, ti, 0)),
            pl.BlockSpec((1, MOBA_BLOCK, MIX_WIDTH), lambda bi, ti: (bi, ti, 0)),
            pl.BlockSpec((1, 1, 1, MIX_WIDTH), lambda bi, ti: (bi, ti, 0, 0)),
        ],
        out_shape=[
            jax.ShapeDtypeStruct((bsz, seq, MIX_WIDTH), BF16),
            jax.ShapeDtypeStruct((bsz, seq, MIX_WIDTH), BF16),
            jax.ShapeDtypeStruct((bsz, n_blocks, 1, MIX_WIDTH), F32),
        ],
        compiler_params=_params(("arbitrary", "arbitrary")),
        name="shared_kv",
    )(x, pos, freq, w_k, w_v)


def _query_b_kernel(x_ref, pos_ref, freq_ref, wq_ref, mk_ref, mv_ref, q_ref, mo_ref):
    xb = x_ref[0].astype(BF16)
    proj = _dot(xb, wq_ref[...])
    q = _partial_rope(proj[:, :MIX_WIDTH], pos_ref[0], freq_ref[...])
    q_ref[0] = (q * (ATTN_SCALE * LOG2_E)).astype(BF16)
    mo_ref[0] = _mem_attention(proj[:, MIX_WIDTH:], mk_ref[0], mv_ref[0]).astype(BF16)


def _query_b(x, pos, freq, w_q, mem_kv, layer):
    bsz, seq, d = x.shape
    tile = min(LIGHT_SEQ_TILE, seq)
    mem_len = mem_kv.shape[1]
    full = lambda bi, ti: (0, 0)
    return pl.pallas_call(
        _query_b_kernel,
        grid=(bsz, seq // tile),
        in_specs=[
            pl.BlockSpec((1, tile, d), lambda bi, ti: (bi, ti, 0)),
            pl.BlockSpec((1, tile, 1), lambda bi, ti: (bi, ti, 0)),
            pl.BlockSpec(freq.shape, full),
            pl.BlockSpec(w_q.shape, full),
            pl.BlockSpec((1, mem_len, MEM_WIDTH), lambda bi, ti: (bi, 0, 2 * layer)),
            pl.BlockSpec((1, mem_len, MEM_WIDTH), lambda bi, ti: (bi, 0, 2 * layer + 1)),
        ],
        out_specs=[
            pl.BlockSpec((1, tile, MIX_WIDTH), lambda bi, ti: (bi, ti, 0)),
            pl.BlockSpec((1, tile, MEM_WIDTH), lambda bi, ti: (bi, ti, 0)),
        ],
        out_shape=[
            jax.ShapeDtypeStruct((bsz, seq, MIX_WIDTH), BF16),
            jax.ShapeDtypeStruct((bsz, seq, MEM_WIDTH), BF16),
        ],
        compiler_params=_params(("arbitrary", "arbitrary")),
        name="query_b",
    )(x, pos, freq, w_q, mem_kv, mem_kv)


def _moba_kernel(q_ref, k_ref, v_ref, km_ref, o_ref, kaug_ref, vaug_ref, m_ref, acc_ref, s0_ref,
                 s1_ref, *, chunk_blocks):
    blk = MOBA_BLOCK
    span = chunk_blocks * blk
    pair = pl.program_id(2)
    chains = [(u, h) for u in range(chunk_blocks) for h in range(HEADS_PER_LANE_GROUP)]
    rows = len(chains) * blk

    @pl.when(pair == 0)
    def _():
        kaug_ref[:, :LANES] = k_ref[0]
        vaug_ref[:, :LANES] = v_ref[0]
        lane = lax.broadcasted_iota(jnp.int32, (blk, LANES), 1)
        ones_lane = (lane == 0).astype(BF16)

        def fill(j, carry):
            off = pl.multiple_of(j * blk, blk)
            kaug_ref[pl.ds(off, blk), LANES:] = (lane == j).astype(BF16)
            vaug_ref[pl.ds(off, blk), LANES:] = ones_lane
            return carry

        lax.fori_loop(0, k_ref.shape[1] // blk, fill, 0)

    def chunk_rows(c):
        return pl.ds(pl.multiple_of(c * span, span), span)

    def sweep(half, sa_ref, sb_ref):
        own_chunk = 2 * pair + half
        q = q_ref[0, half * span:(half + 1) * span, :]
        q_heads = jnp.concatenate(
            [jnp.where(_head_lane_mask(LANES, h), q[u * blk:(u + 1) * blk], 0).astype(BF16)
             for u, h in chains], axis=0)

        block_id = lax.broadcasted_iota(jnp.int32, (rows, LANES), 1)
        own = (own_chunk * chunk_blocks
               + lax.broadcasted_iota(jnp.int32, (rows, LANES), 0) // (HEADS_PER_LANE_GROUP * blk))
        is_past = block_id < own
        gate = jnp.where(is_past, _dot_nt(q_heads, km_ref[0].astype(BF16)), -jnp.inf)
        kth = gate
        for _ in range(MOBA_TOPK - 1):
            kth = jnp.where(kth == jnp.max(kth, axis=-1, keepdims=True), -jnp.inf, kth)
        threshold = jnp.max(kth, axis=-1, keepdims=True)
        allowed = (is_past & (gate >= threshold)) | (block_id == own)
        block_bias = jnp.where(allowed, 0.0, MASK_VALUE).astype(BF16)
        q_aug = jnp.concatenate([q_heads, block_bias], axis=1)

        state_rows = [slice(half * rows + i * blk, half * rows + (i + 1) * blk)
                      for i in range(len(chains))]
        chain_rows = [slice(i * blk, (i + 1) * blk) for i in range(len(chains))]
        for r in state_rows:
            m_ref[r, :] = jnp.full((blk, 1), MASK_VALUE, F32)
            acc_ref[r, :] = jnp.zeros((blk, acc_ref.shape[1]), F32)

        def score_own(s_ref):
            for r, (u, _) in zip(chain_rows, chains):
                width = (u + 1) * blk
                k_part = kaug_ref[pl.ds(pl.multiple_of(own_chunk * span, span), width), :]
                k_pos = lax.broadcasted_iota(jnp.int32, (blk, width), 1)
                q_pos = u * blk + lax.broadcasted_iota(jnp.int32, (blk, width), 0)
                s_ref[r, :width] = jnp.where(k_pos <= q_pos, _dot_nt(q_aug[r], k_part), MASK_VALUE)
                if width < span:
                    s_ref[r, width:] = jnp.full((blk, span - width), MASK_VALUE, F32)

        def score_past(c, s_ref):
            k_chunk = kaug_ref[chunk_rows(c), :]
            for r in chain_rows:
                s_ref[r, :] = _dot_nt(q_aug[r], k_chunk)

        def softmax_value(c, s_ref):
            v_chunk = vaug_ref[chunk_rows(c), :]
            for r, sr in zip(chain_rows, state_rows):
                s = s_ref[r, :]
                m_old = m_ref[sr, :]
                m_new = jnp.maximum(m_old, jnp.max(s, axis=-1, keepdims=True))
                p = jnp.exp2((s - m_new).astype(BF16))
                acc_ref[sr, :] = jnp.exp2(m_old - m_new) * acc_ref[sr, :] + _dot(p, v_chunk)
                m_ref[sr, :] = m_new

        def step_chunk(j):
            return jnp.where(j == 0, own_chunk, j - 1)

        n_past = own_chunk
        score_own(sa_ref)

        def two_steps(i, carry):
            j = 2 * i + 1
            score_past(j - 1, sb_ref)
            softmax_value(step_chunk(j - 1), sa_ref)
            score_past(j, sa_ref)
            softmax_value(j - 1, sb_ref)
            return carry

        lax.fori_loop(0, pair, two_steps, 0)
        if half == 1:
            score_past(n_past - 1, sb_ref)
            softmax_value(step_chunk(n_past - 1), sa_ref)
            softmax_value(n_past - 1, sb_ref)
        else:
            softmax_value(step_chunk(n_past), sa_ref)

        for u in range(chunk_blocks):
            out = None
            for h in range(HEADS_PER_LANE_GROUP):
                acc = acc_ref[state_rows[u * HEADS_PER_LANE_GROUP + h], :]
                o = acc[:, :LANES] / acc[:, LANES:LANES + 1]
                out = o if out is None else jnp.where(_head_lane_mask(LANES, h), o, out)
            o_ref[0, half * span + u * blk:half * span + (u + 1) * blk, :] = out.astype(o_ref.dtype)

    sweep(0, s0_ref, s1_ref)
    sweep(1, s1_ref, s0_ref)


def _moba(q, k, v, k_means):
    bsz, seq, _ = q.shape
    n_blocks = seq // MOBA_BLOCK
    groups = MIX_WIDTH // LANES
    chunk_blocks = MOBA_CHUNK_BLOCKS
    while n_blocks % (2 * chunk_blocks):
        chunk_blocks //= 2
    assert chunk_blocks >= 1, "the sequence must hold an even number of MoBA blocks"
    span = chunk_blocks * MOBA_BLOCK
    rows = HEADS_PER_LANE_GROUP * span
    return pl.pallas_call(
        functools.partial(_moba_kernel, chunk_blocks=chunk_blocks),
        grid=(bsz, groups, seq // (2 * span)),
        in_specs=[
            pl.BlockSpec((1, 2 * span, LANES), lambda bi, gi, qi: (bi, qi, gi)),
            pl.BlockSpec((1, seq, LANES), lambda bi, gi, qi: (bi, 0, gi)),
            pl.BlockSpec((1, seq, LANES), lambda bi, gi, qi: (bi, 0, gi)),
            pl.BlockSpec((1, LANES, LANES), lambda bi, gi, qi: (bi, 0, gi)),
        ],
        out_specs=pl.BlockSpec((1, 2 * span, LANES), lambda bi, gi, qi: (bi, qi, gi)),
        out_shape=jax.ShapeDtypeStruct(q.shape, BF16),
        scratch_shapes=[pltpu.VMEM((seq, 2 * LANES), BF16), pltpu.VMEM((seq, 2 * LANES), BF16),
                        pltpu.VMEM((2 * rows, 1), F32), pltpu.VMEM((2 * rows, 2 * LANES), F32),
                        pltpu.VMEM((rows, span), F32), pltpu.VMEM((rows, span), F32)],
        compiler_params=_params(("arbitrary", "arbitrary", "arbitrary")),
        name="moba",
    )(q, k, v, k_means)


def _mixer_out_kernel(x_ref, main_ref, mo_ref, wo_ref, g_ref, b_ref, o_ref, *, alpha):
    mix = _dot(main_ref[0], wo_ref[:MIX_WIDTH, :]) + _dot(mo_ref[0], wo_ref[MIX_WIDTH:, :])
    o_ref[0] = _layer_norm(alpha * x_ref[0] + mix, g_ref[...], b_ref[...])


def _mixer_out(x, main, mem_out, w_o, g, b, alpha):
    bsz, seq, d = x.shape
    tile = min(LIGHT_SEQ_TILE, seq)
    full = lambda bi, ti: (0, 0)
    return pl.pallas_call(
        functools.partial(_mixer_out_kernel, alpha=alpha),
        grid=(bsz, seq // tile),
        in_specs=[
            pl.BlockSpec((1, tile, d), lambda bi, ti: (bi, ti, 0)),
            pl.BlockSpec((1, tile, MIX_WIDTH), lambda bi, ti: (bi, ti, 0)),
            pl.BlockSpec((1, tile, MEM_WIDTH), lambda bi, ti: (bi, ti, 0)),
            pl.BlockSpec(w_o.shape, full),
            pl.BlockSpec((1, d), full),
            pl.BlockSpec((1, d), full),
        ],
        out_specs=pl.BlockSpec((1, tile, d), lambda bi, ti: (bi, ti, 0)),
        out_shape=jax.ShapeDtypeStruct(x.shape, F32),
        compiler_params=_params(("arbitrary", "arbitrary")),
        name="mixer_out",
    )(x, main, mem_out, w_o, g, b)


def _swiglu_chunk(xb, wg, wu, wd):
    gp = _dot(xb, wg)
    up = _dot(xb, wu)
    return _dot((gp * jax.nn.sigmoid(gp) * up).astype(BF16), wd)


def _ffn_chunks(d_ff):
    chunk = FF_CHUNK if d_ff % FF_CHUNK == 0 else d_ff
    return chunk, d_ff // chunk


def _accumulate_chunks(c, n_chunks, part, acc_ref, finish):
    if n_chunks == 1:
        finish(part())
        return

    @pl.when(c == 0)
    def _():
        acc_ref[...] = part()

    if n_chunks > 2:
        @pl.when((c > 0) & (c < n_chunks - 1))
        def _():
            acc_ref[...] += part()

    @pl.when(c == n_chunks - 1)
    def _():
        finish(acc_ref[...] + part())


def _ffn_kernel(x_ref, wg_ref, wu_ref, wd_ref, g_ref, b_ref, o_ref, acc_ref, xb_ref, *, alpha,
                n_chunks):
    c = pl.program_id(1)

    @pl.when(c == 0)
    def _():
        xb_ref[...] = x_ref[...].astype(BF16)

    def finish(ffn):
        o_ref[...] = _layer_norm(alpha * x_ref[...] + ffn, g_ref[...], b_ref[...])

    def part():
        return _swiglu_chunk(xb_ref[...], wg_ref[...], wu_ref[...], wd_ref[...])

    _accumulate_chunks(c, n_chunks, part, acc_ref, finish)


def _ffn(x, w_gu, w_down, layer, g, b, alpha):
    n, d = x.shape
    d_ff = w_down.shape[1]
    tile = min(FFN_TILE, n)
    chunk, n_chunks = _ffn_chunks(d_ff)
    return pl.pallas_call(
        functools.partial(_ffn_kernel, alpha=alpha, n_chunks=n_chunks),
        grid=(n // tile, n_chunks),
        in_specs=[
            pl.BlockSpec((tile, d), lambda i, c: (i, 0)),
            pl.BlockSpec((None, d, chunk), lambda i, c: (layer, 0, c)),
            pl.BlockSpec((None, d, chunk), lambda i, c: (layer, 0, c + n_chunks)),
            pl.BlockSpec((None, chunk, d), lambda i, c: (layer, c, 0)),
            pl.BlockSpec((1, d), lambda i, c: (0, 0)),
            pl.BlockSpec((1, d), lambda i, c: (0, 0)),
        ],
        out_specs=pl.BlockSpec((tile, d), lambda i, c: (i, 0)),
        out_shape=jax.ShapeDtypeStruct(x.shape, F32),
        scratch_shapes=[pltpu.VMEM((tile, d), F32), pltpu.VMEM((tile, d), BF16)],
        compiler_params=_params(("arbitrary", "arbitrary")),
        name="ffn_dense",
    )(x, w_gu, w_gu, w_down, g, b)


META_EXPERT, META_RANK, META_GATE, META_ROWS = 0, 2, 4, 8
EXPERT_ROWS = 16


def _split_bf16(a):
    hi = a.astype(BF16)
    return hi, (a - hi.astype(F32)).astype(BF16)


def _route_kernel(x_ref, wr_ref, meta_ref, counts_ref, *, n_experts):
    tile = x_ref.shape[0]

    @pl.when(pl.program_id(0) == 0)
    def _():
        counts_ref[...] = jnp.zeros_like(counts_ref)

    x_hi, x_lo = _split_bf16(x_ref[...])
    w_hi, w_lo = _split_bf16(wr_ref[...])
    hi_part = _dot_nt(jnp.concatenate([w_hi, w_lo], axis=0), x_hi)
    logits = hi_part[:EXPERT_ROWS] + (hi_part[EXPERT_ROWS:] + _dot_nt(w_hi, x_lo))
    expert = lax.broadcasted_iota(jnp.int32, (EXPERT_ROWS, tile), 0)
    logits = jnp.where(expert < n_experts, logits, -jnp.inf)
    top1 = jnp.max(logits, axis=0, keepdims=True)
    idx1 = jnp.min(jnp.where(logits == top1, expert, EXPERT_ROWS), axis=0, keepdims=True)
    rest = jnp.where(expert == idx1, -jnp.inf, logits)
    top2 = jnp.max(rest, axis=0, keepdims=True)
    idx2 = jnp.min(jnp.where(rest == top2, expert, EXPERT_ROWS), axis=0, keepdims=True)
    e2 = jnp.exp(top2 - top1)

    chosen = (expert == idx1) | (expert == idx2)
    earlier = (lax.broadcasted_iota(jnp.int32, (tile, tile), 0)
               < lax.broadcasted_iota(jnp.int32, (tile, tile), 1))
    rank = counts_ref[...] + _dot(chosen.astype(BF16), earlier.astype(BF16))
    rank1 = jnp.sum(jnp.where(expert == idx1, rank, 0.0), axis=0, keepdims=True)
    rank2 = jnp.sum(jnp.where(expert == idx2, rank, 0.0), axis=0, keepdims=True)
    counts_ref[...] += jnp.sum(chosen.astype(F32), axis=1, keepdims=True)

    fields = [idx1.astype(F32), idx2.astype(F32), rank1, rank2, 1.0 / (1.0 + e2), e2 / (1.0 + e2)]
    row = lax.broadcasted_iota(jnp.int32, (META_ROWS, tile), 0)
    meta = jnp.zeros((META_ROWS, tile), F32)
    for i, field in enumerate(fields):
        meta = jnp.where(row == i, field, meta)
    meta_ref[...] = meta


def _route(x, w_router):
    n, d = x.shape
    tile = min(ROUTE_TILE, n)
    n_experts = w_router.shape[1]
    return pl.pallas_call(
        functools.partial(_route_kernel, n_experts=n_experts),
        grid=(n // tile,),
        in_specs=[pl.BlockSpec((tile, d), lambda i: (i, 0)),
                  pl.BlockSpec((EXPERT_ROWS, d), lambda i: (0, 0))],
        out_specs=[pl.BlockSpec((META_ROWS, tile), lambda i: (0, i)),
                   pl.BlockSpec((EXPERT_ROWS, 1), lambda i: (0, 0))],
        out_shape=[jax.ShapeDtypeStruct((META_ROWS, n), F32),
                   jax.ShapeDtypeStruct((EXPERT_ROWS, 1), F32)],
        compiler_params=_params(("arbitrary",)),
        name="moe_route",
    )(x, jnp.pad(w_router.T, ((0, EXPERT_ROWS - n_experts), (0, 0))))


def _row_copy(src_ref, src_row, dst_ref, dst_row, sem):
    return pltpu.make_async_copy(src_ref.at[pl.ds(src_row, 1), :], dst_ref.at[pl.ds(dst_row, 1), :], sem)


def _start_then_wait(copies, n):
    def start(r, carry):
        for cp in copies(r):
            cp.start()
        return carry

    def wait(r, carry):
        for cp in copies(r):
            cp.wait()
        return carry

    lax.fori_loop(0, n, start, 0, unroll=DMA_LOOP_UNROLL)
    lax.fori_loop(0, n, wait, 0, unroll=DMA_LOOP_UNROLL)


def _dispatch_kernel(pos_ref, x_ref, init_ref, xs_ref, sem):
    del init_ref

    def copies(r):
        return [_row_copy(x_ref, r, xs_ref, pos_ref[0, 0, TOP_K * r + k], sem) for k in range(TOP_K)]

    _start_then_wait(copies, x_ref.shape[0])


def _dispatch(x, pos, grouped_init):
    n, d = x.shape
    tile = min(DISPATCH_TILE, n)
    any_spec = pl.BlockSpec(memory_space=pl.ANY)
    return pl.pallas_call(
        _dispatch_kernel,
        grid=(n // tile,),
        in_specs=[pl.BlockSpec((1, 1, TOP_K * tile), lambda i: (i, 0, 0), memory_space=pltpu.SMEM),
                  pl.BlockSpec((tile, d), lambda i: (i, 0)), any_spec],
        out_specs=any_spec,
        out_shape=jax.ShapeDtypeStruct(grouped_init.shape, F32),
        scratch_shapes=[pltpu.SemaphoreType.DMA],
        input_output_aliases={2: 0},
        compiler_params=_params(("arbitrary",)),
        name="moe_dispatch",
    )(pos.reshape(n // tile, 1, TOP_K * tile), x, grouped_init)


def _expert_ffn_kernel(tile_expert_ref, n_tiles_ref, xs_ref, wg_ref, wu_ref, wd_ref, o_ref,
                       acc_ref, xb_ref, *, n_chunks):
    del tile_expert_ref
    c = pl.program_id(1)
    in_use = pl.program_id(0) < n_tiles_ref[0]

    @pl.when(jnp.logical_not(in_use) & (c == n_chunks - 1))
    def _():
        o_ref[...] = jnp.zeros_like(o_ref)

    @pl.when(in_use)
    def _():
        @pl.when(c == 0)
        def _():
            xb_ref[...] = xs_ref[...].astype(BF16)

        def finish(ffn):
            o_ref[...] = ffn

        def part():
            return _swiglu_chunk(xb_ref[...], wg_ref[0], wu_ref[0], wd_ref[0])

        _accumulate_chunks(c, n_chunks, part, acc_ref, finish)


def _expert_ffn(xs, tile_expert, n_tiles, w_gu, w_down, layer):
    n_rows, d = xs.shape
    d_ff = w_down.shape[2]
    chunk, n_chunks = _ffn_chunks(d_ff)
    row_tile = lambda t, c, te, nt: (jnp.minimum(t, nt[0] - 1), 0)
    grid_spec = pltpu.PrefetchScalarGridSpec(
        num_scalar_prefetch=2,
        grid=(n_rows // EXPERT_TILE, n_chunks),
        in_specs=[
            pl.BlockSpec((EXPERT_TILE, d), row_tile),
            pl.BlockSpec((None, 1, d, chunk), lambda t, c, te, nt: (layer, te[t], 0, c)),
            pl.BlockSpec((None, 1, d, chunk), lambda t, c, te, nt: (layer, te[t], 0, c + n_chunks)),
            pl.BlockSpec((None, 1, chunk, d), lambda t, c, te, nt: (layer, te[t], c, 0)),
        ],
        out_specs=pl.BlockSpec((EXPERT_TILE, d), lambda t, c, te, nt: (t, 0)),
        scratch_shapes=[pltpu.VMEM((EXPERT_TILE, d), F32), pltpu.VMEM((EXPERT_TILE, d), BF16)],
    )
    return pl.pallas_call(
        functools.partial(_expert_ffn_kernel, n_chunks=n_chunks),
        grid_spec=grid_spec,
        out_shape=jax.ShapeDtypeStruct((n_rows, d), F32),
        compiler_params=_params(("arbitrary", "arbitrary")),
        name="moe_experts",
    )(tile_expert, n_tiles, xs, w_gu, w_gu, w_down)


def _combine_kernel(pos_ref, gates_ref, x_ref, g_ref, b_ref, ys_ref, o_ref, rows_ref, sem, *, alpha):
    tile = x_ref.shape[0]

    def copies(r):
        return [_row_copy(ys_ref, pos_ref[0, 0, TOP_K * r + k], rows_ref.at[k], r, sem)
                for k in range(TOP_K)]

    _start_then_wait(copies, tile)

    gates = gates_ref[...]
    ffn = jnp.zeros(x_ref.shape, F32)
    for k in range(TOP_K):
        ffn = ffn + gates[:, k:k + 1] * rows_ref[k]
    o_ref[...] = _layer_norm(alpha * x_ref[...] + ffn, g_ref[...], b_ref[...])


def _combine(x, gates, pos, ys, g, b, alpha):
    n, d = x.shape
    tile = min(COMBINE_TILE, n)
    return pl.pallas_call(
        functools.partial(_combine_kernel, alpha=alpha),
        grid=(n // tile,),
        in_specs=[pl.BlockSpec((1, 1, TOP_K * tile), lambda i: (i, 0, 0), memory_space=pltpu.SMEM),
                  pl.BlockSpec((tile, TOP_K), lambda i: (i, 0)),
                  pl.BlockSpec((tile, d), lambda i: (i, 0)),
                  pl.BlockSpec((1, d), lambda i: (0, 0)),
                  pl.BlockSpec((1, d), lambda i: (0, 0)),
                  pl.BlockSpec(memory_space=pl.ANY)],
        out_specs=pl.BlockSpec((tile, d), lambda i: (i, 0)),
        out_shape=jax.ShapeDtypeStruct(x.shape, F32),
        scratch_shapes=[pltpu.VMEM((TOP_K, tile, d), F32), pltpu.SemaphoreType.DMA],
        compiler_params=_params(("arbitrary",)),
        name="moe_combine",
    )(pos.reshape(n // tile, 1, TOP_K * tile), gates, x, g, b, ys)


def _moe(x, w_router, w_gu, w_down, layer, g, b, alpha, grouped_init):
    n, d = x.shape
    n_experts = w_router.shape[1]
    meta, counts = _route(x, w_router)

    counts = counts[:n_experts, 0].astype(jnp.int32)
    group_tiles = (counts + EXPERT_TILE - 1) // EXPERT_TILE
    tiles_end = jnp.cumsum(group_tiles)
    group_row0 = (tiles_end - group_tiles) * EXPERT_TILE
    n_grid_tiles = (TOP_K * n) // EXPERT_TILE + n_experts
    tile_ids = jnp.arange(n_grid_tiles, dtype=jnp.int32)
    tile_expert = jnp.minimum(jnp.sum((tile_ids[:, None] >= tiles_end[None, :]).astype(jnp.int32), axis=1),
                              n_experts - 1)
    records = meta.T
    expert = records[:, META_EXPERT:META_EXPERT + TOP_K].astype(jnp.int32)
    rank = records[:, META_RANK:META_RANK + TOP_K].astype(jnp.int32)
    gates = records[:, META_GATE:META_GATE + TOP_K]
    pos = (group_row0[expert] + rank).reshape(-1)

    if grouped_init is None:
        grouped_init = jnp.zeros((n_grid_tiles * EXPERT_TILE, d), F32)
    xs = _dispatch(x, pos, grouped_init)
    ys = _expert_ffn(xs, tile_expert, tiles_end[-1:].astype(jnp.int32), w_gu, w_down, layer)
    return _combine(x, gates, pos, ys, g, b, alpha), xs


def _rope_lane_frequencies():
    inv_freq = ROPE_THETA ** (-jnp.arange(0, ROPE_DIM, 2, dtype=F32) / ROPE_DIM)
    c = jnp.arange(LANES) % HEAD_DIM
    return jnp.where(c < ROPE_DIM, inv_freq[c % (ROPE_DIM // 2)], 0.0).astype(F32)[None, :]


def kernel(x, mem, positions, w_in_a, conv_a, w_q_b, w_kv_shared, w_mem_kv, w_o, ln1_g, ln1_b,
           ln2_g, ln2_b, w_gu_dense, w_down_dense, w_router, w_gu_moe, w_down_moe):
    bsz, seq, d = x.shape
    depth = w_o.shape[0]
    n_a = w_in_a.shape[0]
    alpha = float((2 * depth) ** 0.25)
    assert seq % MOBA_BLOCK == 0 and seq // MOBA_BLOCK <= LANES

    pos = positions.astype(F32)[:, :, None]
    freq = _rope_lane_frequencies()
    mem_len = mem.shape[1]
    w_mem_all = jnp.transpose(w_mem_kv, (1, 0, 2)).reshape(d, depth * 2 * MEM_WIDTH).astype(BF16)
    mem_kv = _matmul(mem.reshape(bsz * mem_len, d), w_mem_all, BF16).reshape(bsz, mem_len, -1)

    w_gu_dense_bf16, w_down_dense_bf16 = w_gu_dense.astype(BF16), w_down_dense.astype(BF16)
    w_gu_moe_bf16, w_down_moe_bf16 = w_gu_moe.astype(BF16), w_down_moe.astype(BF16)
    shared = None
    grouped = None
    for layer in range(depth):
        g1, b1 = ln1_g[layer][None, :], ln1_b[layer][None, :]
        g2, b2 = ln2_g[layer][None, :], ln2_b[layer][None, :]
        wo = w_o[layer].astype(BF16)
        if layer < n_a:
            x = _mixer_a(x, w_in_a[layer].astype(BF16), conv_a[layer], mem_kv, layer, wo, g1, b1, alpha)
        else:
            wq = w_q_b[layer - n_a]
            q, mem_out = _query_b(x, pos, freq, wq.astype(BF16), mem_kv, layer)
            main = _moba(q, *shared)
            x = _mixer_out(x, main, mem_out, wo, g1, b1, alpha)
        xf = x.reshape(bsz * seq, d)
        if layer % 2 == 0:
            xf = _ffn(xf, w_gu_dense_bf16, w_down_dense_bf16, layer // 2, g2, b2, alpha)
        else:
            xf, grouped = _moe(xf, w_router[layer // 2], w_gu_moe_bf16, w_down_moe_bf16, layer // 2,
                               g2, b2, alpha, grouped)
        x = xf.reshape(bsz, seq, d)
        if layer == n_a - 1:
            wk = w_kv_shared[:, :MIX_WIDTH]
            k, v, km = _shared_kv(x, pos, freq, wk.astype(BF16), w_kv_shared[:, MIX_WIDTH:].astype(BF16))
            n_blocks = seq // MOBA_BLOCK
            km = jnp.pad(km.reshape(bsz, n_blocks, MIX_WIDTH), ((0, 0), (0, LANES - n_blocks), (0, 0)))
            shared = (k, v, km)
    return x
```
